```python
import math
import jax
import jax.numpy as jnp
from jax import lax
import numpy as np


D_MODEL = 1024
BATCH = 32
SEQ = 2048
DEPTH = 2

GRID_W = 64
CTX_LEN = 256
N_EVEN = (DEPTH + 1) // 2
N_ODD = DEPTH // 2
DEEPNORM_ALPHA = (2.0 * DEPTH) ** 0.25
DEEPNORM_BETA = (8.0 * DEPTH) ** -0.25
LN_EPS = 1e-5
RMS_EPS = 1e-6
ROPE_BASE = 10000.0

B_HEADS = 4
B_HEAD_DIM = 128
B_WIDTH = B_HEADS * B_HEAD_DIM
B_CONV = 3
B_CHUNK = 64
A_WIDTH = D_MODEL - B_WIDTH
A_CONV = 3
C_Q_HEADS = 8
C_KV_HEADS = 2
C_HEAD_DIM = 64
C_WINDOW = 128
D_HEADS = 8
D_NOPE = 64
D_ROPE = 32
D_V = 64
D_Q_RANK = 384
D_KV_RANK = 256
Q_BLOCK = 128
N_EXPERTS = 16
N_GROUPS = 4
TOP_K = 2
D_EXPERT = 512

EVEN_COLS = (A_WIDTH, A_WIDTH, A_WIDTH, B_WIDTH, B_WIDTH, B_WIDTH, B_WIDTH, 2 * B_HEADS, 2 * B_HEADS)
ODD_COLS = (C_Q_HEADS * C_HEAD_DIM, C_KV_HEADS * C_HEAD_DIM, C_KV_HEADS * C_HEAD_DIM, D_Q_RANK, D_KV_RANK, D_ROPE)
EVEN_IN = sum(EVEN_COLS)
ODD_IN = sum(ODD_COLS)
EVEN_OUT = A_WIDTH + B_WIDTH
ODD_OUT = C_Q_HEADS * C_HEAD_DIM + D_HEADS * D_V

kernel_name = 'hybrid_prefix_diffusion_block'

F32 = jnp.float32


def layer_norm(x, g, b):
    xf = x.astype(F32)
    mu = jnp.mean(xf, -1, keepdims=True)
    var = jnp.mean(jnp.square(xf - mu), -1, keepdims=True)
    return ((xf - mu) * lax.rsqrt(var + LN_EPS) * g.astype(F32) + b.astype(F32)).astype(x.dtype)


def rms_norm(x, g):
    xf = x.astype(F32)
    y = xf * lax.rsqrt(jnp.mean(jnp.square(xf), -1, keepdims=True) + RMS_EPS)
    return (y * g.astype(F32)).astype(x.dtype)


def l2_normalize(x):
    return x * lax.rsqrt(jnp.sum(jnp.square(x), -1, keepdims=True) + 1e-6)


def modulate(x, shift, scale):
    return x * (1 + scale) + shift


def split_cols(p, sizes):
    offsets = [int(o) for o in np.cumsum(sizes)[:-1]]
    return jnp.split(p, offsets, axis=-1)


def flip_seq(t, rev):
    return jnp.flip(t, axis=1) if rev else t


def depthwise_conv(x, w):
    width = w.shape[0]
    return lax.conv_general_dilated(
        x, w[:, None, :].astype(x.dtype), window_strides=(1,), padding=[(width // 2, width // 2)],
        dimension_numbers=('NWC', 'WIO', 'NWC'), feature_group_count=x.shape[-1])


def axial_rope(n_tokens, rot_dim, dtype):
    n_rows = n_tokens // GRID_W
    t = jnp.arange(n_rows * GRID_W)
    rows = (t // GRID_W).astype(F32)
    cols = (t % GRID_W).astype(F32)
    n_freq = rot_dim // 4
    inv_freq = ROPE_BASE ** (-jnp.arange(n_freq, dtype=F32) / n_freq)
    ang = jnp.concatenate([rows[:, None] * inv_freq, cols[:, None] * inv_freq], -1)
    return jnp.cos(ang).astype(dtype), jnp.sin(ang).astype(dtype)


def apply_rope(x, cos, sin):
    bshape = cos.shape[:1] + (1,) * (x.ndim - 3) + cos.shape[1:]
    cos, sin = cos.reshape(bshape), sin.reshape(bshape)
    x1, x2 = jnp.split(x, 2, axis=-1)
    return jnp.concatenate([x1 * cos - x2 * sin, x2 * cos + x1 * sin], -1)


def gated_delta_chunked(q, k, v, beta, g, state):
    bsz, seq, heads, dk = q.shape
    dv = v.shape[-1]
    cs = B_CHUNK
    n_chunks = seq // cs

    def chunks(t):
        t = t.reshape((bsz, n_chunks, cs, heads) + t.shape[3:])
        return jnp.moveaxis(jnp.moveaxis(t, 3, 2), 1, 0)

    q = chunks(q) * dk ** -0.5
    k = chunks(k)
    v = chunks(v)
    beta = chunks(beta)
    gc = jnp.cumsum(chunks(g), axis=-1)
    idx = jnp.arange(cs)
    lower = idx[:, None] >= idx[None, :]
    strict = idx[:, None] > idx[None, :]
    decay = jnp.exp(jnp.where(lower, gc[..., :, None] - gc[..., None, :], -jnp.inf))
    kb = k * beta[..., None]
    a = jnp.where(strict, jnp.einsum('nbhid,nbhjd->nbhij', kb, k) * decay, 0.0)
    m = a + jnp.eye(cs, dtype=a.dtype)
    u = lax.linalg.triangular_solve(m, v * beta[..., None], left_side=True, lower=True, unit_diagonal=True)
    w = lax.linalg.triangular_solve(m, kb * jnp.exp(gc)[..., None], left_side=True, lower=True, unit_diagonal=True)
    attn = jnp.einsum('nbhid,nbhjd->nbhij', q, k) * decay

    def step(s, xs):
        q_n, k_n, u_n, w_n, gc_n, attn_n = xs
        v_new = u_n - jnp.einsum('bhcd,bhde->bhce', w_n, s)
        o_n = (jnp.einsum('bhcd,bhde->bhce', q_n * jnp.exp(gc_n)[..., None], s)
               + jnp.einsum('bhij,bhje->bhie', attn_n, v_new))
        g_last = gc_n[..., -1:]
        s = (s * jnp.exp(g_last)[..., None]
             + jnp.einsum('bhcd,bhce->bhde', k_n * jnp.exp(g_last - gc_n)[..., None], v_new))
        return s, o_n

    state, o = lax.scan(step, state, (q, k, u, w, gc, attn))
    o = jnp.swapaxes(jnp.moveaxis(o, 0, 1), 2, 3).reshape(bsz, seq, heads, dv)
    return o, state


def softmax_with_sink(scores, sink):
    g, r = scores.shape[1], scores.shape[2]
    sink_col = jnp.broadcast_to(sink.astype(F32).reshape(1, g, r, 1, 1), scores.shape[:-1] + (1,))
    p = jax.nn.softmax(jnp.concatenate([scores, sink_col], -1), axis=-1)
    return p[..., :-1]


def window_attention(q, k, v, k_ctx, v_ctx, sink):
    bsz, seq, g, r, d = q.shape
    wdw = C_WINDOW
    n_blk = seq // wdw
    scale = d ** -0.5
    kp = jnp.pad(k, ((0, 0), (wdw, wdw), (0, 0), (0, 0)))
    vp = jnp.pad(v, ((0, 0), (wdw, wdw), (0, 0), (0, 0)))
    qb = jnp.moveaxis(q.reshape(bsz, n_blk, wdw, g, r, d), 1, 0)
    rel = jnp.arange(3 * wdw)[None, :] - wdw - jnp.arange(wdw)[:, None]
    near = jnp.abs(rel) <= wdw
    n_local = 3 * wdw

    def block(args):
        q_blk, i = args
        k_blk = lax.dynamic_slice_in_dim(kp, i * wdw, n_local, axis=1)
        v_blk = lax.dynamic_slice_in_dim(vp, i * wdw, n_local, axis=1)
        kpos = (i - 1) * wdw + jnp.arange(n_local)
        valid = near & ((kpos >= 0) & (kpos < seq))[None, :]
        s_loc = jnp.einsum('bqgrd,bkgd->bgrqk', q_blk, k_blk).astype(F32) * scale
        s_loc = jnp.where(valid, s_loc, -jnp.inf)
        s_ctx = jnp.einsum('bqgrd,bcgd->bgrqc', q_blk, k_ctx).astype(F32) * scale
        p = softmax_with_sink(jnp.concatenate([s_loc, s_ctx], -1), sink).astype(v.dtype)
        return (jnp.einsum('bgrqk,bkgd->bqgrd', p[..., :n_local], v_blk)
                + jnp.einsum('bgrqc,bcgd->bqgrd', p[..., n_local:], v_ctx))

    o = lax.map(block, (qb, jnp.arange(n_blk)))
    return jnp.moveaxis(o, 0, 1).reshape(bsz, seq, g * r * d)


def context_gqa(q, k, v, sink):
    s = jnp.einsum('bqgrd,bkgd->bgrqk', q, k).astype(F32) * C_HEAD_DIM ** -0.5
    p = softmax_with_sink(s, sink).astype(v.dtype)
    o = jnp.einsum('bgrqk,bkgd->bqgrd', p, v)
    return o.reshape(o.shape[0], o.shape[1], -1)


def latent_attention(q_nope, q_rope, k_nope, k_rope, v, q_block):
    bsz, lq, heads, dn = q_nope.shape
    dr = q_rope.shape[-1]
    n_blk = lq // q_block
    scale = (D_NOPE + D_ROPE) ** -0.5
    qn_b = jnp.moveaxis(q_nope.reshape(bsz, n_blk, q_block, heads, dn), 1, 0)
    qr_b = jnp.moveaxis(q_rope.reshape(bsz, n_blk, q_block, heads, dr), 1, 0)

    def block(args):
        qn, qr = args
        s = (jnp.einsum('bqhd,bkhd->bhqk', qn, k_nope)
             + jnp.einsum('bqhd,bkd->bhqk', qr, k_rope)).astype(F32) * scale
        p = jax.nn.softmax(s, axis=-1).astype(v.dtype)
        return jnp.einsum('bhqk,bkhd->bqhd', p, v)

    o = lax.map(block, (qn_b, qr_b))
    return jnp.moveaxis(o, 0, 1).reshape(bsz, lq, heads * v.shape[-1])


def even_mixer(hl, hc, w_in, a_conv, b_conv, b_alog, b_dtbias, b_norm, w_out, ctx_out):
    pl = split_cols(hl @ w_in, EVEN_COLS)
    pc = split_cols(hc @ w_in, EVEN_COLS)

    def short_conv(p):
        return p[0] * depthwise_conv(p[1] * p[2], a_conv)

    def delta_inputs(p):
        q, k, v, beta_logit, a = p[3], p[4], p[5], p[7], p[8]
        bsz, n = q.shape[:2]
        qkv = jax.nn.silu(depthwise_conv(jnp.concatenate([q, k, v], -1), b_conv)).astype(F32)
        q, k, v = [t.reshape(bsz, n, B_HEADS, B_HEAD_DIM) for t in jnp.split(qkv, 3, -1)]
        beta = jax.nn.sigmoid(beta_logit.astype(F32)).reshape(bsz, n, 2, B_HEADS)
        g = -jnp.exp(b_alog.astype(F32)) * jax.nn.softplus(
            a.astype(F32).reshape(bsz, n, 2, B_HEADS) + b_dtbias.astype(F32))
        return l2_normalize(q), l2_normalize(k), v, beta, g

    def delta_output(o, p):
        bsz, n = o.shape[:2]
        gate = jax.nn.silu(p[6].astype(F32)).reshape(bsz, n, B_HEADS, B_HEAD_DIM)
        return (rms_norm(o, b_norm) * gate).reshape(bsz, n, B_WIDTH).astype(hl.dtype)

    ql, kl, vl, beta_l, g_l = delta_inputs(pl)
    qc, kc, vc, beta_c, g_c = delta_inputs(pc)
    state0 = jnp.zeros((hl.shape[0], B_HEADS, B_HEAD_DIM, B_HEAD_DIM), F32)
    o_l = jnp.zeros(vl.shape, F32)
    o_c = jnp.zeros(vc.shape, F32)
    for direction in range(2):
        rev = direction == 1
        oc_d, s_ctx = gated_delta_chunked(
            flip_seq(qc, rev), flip_seq(kc, rev), flip_seq(vc, rev),
            flip_seq(beta_c[:, :, direction], rev), flip_seq(g_c[:, :, direction], rev), state0)
        ol_d, _ = gated_delta_chunked(
            flip_seq(ql, rev), flip_seq(kl, rev), flip_seq(vl, rev),
            flip_seq(beta_l[:, :, direction], rev), flip_seq(g_l[:, :, direction], rev), s_ctx)
        o_l = o_l + flip_seq(ol_d, rev)
        o_c = o_c + flip_seq(oc_d, rev)
    yl = jnp.concatenate([short_conv(pl), delta_output(o_l, pl)], -1) @ w_out
    yc = jnp.concatenate([short_conv(pc), delta_output(o_c, pc)], -1) @ w_out if ctx_out else None
    return yl, yc


def odd_mixer(hl, hc, w_in, c_sink, d_qnorm, d_kvnorm, d_wuq, d_wukv, w_out, ctx_out):
    def project(h, positional):
        bsz, n = h.shape[:2]
        cq, ck, cv, dq, dkv, k_rope = split_cols(h @ w_in, ODD_COLS)
        cq = cq.reshape(bsz, n, C_KV_HEADS, C_Q_HEADS // C_KV_HEADS, C_HEAD_DIM)
        ck = ck.reshape(bsz, n, C_KV_HEADS, C_HEAD_DIM)
        cv = cv.reshape(bsz, n, C_KV_HEADS, C_HEAD_DIM)
        q = (rms_norm(dq, d_qnorm) @ d_wuq).reshape(bsz, n, D_HEADS, D_NOPE + D_ROPE)
        kv = (rms_norm(dkv, d_kvnorm) @ d_wukv).reshape(bsz, n, D_HEADS, D_NOPE + D_V)
        q_nope, q_rope = q[..., :D_NOPE], q[..., D_NOPE:]
        k_nope, v = kv[..., :D_NOPE], kv[..., D_NOPE:]
        if positional:
            cos, sin = axial_rope(n, C_HEAD_DIM, h.dtype)
            cq, ck = apply_rope(cq, cos, sin), apply_rope(ck, cos, sin)
            cos, sin = axial_rope(n, D_ROPE, h.dtype)
            q_rope, k_rope = apply_rope(q_rope, cos, sin), apply_rope(k_rope, cos, sin)
        return cq, ck, cv, q_nope, q_rope, k_nope, k_rope, v

    cq_l, ck_l, cv_l, qn_l, qr_l, kn_l, kr_l, v_l = project(hl, True)
    cq_c, ck_c, cv_c, qn_c, qr_c, kn_c, kr_c, v_c = project(hc, False)
    y_win = window_attention(cq_l, ck_l, cv_l, ck_c, cv_c, c_sink)
    y_mla = latent_attention(qn_l, qr_l, jnp.concatenate([kn_c, kn_l], 1),
                             jnp.concatenate([kr_c, kr_l], 1), jnp.concatenate([v_c, v_l], 1), Q_BLOCK)
    yl = jnp.concatenate([y_win, y_mla], -1) @ w_out
    if ctx_out:
        yc_win = context_gqa(cq_c, ck_c, cv_c, c_sink)
        yc_mla = latent_attention(qn_c, qr_c, kn_c, kr_c, v_c, hc.shape[1])
        yc = jnp.concatenate([yc_win, yc_mla], -1) @ w_out
    else:
        yc = None
    return yl, yc


def moe(h, router_w, router_bias, w_gate, w_up, w_down):
    affinity = jax.nn.sigmoid(jnp.einsum('bld,de->ble', h, router_w).astype(F32))
    sel = affinity + router_bias.astype(F32)
    per_group = N_EXPERTS // N_GROUPS
    grp = sel.reshape(sel.shape[:-1] + (N_GROUPS, per_group))
    group_score = jnp.sum(lax.top_k(grp, 2)[0], -1)
    best = jnp.argmax(group_score, -1)
    in_group = (jnp.arange(N_EXPERTS) // per_group) == best[..., None]
    _, idx = lax.top_k(jnp.where(in_group, sel, -jnp.inf), TOP_K)
    wts = jnp.take_along_axis(affinity, idx, -1)
    wts = wts / jnp.sum(wts, -1, keepdims=True)
    gates = jnp.sum(jax.nn.one_hot(idx, N_EXPERTS, dtype=F32) * wts[..., None], -2).astype(h.dtype)
    out = jnp.zeros_like(h)
    for e in range(N_EXPERTS):
        act = jax.nn.silu(h @ w_gate[e]) * (h @ w_up[e])
        out = out + gates[..., e:e + 1] * (act @ w_down[e])
    return out


def setup_inputs(seed: int = 0) -> dict:
    key = jax.random.key(seed)
    ks = jax.random.split(key, 27)
    d = D_MODEL

    def nrm(i, shape, scale):
        return jax.random.normal(ks[i], shape, F32) * scale

    dt = jnp.exp(jax.random.uniform(ks[12], (N_EVEN, 2, B_HEADS), F32, math.log(1e-3), math.log(1e-1)))
    return {
        'x': nrm(0, (BATCH, SEQ, d), 1.0),
        'c': nrm(1, (BATCH, d), 1.0),
        'ctx': nrm(2, (BATCH, CTX_LEN, d), 1.0),
        'c_ctx': nrm(3, (d,), 1.0),
        'ada_w': nrm(4, (DEPTH, d, 6 * d), 0.5 * d ** -0.5),
        'ada_b': nrm(5, (DEPTH, 6 * d), 0.02),
        'ln_g': 1.0 + nrm(6, (DEPTH, 2, d), 0.02),
        'ln_b': nrm(7, (DEPTH, 2, d), 0.02),
        'ev_w_in': nrm(8, (N_EVEN, d, EVEN_IN), d ** -0.5),
        'ev_a_conv': nrm(9, (N_EVEN, A_CONV, A_WIDTH), A_CONV ** -0.5),
        'ev_b_conv': nrm(10, (N_EVEN, B_CONV, 3 * B_WIDTH), B_CONV ** -0.5),
        'ev_b_alog': jnp.log(jax.random.uniform(ks[11], (N_EVEN, 2, B_HEADS), F32, 1.0, 16.0)),
        'ev_b_dtbias': dt + jnp.log(-jnp.expm1(-dt)),
        'ev_b_norm': 1.0 + nrm(13, (N_EVEN, B_HEAD_DIM), 0.02),
        'ev_w_out': nrm(14, (N_EVEN, EVEN_OUT, d), DEEPNORM_BETA * EVEN_OUT ** -0.5),
        'od_w_in': nrm(15, (N_ODD, d, ODD_IN), d ** -0.5),
        'od_c_sink': nrm(16, (N_ODD, C_Q_HEADS), 1.0),
        'od_d_qnorm': 1.0 + nrm(17, (N_ODD, D_Q_RANK), 0.02),
        'od_d_kvnorm': 1.0 + nrm(18, (N_ODD, D_KV_RANK), 0.02),
        'od_d_wuq': nrm(19, (N_ODD, D_Q_RANK, D_HEADS * (D_NOPE + D_ROPE)), D_Q_RANK ** -0.5),
        'od_d_wukv': nrm(20, (N_ODD, D_KV_RANK, D_HEADS * (D_NOPE + D_V)), D_KV_RANK ** -0.5),
        'od_w_out': nrm(21, (N_ODD, ODD_OUT, d), DEEPNORM_BETA * ODD_OUT ** -0.5),
        'router_w': nrm(22, (d, N_EXPERTS), d ** -0.5),
        'router_bias': nrm(23, (N_EXPERTS,), 0.01),
        'moe_w_gate': nrm(24, (DEPTH, N_EXPERTS, d, D_EXPERT), d ** -0.5),
        'moe_w_up': nrm(25, (DEPTH, N_EXPERTS, d, D_EXPERT), d ** -0.5),
        'moe_w_down': nrm(26, (DEPTH, N_EXPERTS, D_EXPERT, d), DEEPNORM_BETA * D_EXPERT ** -0.5),
    }


def reference(x, c, ctx, c_ctx, ada_w, ada_b, ln_g, ln_b, ev_w_in, ev_a_conv, ev_b_conv, ev_b_alog,
              ev_b_dtbias, ev_b_norm, ev_w_out, od_w_in, od_c_sink, od_d_qnorm, od_d_kvnorm, od_d_wuq,
              od_d_wukv, od_w_out, router_w, router_bias, moe_w_gate, moe_w_up, moe_w_down):
    xl, xc = x, ctx
    for layer in range(DEPTH):
        last = layer == DEPTH - 1
        j = layer // 2
        mod_l = (jax.nn.silu(c) @ ada_w[layer] + ada_b[layer])[:, None, :]
        mod_c = (jax.nn.silu(c_ctx) @ ada_w[layer] + ada_b[layer])[None, None, :]
        sh1_l, sc1_l, g1_l, sh2_l, sc2_l, g2_l = jnp.split(mod_l, 6, -1)
        sh1_c, sc1_c, g1_c, sh2_c, sc2_c, g2_c = jnp.split(mod_c, 6, -1)
        hl = modulate(xl, sh1_l, sc1_l)
        hc = modulate(xc, sh1_c, sc1_c)
        if layer % 2 == 0:
            yl, yc = even_mixer(hl, hc, ev_w_in[j], ev_a_conv[j], ev_b_conv[j], ev_b_alog[j],
                                ev_b_dtbias[j], ev_b_norm[j], ev_w_out[j], not last)
        else:
            yl, yc = odd_mixer(hl, hc, od_w_in[j], od_c_sink[j], od_d_qnorm[j], od_d_kvnorm[j],
                               od_d_wuq[j], od_d_wukv[j], od_w_out[j], not last)
        xl = layer_norm(DEEPNORM_ALPHA * xl + g1_l * yl, ln_g[layer, 0], ln_b[layer, 0])
        hl = modulate(xl, sh2_l, sc2_l)
        if last:
            f_l = moe(hl, router_w, router_bias, moe_w_gate[layer], moe_w_up[layer], moe_w_down[layer])
        else:
            xc = layer_norm(DEEPNORM_ALPHA * xc + g1_c * yc, ln_g[layer, 0], ln_b[layer, 0])
            hc = modulate(xc, sh2_c, sc2_c)
            n_ctx = xc.shape[1]
            f_all = moe(jnp.concatenate([hc, hl], 1), router_w, router_bias,
                        moe_w_gate[layer], moe_w_up[layer], moe_w_down[layer])
            f_l = f_all[:, n_ctx:]
            xc = layer_norm(DEEPNORM_ALPHA * xc + g2_c * f_all[:, :n_ctx], ln_g[layer, 1], ln_b[layer, 1])
        xl = layer_norm(DEEPNORM_ALPHA * xl + g2_l * f_l, ln_g[layer, 1], ln_b[layer, 1])
    return xl
```

```python
import functools

import numpy as np
import jax
import jax.numpy as jnp
from jax import lax
from jax.experimental import pallas as pl
from jax.experimental.pallas import tpu as pltpu

F32 = jnp.float32
BF16 = jnp.bfloat16
HIGHEST = lax.Precision.HIGHEST

DEPTH = 2
GRID_W = 64
DEEPNORM_ALPHA = (2.0 * DEPTH) ** 0.25
LN_EPS = 1e-5
RMS_EPS = 1e-6
ROPE_BASE = 10000.0
B_HEADS = 4
B_HEAD_DIM = 128
B_WIDTH = 512
A_WIDTH = 512
C_Q_HEADS = 8
C_KV_HEADS = 2
C_HEAD_DIM = 64
C_WINDOW = 128
D_HEADS = 8
D_NOPE = 64
D_ROPE = 32
D_V = 64
D_Q_RANK = 384
D_KV_RANK = 256
N_EXPERTS = 16
N_GROUPS = 4
PER_GROUP = N_EXPERTS // N_GROUPS
D_EXPERT = 512

LANES = 128
TM = 256
GDN_CHUNK = 64
MOE_TM = 512
VMEM_LIMIT = 56 * 1024 * 1024


def _cparams(n_axes, vmem=VMEM_LIMIT):
    return pltpu.CompilerParams(dimension_semantics=("arbitrary",) * n_axes, vmem_limit_bytes=vmem)


def _sigmoid(x):
    return 1.0 / (1.0 + jnp.exp(-x))


def _silu(x):
    return x * _sigmoid(x)


def _softplus(x):
    return jnp.maximum(x, 0.0) + jnp.log(1.0 + jnp.exp(-jnp.abs(x)))


def _layer_norm(v, g, b):
    mu = jnp.mean(v, axis=-1, keepdims=True)
    d = v - mu
    var = jnp.mean(d * d, axis=-1, keepdims=True)
    return d * lax.rsqrt(var + LN_EPS) * g + b


def _ada_kernel(c_ref, w_ref, b_ref, o_ref):
    s = _silu(c_ref[...])
    o_ref[...] = jnp.dot(s, w_ref[...], precision=HIGHEST, preferred_element_type=F32) + b_ref[...]


def _ada_mod(cs, ada_w, ada_b):
    depth, d, n6 = ada_w.shape
    rows = cs.shape[0]
    tn = 1536
    return pl.pallas_call(
        _ada_kernel,
        grid=(depth, n6 // tn),
        in_specs=[
            pl.BlockSpec((rows, d), lambda l, n: (0, 0)),
            pl.BlockSpec((None, d, tn), lambda l, n: (l, 0, n)),
            pl.BlockSpec((None, 1, tn), lambda l, n: (l, 0, n)),
        ],
        out_specs=pl.BlockSpec((None, rows, tn), lambda l, n: (l, 0, n)),
        out_shape=jax.ShapeDtypeStruct((depth, rows, n6), F32),
        compiler_params=_cparams(2),
        name="ada_mod",
    )(cs, ada_w, ada_b.reshape(depth, 1, n6))


def _inproj_even_kernel(x_ref, mod_ref, wm_ref, ws_ref, p_ref, s_ref):
    mod = mod_ref[...]
    h = (x_ref[...] * (1.0 + mod[1:2]) + mod[0:1]).astype(BF16)
    p_ref[...] = jnp.dot(h, wm_ref[...], preferred_element_type=F32).astype(BF16)
    s_ref[...] = jnp.dot(h, ws_ref[...], preferred_element_type=F32)


def _inproj_even(xcat, modarr, w_main, w_small):
    bsz, t, d = xcat.shape
    nm = w_main.shape[1]
    return pl.pallas_call(
        _inproj_even_kernel,
        grid=(bsz, t // TM),
        in_specs=[
            pl.BlockSpec((None, TM, d), lambda b, j: (b, j, 0)),
            pl.BlockSpec((None, None, 6, d), lambda b, j: (b, jnp.minimum(j, 1), 0, 0)),
            pl.BlockSpec((d, nm), lambda b, j: (0, 0)),
            pl.BlockSpec((d, LANES), lambda b, j: (0, 0)),
        ],
        out_specs=[
            pl.BlockSpec((None, TM, nm), lambda b, j: (b, j, 0)),
            pl.BlockSpec((None, TM, LANES), lambda b, j: (b, j, 0)),
        ],
        out_shape=[
            jax.ShapeDtypeStruct((bsz, t, nm), BF16),
            jax.ShapeDtypeStruct((bsz, t, LANES), F32),
        ],
        compiler_params=_cparams(2),
        name="inproj_even",
    )(xcat, modarr, w_main, w_small)


HALO = 16


def _conv3(z, zp, zn, w):
    n = z.shape[0]
    rows = lax.broadcasted_iota(jnp.int32, z.shape, 0)
    zprev = jnp.where(rows == 0, zp, pltpu.roll(z, 1, 0))
    znext = jnp.where(rows == n - 1, zn, pltpu.roll(z, n - 1, 0))
    return w[0:1] * zprev + w[1:2] * z + w[2:3] * znext


def _even_prep_kernel(p_ref, pp_ref, pn_ref, s_ref, aw_ref, bw_ref, alog_ref, dtb_ref,
                      ya_ref, qkv_ref, aux_ref, auxt_ref, *, chunk):
    j = pl.program_id(1)
    nj = pl.num_programs(1)
    prev_on = jnp.where(jnp.logical_and(j != 0, j != 1), 1.0, 0.0)
    next_on = jnp.where(jnp.logical_and(j != 0, j != nj - 1), 1.0, 0.0)
    prow = pp_ref[...].astype(F32)[HALO - 1:HALO] * prev_on
    nrow = pn_ref[...].astype(F32)[0:1] * next_on

    def seg(lo, hi):
        return p_ref[:, lo:hi].astype(F32), prow[:, lo:hi], nrow[:, lo:hi]

    a0, _, _ = seg(0, A_WIDTH)
    a1, a1p, a1n = seg(A_WIDTH, 2 * A_WIDTH)
    a2, a2p, a2n = seg(2 * A_WIDTH, 3 * A_WIDTH)
    ya_ref[...] = (a0 * _conv3(a1 * a2, a1p * a2p, a1n * a2n, aw_ref[...])).astype(BF16)

    base = 3 * A_WIDTH
    for which in range(3):
        lo = base + which * B_WIDTH
        z, zp, zn = seg(lo, lo + B_WIDTH)
        c = _silu(_conv3(z, zp, zn, bw_ref[:, which * B_WIDTH:(which + 1) * B_WIDTH]))
        for h in range(B_HEADS):
            ch = c[:, h * B_HEAD_DIM:(h + 1) * B_HEAD_DIM]
            if which < 2:
                ss = jnp.sum(ch * ch, axis=-1, keepdims=True)
                ch = ch * lax.rsqrt(ss + 1e-6)
                if which == 0:
                    ch = ch * (B_HEAD_DIM ** -0.5)
            col = which * B_WIDTH + h * B_HEAD_DIM
            qkv_ref[:, col:col + B_HEAD_DIM] = ch.astype(BF16)

    s = s_ref[...]
    beta = _sigmoid(s)
    g = -jnp.exp(alog_ref[...]) * _softplus(s + dtb_ref[...])
    n = s.shape[0]
    ri = lax.broadcasted_iota(jnp.int32, (n, n), 0)
    ci = lax.broadcasted_iota(jnp.int32, (n, n), 1)
    same = (ri // chunk) == (ci // chunk)
    m_fwd = jnp.where(jnp.logical_and(same, ci <= ri), 1.0, 0.0)
    m_rev = jnp.where(jnp.logical_and(same, ci >= ri), 1.0, 0.0)
    gc_f = jnp.dot(m_fwd, g, precision=HIGHEST, preferred_element_type=F32)
    gc_r = jnp.dot(m_rev, g, precision=HIGHEST, preferred_element_type=F32)
    lane = lax.broadcasted_iota(jnp.int32, s.shape, 1)
    gc = jnp.where(lane >= 8 + B_HEADS, gc_r, gc_f)
    aux = jnp.where(lane < 8, beta, jnp.where(lane < 16, gc, 0.0))
    aux_ref[...] = aux
    auxt = aux.T
    for cc in range(n // chunk):
        auxt_ref[cc] = auxt[0:16, cc * chunk:(cc + 1) * chunk]


def _even_prep(p, small, a_conv, b_conv, alog_pad, dtb_pad, chunk):
    bsz, t, nm = p.shape
    nhb = TM // HALO
    last_hb = t // HALO - 1
    nck = TM // chunk
    return pl.pallas_call(
        functools.partial(_even_prep_kernel, chunk=chunk),
        grid=(bsz, t // TM),
        in_specs=[
            pl.BlockSpec((None, TM, nm), lambda b, j: (b, j, 0)),
            pl.BlockSpec((None, HALO, nm), lambda b, j: (b, jnp.maximum(j * nhb - 1, 0), 0)),
            pl.BlockSpec((None, HALO, nm), lambda b, j: (b, jnp.minimum((j + 1) * nhb, last_hb), 0)),
            pl.BlockSpec((None, TM, LANES), lambda b, j: (b, j, 0)),
            pl.BlockSpec((3, A_WIDTH), lambda b, j: (0, 0)),
            pl.BlockSpec((3, 3 * B_WIDTH), lambda b, j: (0, 0)),
            pl.BlockSpec((1, LANES), lambda b, j: (0, 0)),
            pl.BlockSpec((1, LANES), lambda b, j: (0, 0)),
        ],
        out_specs=[
            pl.BlockSpec((None, TM, A_WIDTH), lambda b, j: (b, j, 0)),
            pl.BlockSpec((None, TM, 3 * B_WIDTH), lambda b, j: (b, j, 0)),
            pl.BlockSpec((None, TM, LANES), lambda b, j: (b, j, 0)),
            pl.BlockSpec((None, nck, 16, chunk), lambda b, j: (b, j, 0, 0)),
        ],
        out_shape=[
            jax.ShapeDtypeStruct((bsz, t, A_WIDTH), BF16),
            jax.ShapeDtypeStruct((bsz, t, 3 * B_WIDTH), BF16),
            jax.ShapeDtypeStruct((bsz, t, LANES), F32),
            jax.ShapeDtypeStruct((bsz, t // chunk, 16, chunk), F32),
        ],
        compiler_params=_cparams(2),
        name="even_prep",
    )(p, p, p, small, a_conv, b_conv, alog_pad, dtb_pad)


def _gdn_kernel(q_ref, k_ref, v_ref, gate_ref, aux_ref, auxt_ref, bn_ref, o_ref, s_scr, acc_scr,
                *, chunk, n_ctx_chunks):
    t = q_ref.shape[0]
    nc = t // chunk
    cs = chunk
    n_chain = 2 * B_HEADS
    s_scr[...] = jnp.zeros_like(s_scr)
    acc_scr[...] = jnp.zeros_like(acc_scr)

    ri = lax.broadcasted_iota(jnp.int32, (cs, cs), 0)
    ci = lax.broadcasted_iota(jnp.int32, (cs, cs), 1)
    eye = jnp.where(ri == ci, 1.0, 0.0)
    n_sq = int(np.log2(cs))

    def step(n, carry):
        c_fwd = n
        c_rev = jnp.where(n < n_ctx_chunks, n_ctx_chunks - 1 - n, nc - 1 - (n - n_ctx_chunks))
        for d in range(2):
            c = c_fwd if d == 0 else c_rev
            r0 = pl.multiple_of(c * cs, cs)
            auxc = aux_ref[pl.ds(r0, cs), :]
            incl = (ci <= ri) if d == 0 else (ci >= ri)
            strict = (ci < ri) if d == 0 else (ci > ri)
            last = cs - 1 if d == 0 else 0
            for h in range(B_HEADS):
                chain = d * B_HEADS + h
                lo = h * B_HEAD_DIM
                q = q_ref[pl.ds(r0, cs), lo:lo + B_HEAD_DIM]
                k = k_ref[pl.ds(r0, cs), lo:lo + B_HEAD_DIM]
                v = v_ref[pl.ds(r0, cs), lo:lo + B_HEAD_DIM].astype(F32)
                beta = auxc[:, chain:chain + 1]
                gcol = auxc[:, 8 + chain:9 + chain]
                grow = auxt_ref[c, 8 + chain:9 + chain, :]
                glast = gcol[last:last + 1, :]
                decay = jnp.exp(jnp.where(incl, gcol - grow, -jnp.inf))
                kk = lax.dot_general(k, k, (((1,), (1,)), ((), ())), preferred_element_type=F32)
                qk = lax.dot_general(q, k, (((1,), (1,)), ((), ())), preferred_element_type=F32)
                a = jnp.where(strict, beta * kk * decay, 0.0)
                npow = -a
                tinv = eye + npow
                for _ in range(n_sq - 1):
                    npow = jnp.dot(npow, npow, precision=HIGHEST, preferred_element_type=F32)
                    tinv = tinv + jnp.dot(tinv, npow, precision=HIGHEST, preferred_element_type=F32)
                kf = k.astype(F32)
                rhs = jnp.concatenate([v * beta, kf * (beta * jnp.exp(gcol))], axis=1)
                uw = jnp.dot(tinv.astype(BF16), rhs.astype(BF16), preferred_element_type=F32)
                u = uw[:, :B_HEAD_DIM]
                w = uw[:, B_HEAD_DIM:]
                attn = (qk * decay).astype(BF16)
                qe = (q.astype(F32) * jnp.exp(gcol)).astype(BF16)
                ke = (kf * jnp.exp(glast - gcol)).astype(BF16)
                s = s_scr[chain]
                sb = s.astype(BF16)
                v_new = u - jnp.dot(w.astype(BF16), sb, preferred_element_type=F32)
                vb = v_new.astype(BF16)
                o = (jnp.dot(qe, sb, preferred_element_type=F32)
                     + jnp.dot(attn, vb, preferred_element_type=F32))
                s_scr[chain] = s * jnp.exp(glast) + lax.dot_general(
                    ke, vb, (((0,), (0,)), ((), ())), preferred_element_type=F32)
                acc_scr[pl.ds(r0, cs), lo:lo + B_HEAD_DIM] += o
        return carry

    lax.fori_loop(0, nc, step, 0)

    rows = 256

    def fin(i, carry):
        r0 = pl.multiple_of(i * rows, rows)
        for h in range(B_HEADS):
            lo = h * B_HEAD_DIM
            o = acc_scr[pl.ds(r0, rows), lo:lo + B_HEAD_DIM]
            ms = jnp.mean(o * o, axis=-1, keepdims=True)
            y = o * lax.rsqrt(ms + RMS_EPS) * bn_ref[...]
            gate = gate_ref[pl.ds(r0, rows), lo:lo + B_HEAD_DIM].astype(F32)
            o_ref[pl.ds(r0, rows), lo:lo + B_HEAD_DIM] = (y * _silu(gate)).astype(BF16)
        return carry

    lax.fori_loop(0, t // rows, fin, 0)


def _gdn(qkv, p, aux, auxt, b_norm, chunk, n_ctx_chunks):
    bsz, t, _ = qkv.shape
    gate_blk = (3 * A_WIDTH + 3 * B_WIDTH) // B_WIDTH
    return pl.pallas_call(
        functools.partial(_gdn_kernel, chunk=chunk, n_ctx_chunks=n_ctx_chunks),
        grid=(bsz,),
        in_specs=[
            pl.BlockSpec((None, t, B_WIDTH), lambda b: (b, 0, 0)),
            pl.BlockSpec((None, t, B_WIDTH), lambda b: (b, 0, 1)),
            pl.BlockSpec((None, t, B_WIDTH), lambda b: (b, 0, 2)),
            pl.BlockSpec((None, t, B_WIDTH), lambda b: (b, 0, gate_blk)),
            pl.BlockSpec((None, t, LANES), lambda b: (b, 0, 0)),
            pl.BlockSpec((None, t // chunk, 16, chunk), lambda b: (b, 0, 0, 0)),
            pl.BlockSpec((1, B_HEAD_DIM), lambda b: (0, 0)),
        ],
        out_specs=pl.BlockSpec((None, t, B_WIDTH), lambda b: (b, 0, 0)),
        out_shape=jax.ShapeDtypeStruct((bsz, t, B_WIDTH), BF16),
        scratch_shapes=[
            pltpu.VMEM((2 * B_HEADS, B_HEAD_DIM, B_HEAD_DIM), F32),
            pltpu.VMEM((t, B_WIDTH), F32),
        ],
        compiler_params=_cparams(1),
        name="gdn",
    )(qkv, qkv, qkv, p, aux, auxt, b_norm)


def _route(logits_t, bias, sel_scr, gate_scr):
    aff = _sigmoid(logits_t)
    sel_scr[...] = aff + bias
    n = logits_t.shape[1]
    sel = [sel_scr[e:e + 1, :] for e in range(N_EXPERTS)]
    affr = [aff[e:e + 1, :] for e in range(N_EXPERTS)]
    in_top2 = []
    gscore = []
    for g in range(N_GROUPS):
        ids = range(g * PER_GROUP, (g + 1) * PER_GROUP)
        gs = jnp.zeros((1, n), F32)
        for e in ids:
            rank = jnp.zeros((1, n), F32)
            for e2 in ids:
                if e2 == e:
                    continue
                ahead = (sel[e2] >= sel[e]) if e2 < e else (sel[e2] > sel[e])
                rank = rank + jnp.where(ahead, 1.0, 0.0)
            top = rank < 2.0
            in_top2.append(top)
            gs = gs + jnp.where(top, sel[e], 0.0)
        gscore.append(gs)
    best = jnp.zeros((1, n), jnp.int32)
    bestv = gscore[0]
    for g in range(1, N_GROUPS):
        better = gscore[g] > bestv
        best = jnp.where(better, g, best)
        bestv = jnp.where(better, gscore[g], bestv)
    chosen = [in_top2[e] & (best == e // PER_GROUP) for e in range(N_EXPERTS)]
    denom = jnp.zeros((1, n), F32)
    for e in range(N_EXPERTS):
        denom = denom + jnp.where(chosen[e], affr[e], 0.0)
    gate_scr[...] = jnp.zeros_like(gate_scr)
    for e in range(N_EXPERTS):
        gate_scr[e:e + 1, :] = jnp.where(chosen[e], affr[e] / denom, 0.0)
    return gate_scr[...].T


def _outproj_kernel(ya_ref, yb_ref, x_ref, mod_ref, w_ref, lng_ref, lnb_ref, rwt_ref, rb_ref,
                    xo_ref, h_ref, g_ref, sel_scr, gate_scr):
    mod = mod_ref[...]
    wa = w_ref[0:ya_ref.shape[1], :]
    wb = w_ref[ya_ref.shape[1]:, :]
    y = (jnp.dot(ya_ref[...], wa, preferred_element_type=F32)
         + jnp.dot(yb_ref[...], wb, preferred_element_type=F32))
    xn = _layer_norm(DEEPNORM_ALPHA * x_ref[...] + mod[2:3] * y, lng_ref[...], lnb_ref[...])
    xo_ref[...] = xn
    h = xn * (1.0 + mod[4:5]) + mod[3:4]
    h_ref[...] = h.astype(BF16)
    logits_t = lax.dot_general(rwt_ref[...], h, (((1,), (1,)), ((), ())),
                               precision=HIGHEST, preferred_element_type=F32)
    g_ref[...] = _route(logits_t, rb_ref[...], sel_scr, gate_scr)


def _outproj(ya, yb, xres, modarr, w_out, ln_g, ln_b, rwt, rbias, row_blk0):
    bsz, n, wa = ya.shape
    d = xres.shape[2]
    return pl.pallas_call(
        _outproj_kernel,
        grid=(bsz, n // TM),
        in_specs=[
            pl.BlockSpec((None, TM, wa), lambda b, j: (b, j, 0)),
            pl.BlockSpec((None, TM, yb.shape[2]), lambda b, j: (b, j, 0)),
            pl.BlockSpec((None, TM, d), lambda b, j: (b, j + row_blk0, 0)),
            pl.BlockSpec((None, None, 6, d), lambda b, j: (b, jnp.minimum(j + row_blk0, 1), 0, 0)),
            pl.BlockSpec(w_out.shape, lambda b, j: (0, 0)),
            pl.BlockSpec((1, d), lambda b, j: (0, 0)),
            pl.BlockSpec((1, d), lambda b, j: (0, 0)),
            pl.BlockSpec((N_EXPERTS, d), lambda b, j: (0, 0)),
            pl.BlockSpec((N_EXPERTS, 1), lambda b, j: (0, 0)),
        ],
        out_specs=[
            pl.BlockSpec((None, TM, d), lambda b, j: (b, j, 0)),
            pl.BlockSpec((None, TM, d), lambda b, j: (b, j, 0)),
            pl.BlockSpec((None, TM, LANES), lambda b, j: (b, j, 0)),
        ],
        out_shape=[
            jax.ShapeDtypeStruct((bsz, n, d), F32),
            jax.ShapeDtypeStruct((bsz, n, d), BF16),
            jax.ShapeDtypeStruct((bsz, n, LANES), F32),
        ],
        scratch_shapes=[pltpu.VMEM((N_EXPERTS, TM), F32), pltpu.VMEM((LANES, TM), F32)],
        compiler_params=_cparams(2),
        name="outproj",
    )(ya, yb, xres, modarr, w_out, ln_g, ln_b, rwt, rbias)


def _moe_dense_kernel(h_ref, g_ref, wg_ref, wu_ref, wd_ref, o_ref, acc_ref):
    e = pl.program_id(1)

    @pl.when(e == 0)
    def _():
        acc_ref[...] = jnp.zeros_like(acc_ref)

    h = h_ref[...]
    gate = jnp.dot(h, wg_ref[...], preferred_element_type=F32)
    up = jnp.dot(h, wu_ref[...], preferred_element_type=F32)
    act = (_silu(gate) * up).astype(BF16)
    y = jnp.dot(act, wd_ref[...], preferred_element_type=F32)
    lane = lax.broadcasted_iota(jnp.int32, g_ref.shape, 1)
    gcol = jnp.sum(jnp.where(lane == e, g_ref[...], 0.0), axis=-1, keepdims=True)
    acc_ref[...] += gcol * y

    @pl.when(e == pl.num_programs(1) - 1)
    def _():
        o_ref[...] = acc_ref[...].astype(o_ref.dtype)


def _moe_dense(h, gates, wg, wu, wd):
    n, d = h.shape
    ne, _, de = wg.shape
    return pl.pallas_call(
        _moe_dense_kernel,
        grid=(n // MOE_TM, ne),
        in_specs=[
            pl.BlockSpec((MOE_TM, d), lambda i, e: (i, 0)),
            pl.BlockSpec((MOE_TM, LANES), lambda i, e: (i, 0)),
            pl.BlockSpec((None, d, de), lambda i, e: (e, 0, 0)),
            pl.BlockSpec((None, d, de), lambda i, e: (e, 0, 0)),
            pl.BlockSpec((None, de, d), lambda i, e: (e, 0, 0)),
        ],
        out_specs=pl.BlockSpec((MOE_TM, d), lambda i, e: (i, 0)),
        out_shape=jax.ShapeDtypeStruct((n, d), BF16),
        scratch_shapes=[pltpu.VMEM((MOE_TM, d), F32)],
        compiler_params=_cparams(2),
        name="moe_dense",
    )(h, gates, wg, wu, wd)


def _ln2_kernel(x_ref, f_ref, mod_ref, lng_ref, lnb_ref, o_ref):
    mod = mod_ref[...]
    v = DEEPNORM_ALPHA * x_ref[...] + mod[5:6] * f_ref[...].astype(F32)
    o_ref[...] = _layer_norm(v, lng_ref[...], lnb_ref[...])


def _ln2(x, f, modarr, ln_g, ln_b, kind0):
    bsz, n, d = x.shape
    return pl.pallas_call(
        _ln2_kernel,
        grid=(bsz, n // TM),
        in_specs=[
            pl.BlockSpec((None, TM, d), lambda b, j: (b, j, 0)),
            pl.BlockSpec((None, TM, d), lambda b, j: (b, j, 0)),
            pl.BlockSpec((None, None, 6, d), lambda b, j: (b, jnp.minimum(j + kind0, 1), 0, 0)),
            pl.BlockSpec((1, d), lambda b, j: (0, 0)),
            pl.BlockSpec((1, d), lambda b, j: (0, 0)),
        ],
        out_specs=pl.BlockSpec((None, TM, d), lambda b, j: (b, j, 0)),
        out_shape=jax.ShapeDtypeStruct((bsz, n, d), F32),
        compiler_params=_cparams(2),
        name="ln2",
    )(x, f, modarr, ln_g, ln_b)


def _rope(x, c, s1, s2, shift):
    w = x.shape[1]
    return x * c + pltpu.roll(x, w - shift, 1) * s1 + pltpu.roll(x, shift, 1) * s2


def _rms(x, g):
    return x * lax.rsqrt(jnp.mean(x * x, axis=-1, keepdims=True) + RMS_EPS) * g


def _inproj_odd_kernel(x_ref, mod_ref, w_ref, qn_ref, kvn_ref, wuq_ref, wk_ref, we_ref, wv_ref,
                       tw_ref, tq_ref, tk_ref, qw_ref, kw_ref, vw_ref, qm_ref, km_ref, vm_ref):
    j = pl.program_id(1)
    is_ctx = j == 0
    mod = mod_ref[...]
    h = (x_ref[...] * (1.0 + mod[1:2]) + mod[0:1]).astype(BF16)
    p = jnp.dot(h, w_ref[...], preferred_element_type=F32)

    def tables(t_ref):
        c = jnp.where(is_ctx, 1.0, t_ref[0])
        s1 = jnp.where(is_ctx, 0.0, t_ref[1])
        s2 = jnp.where(is_ctx, 0.0, t_ref[2])
        return c, s1, s2

    cw, s1w, s2w = tables(tw_ref)
    nq = C_Q_HEADS * C_HEAD_DIM
    for r in range(nq // LANES):
        blk = _rope(p[:, r * LANES:(r + 1) * LANES], cw, s1w, s2w, C_HEAD_DIM // 2)
        qw_ref[:, r * LANES:(r + 1) * LANES] = (blk * (C_HEAD_DIM ** -0.5)).astype(BF16)
    kw_ref[...] = _rope(p[:, nq:nq + LANES], cw, s1w, s2w, C_HEAD_DIM // 2).astype(BF16)
    vw_ref[...] = p[:, nq + LANES:nq + 2 * LANES].astype(BF16)

    o = nq + 2 * LANES
    dq = _rms(p[:, o:o + D_Q_RANK], qn_ref[...]).astype(BF16)
    o += D_Q_RANK
    dkv = _rms(p[:, o:o + D_KV_RANK], kvn_ref[...]).astype(BF16)
    o += D_KV_RANK
    cq, s1q, s2q = tables(tq_ref)
    ck, s1k, s2k = tables(tk_ref)
    krope = _rope(p[:, o:o + LANES], ck, s1k, s2k, D_ROPE // 2).astype(BF16)
    scale = (D_NOPE + D_ROPE) ** -0.5
    for hh in range(D_HEADS):
        sl = slice(hh * LANES, (hh + 1) * LANES)
        qh = jnp.dot(dq, wuq_ref[:, sl], preferred_element_type=F32)
        qm_ref[:, sl] = (_rope(qh, cq, s1q, s2q, D_ROPE // 2) * scale).astype(BF16)
        kh = (jnp.dot(dkv, wk_ref[:, sl], preferred_element_type=F32)
              + jnp.dot(krope, we_ref[:, sl], preferred_element_type=F32))
        km_ref[:, sl] = kh.astype(BF16)
        vm_ref[:, sl] = jnp.dot(dkv, wv_ref[:, sl], preferred_element_type=F32).astype(BF16)


def _inproj_odd(xin, modarr, w1, qnorm, kvnorm, wuq, wk, we, wv, tab_w, tab_q, tab_k):
    bsz, t, d = xin.shape
    n1 = w1.shape[1]
    hw = D_HEADS * LANES

    def tab_spec():
        return pl.BlockSpec((3, TM, LANES), lambda b, j: (0, jnp.maximum(j - 1, 0), 0))

    def full(a):
        return pl.BlockSpec(a.shape, lambda b, j: (0,) * a.ndim)

    def out(width):
        return pl.BlockSpec((None, TM, width), lambda b, j: (b, j, 0))

    widths = (C_Q_HEADS * C_HEAD_DIM, LANES, LANES, hw, hw, hw)
    return pl.pallas_call(
        _inproj_odd_kernel,
        grid=(bsz, t // TM),
        in_specs=[
            pl.BlockSpec((None, TM, d), lambda b, j: (b, j, 0)),
            pl.BlockSpec((None, None, 6, d), lambda b, j: (b, jnp.minimum(j, 1), 0, 0)),
            full(w1), full(qnorm), full(kvnorm), full(wuq), full(wk), full(we), full(wv),
            tab_spec(), tab_spec(), tab_spec(),
        ],
        out_specs=[out(w) for w in widths],
        out_shape=[jax.ShapeDtypeStruct((bsz, t, w), BF16) for w in widths],
        compiler_params=_cparams(2),
        name="inproj_odd",
    )(xin, modarr, w1, qnorm, kvnorm, wuq, wk, we, wv, tab_w, tab_q, tab_k)


def _win_kernel(sink_ref, q_ref, k_ref, v_ref, o_ref, klo_scr, khi_scr, *, n_ctx):
    i = pl.program_id(1)
    wdw = C_WINDOW
    t = k_ref.shape[0]
    lane = lax.broadcasted_iota(jnp.int32, (t, LANES), 1)

    @pl.when(i == 0)
    def _():
        kk = k_ref[...]
        klo_scr[...] = jnp.where(lane < C_HEAD_DIM, kk, jnp.zeros_like(kk))
        khi_scr[...] = jnp.where(lane >= C_HEAD_DIM, kk, jnp.zeros_like(kk))

    n_lat_blk = (t - n_ctx) // wdw
    blk0 = jnp.clip(i - 1, 0, n_lat_blk - 3)
    r0 = pl.multiple_of(n_ctx + blk0 * wdw, wdw)
    kpos = blk0 * wdw + lax.broadcasted_iota(jnp.int32, (wdw, 3 * wdw), 1)
    qpos = i * wdw + lax.broadcasted_iota(jnp.int32, (wdw, 3 * wdw), 0)
    near = jnp.abs(kpos - qpos) <= wdw
    v_loc = v_ref[pl.ds(r0, 3 * wdw), :]
    v_ctx = v_ref[0:n_ctx, :]
    olane = lax.broadcasted_iota(jnp.int32, (wdw, LANES), 1)
    n_rep = C_Q_HEADS // C_KV_HEADS
    for r in range(n_rep):
        q = q_ref[:, r * LANES:(r + 1) * LANES]
        outs = []
        for g, k_scr in enumerate((klo_scr, khi_scr)):
            k_loc = k_scr[pl.ds(r0, 3 * wdw), :]
            k_ctx = k_scr[0:n_ctx, :]
            s_loc = lax.dot_general(q, k_loc, (((1,), (1,)), ((), ())), preferred_element_type=F32)
            s_loc = jnp.where(near, s_loc, -jnp.inf)
            s_ctx = lax.dot_general(q, k_ctx, (((1,), (1,)), ((), ())), preferred_element_type=F32)
            sink = sink_ref[g * n_rep + r]
            m = jnp.maximum(jnp.maximum(jnp.max(s_loc, axis=-1, keepdims=True),
                                        jnp.max(s_ctx, axis=-1, keepdims=True)), sink)
            p_loc = jnp.exp(s_loc - m)
            p_ctx = jnp.exp(s_ctx - m)
            den = (jnp.sum(p_loc, axis=-1, keepdims=True) + jnp.sum(p_ctx, axis=-1, keepdims=True)
                   + jnp.exp(sink - m))
            pv = (jnp.dot(p_loc.astype(BF16), v_loc, preferred_element_type=F32)
                  + jnp.dot(p_ctx.astype(BF16), v_ctx, preferred_element_type=F32))
            outs.append(pv / den)
        o_ref[:, r * LANES:(r + 1) * LANES] = jnp.where(olane < C_HEAD_DIM, outs[0], outs[1]).astype(BF16)


def _win_attention(sink, qw, kw, vw, n_ctx):
    bsz, t, nq = qw.shape
    n_lat = t - n_ctx
    ctx_blk = n_ctx // C_WINDOW
    grid_spec = pltpu.PrefetchScalarGridSpec(
        num_scalar_prefetch=1,
        grid=(bsz, n_lat // C_WINDOW),
        in_specs=[
            pl.BlockSpec((None, C_WINDOW, nq), lambda b, i, s: (b, i + ctx_blk, 0)),
            pl.BlockSpec((None, t, LANES), lambda b, i, s: (b, 0, 0)),
            pl.BlockSpec((None, t, LANES), lambda b, i, s: (b, 0, 0)),
        ],
        out_specs=pl.BlockSpec((None, C_WINDOW, nq), lambda b, i, s: (b, i, 0)),
        scratch_shapes=[pltpu.VMEM((t, LANES), BF16), pltpu.VMEM((t, LANES), BF16)],
    )
    return pl.pallas_call(
        functools.partial(_win_kernel, n_ctx=n_ctx),
        grid_spec=grid_spec,
        out_shape=jax.ShapeDtypeStruct((bsz, n_lat, nq), BF16),
        compiler_params=_cparams(2),
        name="win_attention",
    )(sink, qw, kw, vw)


MLA_TQ = 256


def _mla_kernel(q_ref, k_ref, v_ref, o_ref):
    for hp in range(D_HEADS // 2):
        acc = None
        for hh in (2 * hp, 2 * hp + 1):
            sl = slice(hh * LANES, (hh + 1) * LANES)
            s = lax.dot_general(q_ref[:, sl], k_ref[:, sl], (((1,), (1,)), ((), ())),
                                preferred_element_type=F32)
            m = jnp.max(s, axis=-1, keepdims=True)
            p = jnp.exp(s - m)
            den = jnp.sum(p, axis=-1, keepdims=True)
            pv = jnp.dot(p.astype(BF16), v_ref[:, sl], preferred_element_type=F32) / den
            acc = pv if acc is None else acc + pv
        o_ref[:, hp * LANES:(hp + 1) * LANES] = acc.astype(BF16)


def _mla_attention(qm, km, vm, n_ctx):
    bsz, t, hw = qm.shape
    n_lat = t - n_ctx
    ctx_blk = n_ctx // MLA_TQ
    ow = D_HEADS * D_V
    return pl.pallas_call(
        _mla_kernel,
        grid=(bsz, n_lat // MLA_TQ),
        in_specs=[
            pl.BlockSpec((None, MLA_TQ, hw), lambda b, i: (b, i + ctx_blk, 0)),
            pl.BlockSpec((None, t, hw), lambda b, i: (b, 0, 0)),
            pl.BlockSpec((None, t, hw), lambda b, i: (b, 0, 0)),
        ],
        out_specs=pl.BlockSpec((None, MLA_TQ, ow), lambda b, i: (b, i, 0)),
        out_shape=jax.ShapeDtypeStruct((bsz, n_lat, ow), BF16),
        compiler_params=_cparams(2),
        name="mla_attention",
    )(qm, km, vm)


def _rope_tables(n_tokens, rot_dim, group, offset):
    t = jnp.arange(n_tokens)
    rows = (t // GRID_W).astype(F32)
    cols = (t % GRID_W).astype(F32)
    n_freq = rot_dim // 4
    inv_freq = ROPE_BASE ** (-jnp.arange(n_freq, dtype=F32) / n_freq)
    ang = jnp.concatenate([rows[:, None] * inv_freq, cols[:, None] * inv_freq], -1)
    cos, sin = jnp.cos(ang), jnp.sin(ang)
    half = rot_dim // 2
    c = jnp.ones((n_tokens, LANES), F32)
    s1 = jnp.zeros((n_tokens, LANES), F32)
    s2 = jnp.zeros((n_tokens, LANES), F32)
    for start in range(offset, LANES, group):
        c = c.at[:, start:start + half].set(cos).at[:, start + half:start + rot_dim].set(cos)
        s1 = s1.at[:, start:start + half].set(-sin)
        s2 = s2.at[:, start + half:start + rot_dim].set(sin)
    return jnp.stack([c, s1, s2])


def _odd_weights(w_in, wuq, wukv, w_out):
    d = w_in.shape[0]
    nq = C_Q_HEADS * C_HEAD_DIM
    nkv = C_KV_HEADS * C_HEAD_DIM
    n_rep = C_Q_HEADS // C_KV_HEADS
    order = [g * n_rep + r for r in range(n_rep) for g in range(C_KV_HEADS)]
    cq = w_in[:, :nq].reshape(d, C_Q_HEADS, C_HEAD_DIM)[:, order].reshape(d, nq)
    rest = w_in[:, nq:nq + 2 * nkv + D_Q_RANK + D_KV_RANK]
    krope = jnp.pad(w_in[:, nq + 2 * nkv + D_Q_RANK + D_KV_RANK:], ((0, 0), (0, LANES - D_ROPE)))
    w1 = jnp.concatenate([cq, rest, krope], axis=1).astype(BF16)
    qh = wuq.reshape(D_Q_RANK, D_HEADS, D_NOPE + D_ROPE)
    wuq_p = jnp.pad(qh, ((0, 0), (0, 0), (0, LANES - D_NOPE - D_ROPE))).reshape(D_Q_RANK, D_HEADS * LANES)
    kvh = wukv.reshape(D_KV_RANK, D_HEADS, D_NOPE + D_V)
    wk_p = jnp.pad(kvh[:, :, :D_NOPE], ((0, 0), (0, 0), (0, LANES - D_NOPE))).reshape(D_KV_RANK, D_HEADS * LANES)
    e_blk = jnp.zeros((LANES, LANES), F32).at[jnp.arange(D_ROPE), D_NOPE + jnp.arange(D_ROPE)].set(1.0)
    we = jnp.tile(e_blk, (1, D_HEADS))
    vh = kvh[:, :, D_NOPE:]
    even = (jnp.arange(D_HEADS) % 2 == 0)[None, :, None]
    wv_p = jnp.where(even, jnp.pad(vh, ((0, 0), (0, 0), (0, D_V))),
                     jnp.pad(vh, ((0, 0), (0, 0), (D_V, 0)))).reshape(D_KV_RANK, D_HEADS * LANES)
    wo_win = w_out[:nq].reshape(C_Q_HEADS, C_HEAD_DIM, -1)[jnp.array(order)].reshape(nq, -1)
    wo = jnp.concatenate([wo_win, w_out[nq:]], axis=0).astype(BF16)
    return w1, wuq_p.astype(BF16), wk_p.astype(BF16), we.astype(BF16), wv_p.astype(BF16), wo


def kernel(x, c, ctx, c_ctx, ada_w, ada_b, ln_g, ln_b, ev_w_in, ev_a_conv, ev_b_conv, ev_b_alog, ev_b_dtbias, ev_b_norm, ev_w_out, od_w_in, od_c_sink, od_d_qnorm, od_d_kvnorm, od_d_wuq, od_d_wukv, od_w_out, router_w, router_bias, moe_w_gate, moe_w_up, moe_w_down):
    bsz, n_lat, d = x.shape
    n_ctx = ctx.shape[1]
    assert n_ctx == TM and n_lat % TM == 0 and n_lat % GRID_W == 0
    assert ada_w.shape[0] == DEPTH and bsz + 1 <= 40
    t = n_ctx + n_lat

    cs = jnp.zeros((40, d), F32).at[:bsz].set(c).at[bsz].set(c_ctx)
    mods = _ada_mod(cs, ada_w, ada_b)

    def modarr(layer):
        m = mods[layer].reshape(40, 6, d)
        return jnp.stack([jnp.broadcast_to(m[bsz], (bsz, 6, d)), m[:bsz]], axis=1)

    rwt = router_w.T
    rbias = router_bias.reshape(N_EXPERTS, 1)
    xcat = jnp.concatenate([ctx, x], axis=1)

    mod0 = modarr(0)
    n_main = 3 * A_WIDTH + 4 * B_WIDTH
    w_main = ev_w_in[0][:, :n_main].astype(BF16)
    w_small = jnp.pad(ev_w_in[0][:, n_main:], ((0, 0), (0, LANES - 4 * B_HEADS))).astype(BF16)
    p, small = _inproj_even(xcat, mod0, w_main, w_small)
    alog_pad = jnp.zeros((1, LANES), F32).at[0, 8:16].set(ev_b_alog[0].reshape(-1))
    dtb_pad = jnp.zeros((1, LANES), F32).at[0, 8:16].set(ev_b_dtbias[0].reshape(-1))
    ya, qkv, aux, auxt = _even_prep(p, small, ev_a_conv[0], ev_b_conv[0], alog_pad, dtb_pad, GDN_CHUNK)
    yb = _gdn(qkv, p, aux, auxt, ev_b_norm[0].reshape(1, B_HEAD_DIM), GDN_CHUNK, n_ctx // GDN_CHUNK)
    x1, h1, gates = _outproj(ya, yb, xcat, mod0, ev_w_out[0].astype(BF16),
                             ln_g[0, 0].reshape(1, d), ln_b[0, 0].reshape(1, d), rwt, rbias, 0)
    f = _moe_dense(h1.reshape(bsz * t, d), gates.reshape(bsz * t, LANES),
                   moe_w_gate[0].astype(BF16), moe_w_up[0].astype(BF16), moe_w_down[0].astype(BF16))
    x2 = _ln2(x1, f.reshape(bsz, t, d), mod0, ln_g[0, 1].reshape(1, d), ln_b[0, 1].reshape(1, d), 0)

    mod1 = modarr(1)
    w1, wuq_p, wk_p, we, wv_p, wo = _odd_weights(od_w_in[0], od_d_wuq[0], od_d_wukv[0], od_w_out[0])
    tab_w = _rope_tables(n_lat, C_HEAD_DIM, C_HEAD_DIM, 0)
    tab_q = _rope_tables(n_lat, D_ROPE, LANES, D_NOPE)
    tab_k = _rope_tables(n_lat, D_ROPE, LANES, 0)
    qw, kw, vw, qm, km, vm = _inproj_odd(
        x2, mod1, w1, od_d_qnorm[0].reshape(1, -1), od_d_kvnorm[0].reshape(1, -1),
        wuq_p, wk_p, we, wv_p, tab_w, tab_q, tab_k)
    y_win = _win_attention(od_c_sink[0], qw, kw, vw, n_ctx)
    y_mla = _mla_attention(qm, km, vm, n_ctx)
    x3, h3, gates1 = _outproj(y_win, y_mla, x2, mod1, wo, ln_g[1, 0].reshape(1, d),
                              ln_b[1, 0].reshape(1, d), rwt, rbias, n_ctx // TM)
    f1 = _moe_dense(h3.reshape(bsz * n_lat, d), gates1.reshape(bsz * n_lat, LANES),
                    moe_w_gate[1].astype(BF16), moe_w_up[1].astype(BF16), moe_w_down[1].astype(BF16))
    return _ln2(x3, f1.reshape(bsz, n_lat, d), mod1, ln_g[1, 1].reshape(1, d), ln_b[1, 1].reshape(1, d), 1)
```

```python
import functools

import numpy as np
import jax
import jax.numpy as jnp
from jax import lax
from jax.experimental import pallas as pl
from jax.experimental.pallas import tpu as pltpu

F32 = jnp.float32
BF16 = jnp.bfloat16
HIGHEST = lax.Precision.HIGHEST

DEPTH = 2
GRID_W = 64
DEEPNORM_ALPHA = (2.0 * DEPTH) ** 0.25
LN_EPS = 1e-5
RMS_EPS = 1e-6
ROPE_BASE = 10000.0
B_HEADS = 4
B_HEAD_DIM = 128
B_WIDTH = 512
A_WIDTH = 512
C_Q_HEADS = 8
C_KV_HEADS = 2
C_HEAD_DIM = 64
C_WINDOW = 128
D_HEADS = 8
D_NOPE = 64
D_ROPE = 32
D_V = 64
D_Q_RANK = 384
D_KV_RANK = 256
N_EXPERTS = 16
N_GROUPS = 4
PER_GROUP = N_EXPERTS // N_GROUPS
D_EXPERT = 512

LANES = 128
TM = 256
GDN_CHUNK = 64
MOE_TM = 512
VMEM_LIMIT = 56 * 1024 * 1024


def _cparams(n_axes, vmem=VMEM_LIMIT):
    return pltpu.CompilerParams(dimension_semantics=("arbitrary",) * n_axes, vmem_limit_bytes=vmem)


def _sigmoid(x):
    return 1.0 / (1.0 + jnp.exp(-x))


def _silu(x):
    return x * _sigmoid(x)


def _softplus(x):
    return jnp.maximum(x, 0.0) + jnp.log(1.0 + jnp.exp(-jnp.abs(x)))


def _layer_norm(v, g, b):
    mu = jnp.mean(v, axis=-1, keepdims=True)
    d = v - mu
    var = jnp.mean(d * d, axis=-1, keepdims=True)
    return d * lax.rsqrt(var + LN_EPS) * g + b


def _ada_kernel(c_ref, w_ref, b_ref, o_ref):
    s = _silu(c_ref[...])
    o_ref[...] = jnp.dot(s, w_ref[...], precision=HIGHEST, preferred_element_type=F32) + b_ref[...]


def _ada_mod(cs, ada_w, ada_b):
    depth, d, n6 = ada_w.shape
    rows = cs.shape[0]
    tn = 1536
    return pl.pallas_call(
        _ada_kernel,
        grid=(depth, n6 // tn),
        in_specs=[
            pl.BlockSpec((rows, d), lambda l, n: (0, 0)),
            pl.BlockSpec((None, d, tn), lambda l, n: (l, 0, n)),
            pl.BlockSpec((None, 1, tn), lambda l, n: (l, 0, n)),
        ],
        out_specs=pl.BlockSpec((None, rows, tn), lambda l, n: (l, 0, n)),
        out_shape=jax.ShapeDtypeStruct((depth, rows, n6), F32),
        compiler_params=_cparams(2),
        name="ada_mod",
    )(cs, ada_w, ada_b.reshape(depth, 1, n6))


def _inproj_even_kernel(x_ref, mod_ref, wm_ref, ws_ref, p_ref, s_ref):
    mod = mod_ref[...]
    h = (x_ref[...] * (1.0 + mod[1:2]) + mod[0:1]).astype(BF16)
    p_ref[...] = jnp.dot(h, wm_ref[...], preferred_element_type=F32).astype(BF16)
    s_ref[...] = jnp.dot(h, ws_ref[...], preferred_element_type=F32)


def _inproj_even(xcat, modarr, w_main, w_small):
    bsz, t, d = xcat.shape
    nm = w_main.shape[1]
    return pl.pallas_call(
        _inproj_even_kernel,
        grid=(bsz, t // TM),
        in_specs=[
            pl.BlockSpec((None, TM, d), lambda b, j: (b, j, 0)),
            pl.BlockSpec((None, None, 6, d), lambda b, j: (b, jnp.minimum(j, 1), 0, 0)),
            pl.BlockSpec((d, nm), lambda b, j: (0, 0)),
            pl.BlockSpec((d, LANES), lambda b, j: (0, 0)),
        ],
        out_specs=[
            pl.BlockSpec((None, TM, nm), lambda b, j: (b, j, 0)),
            pl.BlockSpec((None, TM, LANES), lambda b, j: (b, j, 0)),
        ],
        out_shape=[
            jax.ShapeDtypeStruct((bsz, t, nm), BF16),
            jax.ShapeDtypeStruct((bsz, t, LANES), F32),
        ],
        compiler_params=_cparams(2),
        name="inproj_even",
    )(xcat, modarr, w_main, w_small)


HALO = 16


def _conv3(z, zp, zn, w):
    n = z.shape[0]
    rows = lax.broadcasted_iota(jnp.int32, z.shape, 0)
    zprev = jnp.where(rows == 0, zp, pltpu.roll(z, 1, 0))
    znext = jnp.where(rows == n - 1, zn, pltpu.roll(z, n - 1, 0))
    return w[0:1] * zprev + w[1:2] * z + w[2:3] * znext


def _even_prep_kernel(p_ref, pp_ref, pn_ref, s_ref, aw_ref, bw_ref, alog_ref, dtb_ref,
                      ya_ref, u_ref, w_ref, qe_ref, ket_ref, att_ref, dec_ref,
                      q_scr, k_scr, v_scr, *, chunk):
    j = pl.program_id(1)
    nj = pl.num_programs(1)
    prev_on = jnp.where(jnp.logical_and(j != 0, j != 1), 1.0, 0.0)
    next_on = jnp.where(jnp.logical_and(j != 0, j != nj - 1), 1.0, 0.0)
    prow = pp_ref[...].astype(F32)[HALO - 1:HALO] * prev_on
    nrow = pn_ref[...].astype(F32)[0:1] * next_on

    def seg(lo, hi):
        return p_ref[:, lo:hi].astype(F32), prow[:, lo:hi], nrow[:, lo:hi]

    a0, _, _ = seg(0, A_WIDTH)
    a1, a1p, a1n = seg(A_WIDTH, 2 * A_WIDTH)
    a2, a2p, a2n = seg(2 * A_WIDTH, 3 * A_WIDTH)
    ya_ref[...] = (a0 * _conv3(a1 * a2, a1p * a2p, a1n * a2n, aw_ref[...])).astype(BF16)

    base = 3 * A_WIDTH
    for which in range(3):
        lo = base + which * B_WIDTH
        z, zp, zn = seg(lo, lo + B_WIDTH)
        c = _silu(_conv3(z, zp, zn, bw_ref[:, which * B_WIDTH:(which + 1) * B_WIDTH]))
        for h in range(B_HEADS):
            ch = c[:, h * B_HEAD_DIM:(h + 1) * B_HEAD_DIM]
            if which < 2:
                ss = jnp.sum(ch * ch, axis=-1, keepdims=True)
                ch = ch * lax.rsqrt(ss + 1e-6)
                if which == 0:
                    ch = ch * (B_HEAD_DIM ** -0.5)
            sl = slice(h * B_HEAD_DIM, (h + 1) * B_HEAD_DIM)
            if which == 0:
                q_scr[:, sl] = ch.astype(BF16)
            elif which == 1:
                k_scr[:, sl] = ch.astype(BF16)
            else:
                v_scr[:, sl] = ch

    s = s_ref[...]
    beta = _sigmoid(s)
    g = -jnp.exp(alog_ref[...]) * _softplus(s + dtb_ref[...])
    n = s.shape[0]
    nck = n // chunk
    ri = lax.broadcasted_iota(jnp.int32, (n, n), 0)
    ci = lax.broadcasted_iota(jnp.int32, (n, n), 1)
    same = (ri // chunk) == (ci // chunk)
    m_fwd = jnp.where(jnp.logical_and(same, ci <= ri), 1.0, 0.0)
    m_rev = jnp.where(jnp.logical_and(same, ci >= ri), 1.0, 0.0)
    m_all = jnp.where(same, 1.0, 0.0)
    gc_f = jnp.dot(m_fwd, g, precision=HIGHEST, preferred_element_type=F32)
    gc_r = jnp.dot(m_rev, g, precision=HIGHEST, preferred_element_type=F32)
    tot = jnp.dot(m_all, g, precision=HIGHEST, preferred_element_type=F32)
    lane = lax.broadcasted_iota(jnp.int32, s.shape, 1)
    gc = jnp.where(lane >= 8 + B_HEADS, gc_r, gc_f)
    e_gc = jnp.exp(gc)
    e_rest = jnp.exp(tot - gc)
    gct = gc.T
    e_tot = jnp.exp(tot)

    ri = lax.broadcasted_iota(jnp.int32, (1, chunk, chunk), 1)
    ci = lax.broadcasted_iota(jnp.int32, (1, chunk, chunk), 2)
    eye = jnp.where(ri == ci, 1.0, 0.0)
    n_sq = int(np.log2(chunk))
    nt_batched = (((2,), (2,)), ((0,), (0,)))
    nn_batched = (((2,), (1,)), ((0,), (0,)))
    kk, qk, kf, qf = [], [], [], []
    for h in range(B_HEADS):
        hs = slice(h * B_HEAD_DIM, (h + 1) * B_HEAD_DIM)
        k3 = k_scr[:, hs].reshape(nck, chunk, B_HEAD_DIM)
        q3 = q_scr[:, hs].reshape(nck, chunk, B_HEAD_DIM)
        kk.append(lax.dot_general(k3, k3, nt_batched, preferred_element_type=F32))
        qk.append(lax.dot_general(q3, k3, nt_batched, preferred_element_type=F32))
        kf.append(k_scr[:, hs].astype(F32))
        qf.append(q_scr[:, hs].astype(F32))
    a_all, rhs_all = [], []
    for d in range(2):
        incl = (ci <= ri) if d == 0 else (ci >= ri)
        strict = (ci < ri) if d == 0 else (ci > ri)
        for h in range(B_HEADS):
            chain = d * B_HEADS + h
            hs = slice(h * B_HEAD_DIM, (h + 1) * B_HEAD_DIM)
            bcol = beta[:, chain:chain + 1]
            e1 = e_gc[:, 8 + chain:9 + chain]
            e2 = e_rest[:, 8 + chain:9 + chain]
            gcol = gc[:, 8 + chain:9 + chain].reshape(nck, chunk, 1)
            grow = jnp.stack([gct[8 + chain:9 + chain, cc * chunk:(cc + 1) * chunk]
                              for cc in range(nck)], axis=0)
            decay = jnp.exp(jnp.where(incl, gcol - grow, -jnp.inf))
            a_all.append(jnp.where(strict, bcol.reshape(nck, chunk, 1) * kk[h] * decay, 0.0))
            att_ref[d, h] = (qk[h] * decay).reshape(n, chunk).astype(BF16)
            qe_ref[d, h] = (qf[h] * e1).astype(BF16)
            ket = (kf[h] * e2).T
            for cc in range(nck):
                ket_ref[d, h, cc] = ket[:, cc * chunk:(cc + 1) * chunk].astype(BF16)
                dec_ref[cc, chain] = jnp.broadcast_to(
                    e_tot[cc * chunk:cc * chunk + 1, 8 + chain:9 + chain], (1, LANES))
            rhs = jnp.concatenate([(v_scr[:, hs] * bcol).astype(BF16),
                                   (kf[h] * (bcol * e1)).astype(BF16)], axis=1)
            rhs_all.append(rhs.reshape(nck, chunk, 2 * B_HEAD_DIM))
    npow = -jnp.concatenate(a_all, axis=0)
    tinv = eye + npow
    for _ in range(n_sq - 1):
        nb = npow.astype(BF16)
        npow = lax.dot_general(nb, nb, nn_batched, preferred_element_type=F32)
        tinv = tinv + lax.dot_general(tinv.astype(BF16), npow.astype(BF16), nn_batched,
                                      preferred_element_type=F32)
    uw = lax.dot_general(tinv.astype(BF16), jnp.concatenate(rhs_all, axis=0), nn_batched,
                         preferred_element_type=F32)
    for d in range(2):
        for h in range(B_HEADS):
            blk = uw[(d * B_HEADS + h) * nck:(d * B_HEADS + h + 1) * nck]
            u_ref[d, h] = blk[:, :, :B_HEAD_DIM].reshape(n, B_HEAD_DIM).astype(BF16)
            w_ref[d, h] = blk[:, :, B_HEAD_DIM:].reshape(n, B_HEAD_DIM).astype(BF16)


def _even_prep(p, small, a_conv, b_conv, alog_pad, dtb_pad, chunk):
    bsz, t, nm = p.shape
    nhb = TM // HALO
    last_hb = t // HALO - 1
    nck = TM // chunk
    nc = t // chunk

    def per_dir():
        return (pl.BlockSpec((None, 2, B_HEADS, TM, B_HEAD_DIM), lambda b, j: (b, 0, 0, j, 0)),
                jax.ShapeDtypeStruct((bsz, 2, B_HEADS, t, B_HEAD_DIM), BF16))

    outs = [
        (pl.BlockSpec((None, TM, A_WIDTH), lambda b, j: (b, j, 0)),
         jax.ShapeDtypeStruct((bsz, t, A_WIDTH), BF16)),
        per_dir(), per_dir(), per_dir(),
        (pl.BlockSpec((None, 2, B_HEADS, nck, B_HEAD_DIM, chunk), lambda b, j: (b, 0, 0, j, 0, 0)),
         jax.ShapeDtypeStruct((bsz, 2, B_HEADS, nc, B_HEAD_DIM, chunk), BF16)),
        (pl.BlockSpec((None, 2, B_HEADS, TM, chunk), lambda b, j: (b, 0, 0, j, 0)),
         jax.ShapeDtypeStruct((bsz, 2, B_HEADS, t, chunk), BF16)),
        (pl.BlockSpec((None, nck, 2 * B_HEADS, 1, LANES), lambda b, j: (b, j, 0, 0, 0)),
         jax.ShapeDtypeStruct((bsz, nc, 2 * B_HEADS, 1, LANES), F32)),
    ]
    return pl.pallas_call(
        functools.partial(_even_prep_kernel, chunk=chunk),
        grid=(bsz, t // TM),
        in_specs=[
            pl.BlockSpec((None, TM, nm), lambda b, j: (b, j, 0)),
            pl.BlockSpec((None, HALO, nm), lambda b, j: (b, jnp.maximum(j * nhb - 1, 0), 0)),
            pl.BlockSpec((None, HALO, nm), lambda b, j: (b, jnp.minimum((j + 1) * nhb, last_hb), 0)),
            pl.BlockSpec((None, TM, LANES), lambda b, j: (b, j, 0)),
            pl.BlockSpec((3, A_WIDTH), lambda b, j: (0, 0)),
            pl.BlockSpec((3, 3 * B_WIDTH), lambda b, j: (0, 0)),
            pl.BlockSpec((1, LANES), lambda b, j: (0, 0)),
            pl.BlockSpec((1, LANES), lambda b, j: (0, 0)),
        ],
        out_specs=[o[0] for o in outs],
        out_shape=[o[1] for o in outs],
        scratch_shapes=[pltpu.VMEM((TM, B_WIDTH), BF16), pltpu.VMEM((TM, B_WIDTH), BF16),
                        pltpu.VMEM((TM, B_WIDTH), F32)],
        compiler_params=_cparams(2),
        name="even_prep",
    )(p, p, p, small, a_conv, b_conv, alog_pad, dtb_pad)


def _gdn_rec_kernel(uf, wf, qf, kf, af, df, ur, wr, qr, kr, ar, dr, of_ref, or_ref, s_scr, *, chunk):
    @pl.when(pl.program_id(1) == 0)
    def _():
        s_scr[...] = jnp.zeros_like(s_scr)

    nck = uf.shape[1] // chunk
    nn_batched = (((2,), (1,)), ((0,), (0,)))

    def both(fwd, rev):
        return jnp.concatenate([fwd, rev], axis=0)

    for cc in range(nck):
        cr = nck - 1 - cc
        rf = slice(cc * chunk, (cc + 1) * chunk)
        rr = slice(cr * chunk, (cr + 1) * chunk)
        s = s_scr[...]
        sb = s.astype(BF16)
        u = both(uf[:, rf, :], ur[:, rr, :]).astype(F32)
        v_new = u - lax.dot_general(both(wf[:, rf, :], wr[:, rr, :]), sb, nn_batched,
                                    preferred_element_type=F32)
        vb = v_new.astype(BF16)
        o = (lax.dot_general(both(qf[:, rf, :], qr[:, rr, :]), sb, nn_batched,
                             preferred_element_type=F32)
             + lax.dot_general(both(af[:, rf, :], ar[:, rr, :]), vb, nn_batched,
                               preferred_element_type=F32))
        dec = both(df[cc, 0:B_HEADS], dr[cr, B_HEADS:2 * B_HEADS])
        s_scr[...] = s * dec + lax.dot_general(both(kf[:, cc], kr[:, cr]), vb, nn_batched,
                                               preferred_element_type=F32)
        for h in range(B_HEADS):
            hs = slice(h * B_HEAD_DIM, (h + 1) * B_HEAD_DIM)
            of_ref[rf, hs] = o[h].astype(BF16)
            or_ref[rr, hs] = o[B_HEADS + h].astype(BF16)


def _gdn_rec(u, w, qe, ket, att, dec, chunk):
    bsz, _, _, t, _ = u.shape
    nt = t // TM
    nck = TM // chunk

    def tile(d, s):
        return s if d == 0 else jnp.where(s == 0, 0, nt - s)

    in_specs = []
    for d in range(2):
        for _ in range(3):
            in_specs.append(pl.BlockSpec((None, None, B_HEADS, TM, B_HEAD_DIM),
                                         lambda b, s, d=d: (b, d, 0, tile(d, s), 0)))
        in_specs.append(pl.BlockSpec((None, None, B_HEADS, nck, B_HEAD_DIM, chunk),
                                     lambda b, s, d=d: (b, d, 0, tile(d, s), 0, 0)))
        in_specs.append(pl.BlockSpec((None, None, B_HEADS, TM, chunk),
                                     lambda b, s, d=d: (b, d, 0, tile(d, s), 0)))
        in_specs.append(pl.BlockSpec((None, nck, 2 * B_HEADS, 1, LANES),
                                     lambda b, s, d=d: (b, tile(d, s), 0, 0, 0)))
    return pl.pallas_call(
        functools.partial(_gdn_rec_kernel, chunk=chunk),
        grid=(bsz, nt),
        in_specs=in_specs,
        out_specs=[pl.BlockSpec((None, TM, B_WIDTH), lambda b, s, d=d: (b, tile(d, s), 0))
                   for d in range(2)],
        out_shape=[jax.ShapeDtypeStruct((bsz, t, B_WIDTH), BF16)] * 2,
        scratch_shapes=[pltpu.VMEM((2 * B_HEADS, B_HEAD_DIM, B_HEAD_DIM), F32)],
        compiler_params=_cparams(2),
        name="gdn_rec",
    )(u, w, qe, ket, att, dec, u, w, qe, ket, att, dec)


def _route(logits_t, bias, sel_scr, gate_scr):
    aff = _sigmoid(logits_t)
    sel_scr[...] = aff + bias
    n = logits_t.shape[1]
    sel = [sel_scr[e:e + 1, :] for e in range(N_EXPERTS)]
    affr = [aff[e:e + 1, :] for e in range(N_EXPERTS)]
    in_top2 = []
    gscore = []
    for g in range(N_GROUPS):
        ids = range(g * PER_GROUP, (g + 1) * PER_GROUP)
        gs = jnp.zeros((1, n), F32)
        for e in ids:
            rank = jnp.zeros((1, n), F32)
            for e2 in ids:
                if e2 == e:
                    continue
                ahead = (sel[e2] >= sel[e]) if e2 < e else (sel[e2] > sel[e])
                rank = rank + jnp.where(ahead, 1.0, 0.0)
            top = rank < 2.0
            in_top2.append(top)
            gs = gs + jnp.where(top, sel[e], 0.0)
        gscore.append(gs)
    best = jnp.zeros((1, n), jnp.int32)
    bestv = gscore[0]
    for g in range(1, N_GROUPS):
        better = gscore[g] > bestv
        best = jnp.where(better, g, best)
        bestv = jnp.where(better, gscore[g], bestv)
    chosen = [in_top2[e] & (best == e // PER_GROUP) for e in range(N_EXPERTS)]
    denom = jnp.zeros((1, n), F32)
    for e in range(N_EXPERTS):
        denom = denom + jnp.where(chosen[e], affr[e], 0.0)
    gate_scr[...] = jnp.zeros_like(gate_scr)
    for e in range(N_EXPERTS):
        gate_scr[e:e + 1, :] = jnp.where(chosen[e], affr[e] / denom, 0.0)
    return gate_scr[...].T


def _outproj_even_kernel(ya_ref, of_ref, or_ref, gate_ref, bn_ref, *rest):
    parts = []
    for h in range(B_HEADS):
        hs = slice(h * B_HEAD_DIM, (h + 1) * B_HEAD_DIM)
        o = of_ref[:, hs].astype(F32) + or_ref[:, hs].astype(F32)
        y = o * lax.rsqrt(jnp.mean(o * o, axis=-1, keepdims=True) + RMS_EPS) * bn_ref[...]
        parts.append((y * _silu(gate_ref[:, hs].astype(F32))).astype(BF16))
    _outproj_body(ya_ref[...], jnp.concatenate(parts, axis=1), *rest)


def _outproj_kernel(ya_ref, yb_ref, *rest):
    _outproj_body(ya_ref[...], yb_ref[...], *rest)


def _outproj_body(ya, yb, x_ref, mod_ref, w_ref, lng_ref, lnb_ref, rwt_ref, rb_ref,
                  xo_ref, h_ref, g_ref, sel_scr, gate_scr):
    mod = mod_ref[...]
    wa = w_ref[0:ya.shape[1], :]
    wb = w_ref[ya.shape[1]:, :]
    y = (jnp.dot(ya, wa, preferred_element_type=F32)
         + jnp.dot(yb, wb, preferred_element_type=F32))
    xn = _layer_norm(DEEPNORM_ALPHA * x_ref[...] + mod[2:3] * y, lng_ref[...], lnb_ref[...])
    xo_ref[...] = xn
    h = xn * (1.0 + mod[4:5]) + mod[3:4]
    h_ref[...] = h.astype(BF16)
    logits_t = lax.dot_general(rwt_ref[...], h, (((1,), (1,)), ((), ())),
                               precision=HIGHEST, preferred_element_type=F32)
    g_ref[...] = _route(logits_t, rb_ref[...], sel_scr, gate_scr)


def _outproj(ya, yb, xres, modarr, w_out, ln_g, ln_b, rwt, rbias, row_blk0, gdn=None):
    bsz, n, wa = ya.shape
    d = xres.shape[2]

    def tok(width, col_blk=0):
        return pl.BlockSpec((None, TM, width), lambda b, j: (b, j, col_blk))

    if gdn is None:
        body, mix_args, mix_specs = _outproj_kernel, (ya, yb), [tok(wa), tok(yb.shape[2])]
    else:
        o_fwd, o_rev, p, b_norm = gdn
        gate_blk = (3 * A_WIDTH + 3 * B_WIDTH) // B_WIDTH
        body, mix_args = _outproj_even_kernel, (ya, o_fwd, o_rev, p, b_norm)
        mix_specs = [tok(wa), tok(B_WIDTH), tok(B_WIDTH), tok(B_WIDTH, gate_blk),
                     pl.BlockSpec((1, B_HEAD_DIM), lambda b, j: (0, 0))]
    return pl.pallas_call(
        body,
        grid=(bsz, n // TM),
        in_specs=mix_specs + [
            pl.BlockSpec((None, TM, d), lambda b, j: (b, j + row_blk0, 0)),
            pl.BlockSpec((None, None, 6, d), lambda b, j: (b, jnp.minimum(j + row_blk0, 1), 0, 0)),
            pl.BlockSpec(w_out.shape, lambda b, j: (0, 0)),
            pl.BlockSpec((1, d), lambda b, j: (0, 0)),
            pl.BlockSpec((1, d), lambda b, j: (0, 0)),
            pl.BlockSpec((N_EXPERTS, d), lambda b, j: (0, 0)),
            pl.BlockSpec((N_EXPERTS, 1), lambda b, j: (0, 0)),
        ],
        out_specs=[
            pl.BlockSpec((None, TM, d), lambda b, j: (b, j, 0)),
            pl.BlockSpec((None, TM, d), lambda b, j: (b, j, 0)),
            pl.BlockSpec((None, TM, LANES), lambda b, j: (b, j, 0)),
        ],
        out_shape=[
            jax.ShapeDtypeStruct((bsz, n, d), F32),
            jax.ShapeDtypeStruct((bsz, n, d), BF16),
            jax.ShapeDtypeStruct((bsz, n, LANES), F32),
        ],
        scratch_shapes=[pltpu.VMEM((N_EXPERTS, TM), F32), pltpu.VMEM((LANES, TM), F32)],
        compiler_params=_cparams(2),
        name="outproj",
    )(*mix_args, xres, modarr, w_out, ln_g, ln_b, rwt, rbias)


def _moe_dense_kernel(h_ref, g_ref, wg_ref, wu_ref, wd_ref, o_ref, acc_ref):
    e = pl.program_id(1)

    @pl.when(e == 0)
    def _():
        acc_ref[...] = jnp.zeros_like(acc_ref)

    h = h_ref[...]
    gate = jnp.dot(h, wg_ref[...], preferred_element_type=F32)
    up = jnp.dot(h, wu_ref[...], preferred_element_type=F32)
    act = (_silu(gate) * up).astype(BF16)
    y = jnp.dot(act, wd_ref[...], preferred_element_type=F32)
    lane = lax.broadcasted_iota(jnp.int32, g_ref.shape, 1)
    gcol = jnp.sum(jnp.where(lane == e, g_ref[...], 0.0), axis=-1, keepdims=True)
    acc_ref[...] += gcol * y

    @pl.when(e == pl.num_programs(1) - 1)
    def _():
        o_ref[...] = acc_ref[...].astype(o_ref.dtype)


def _moe_dense(h, gates, wg, wu, wd):
    n, d = h.shape
    ne, _, de = wg.shape
    return pl.pallas_call(
        _moe_dense_kernel,
        grid=(n // MOE_TM, ne),
        in_specs=[
            pl.BlockSpec((MOE_TM, d), lambda i, e: (i, 0)),
            pl.BlockSpec((MOE_TM, LANES), lambda i, e: (i, 0)),
            pl.BlockSpec((None, d, de), lambda i, e: (e, 0, 0)),
            pl.BlockSpec((None, d, de), lambda i, e: (e, 0, 0)),
            pl.BlockSpec((None, de, d), lambda i, e: (e, 0, 0)),
        ],
        out_specs=pl.BlockSpec((MOE_TM, d), lambda i, e: (i, 0)),
        out_shape=jax.ShapeDtypeStruct((n, d), BF16),
        scratch_shapes=[pltpu.VMEM((MOE_TM, d), F32)],
        compiler_params=_cparams(2),
        name="moe_dense",
    )(h, gates, wg, wu, wd)


def _ln2_kernel(x_ref, f_ref, mod_ref, lng_ref, lnb_ref, o_ref):
    mod = mod_ref[...]
    v = DEEPNORM_ALPHA * x_ref[...] + mod[5:6] * f_ref[...].astype(F32)
    o_ref[...] = _layer_norm(v, lng_ref[...], lnb_ref[...])


def _ln2(x, f, modarr, ln_g, ln_b, kind0):
    bsz, n, d = x.shape
    return pl.pallas_call(
        _ln2_kernel,
        grid=(bsz, n // TM),
        in_specs=[
            pl.BlockSpec((None, TM, d), lambda b, j: (b, j, 0)),
            pl.BlockSpec((None, TM, d), lambda b, j: (b, j, 0)),
            pl.BlockSpec((None, None, 6, d), lambda b, j: (b, jnp.minimum(j + kind0, 1), 0, 0)),
            pl.BlockSpec((1, d), lambda b, j: (0, 0)),
            pl.BlockSpec((1, d), lambda b, j: (0, 0)),
        ],
        out_specs=pl.BlockSpec((None, TM, d), lambda b, j: (b, j, 0)),
        out_shape=jax.ShapeDtypeStruct((bsz, n, d), F32),
        compiler_params=_cparams(2),
        name="ln2",
    )(x, f, modarr, ln_g, ln_b)


def _rope(x, c, s1, s2, shift):
    w = x.shape[1]
    return x * c + pltpu.roll(x, w - shift, 1) * s1 + pltpu.roll(x, shift, 1) * s2


def _rms(x, g):
    return x * lax.rsqrt(jnp.mean(x * x, axis=-1, keepdims=True) + RMS_EPS) * g


def _inproj_odd_kernel(x_ref, mod_ref, w_ref, qn_ref, kvn_ref, wuq_ref, wk_ref, we_ref, wv_ref,
                       tw_ref, tq_ref, tk_ref, qw_ref, kw_ref, vw_ref, qm_ref, km_ref, vm_ref):
    j = pl.program_id(1)
    is_ctx = j == 0
    mod = mod_ref[...]
    h = (x_ref[...] * (1.0 + mod[1:2]) + mod[0:1]).astype(BF16)
    p = jnp.dot(h, w_ref[...], preferred_element_type=F32)

    def tables(t_ref):
        c = jnp.where(is_ctx, 1.0, t_ref[0])
        s1 = jnp.where(is_ctx, 0.0, t_ref[1])
        s2 = jnp.where(is_ctx, 0.0, t_ref[2])
        return c, s1, s2

    cw, s1w, s2w = tables(tw_ref)
    nq = C_Q_HEADS * C_HEAD_DIM
    for r in range(nq // LANES):
        blk = _rope(p[:, r * LANES:(r + 1) * LANES], cw, s1w, s2w, C_HEAD_DIM // 2)
        qw_ref[:, r * LANES:(r + 1) * LANES] = (blk * (C_HEAD_DIM ** -0.5)).astype(BF16)
    kw_ref[...] = _rope(p[:, nq:nq + LANES], cw, s1w, s2w, C_HEAD_DIM // 2).astype(BF16)
    vw_ref[...] = p[:, nq + LANES:nq + 2 * LANES].astype(BF16)

    o = nq + 2 * LANES
    dq = _rms(p[:, o:o + D_Q_RANK], qn_ref[...]).astype(BF16)
    o += D_Q_RANK
    dkv = _rms(p[:, o:o + D_KV_RANK], kvn_ref[...]).astype(BF16)
    o += D_KV_RANK
    cq, s1q, s2q = tables(tq_ref)
    ck, s1k, s2k = tables(tk_ref)
    krope = _rope(p[:, o:o + LANES], ck, s1k, s2k, D_ROPE // 2).astype(BF16)
    scale = (D_NOPE + D_ROPE) ** -0.5
    for hh in range(D_HEADS):
        sl = slice(hh * LANES, (hh + 1) * LANES)
        qh = jnp.dot(dq, wuq_ref[:, sl], preferred_element_type=F32)
        qm_ref[:, sl] = (_rope(qh, cq, s1q, s2q, D_ROPE // 2) * scale).astype(BF16)
        kh = (jnp.dot(dkv, wk_ref[:, sl], preferred_element_type=F32)
              + jnp.dot(krope, we_ref[:, sl], preferred_element_type=F32))
        km_ref[:, sl] = kh.astype(BF16)
        vm_ref[:, sl] = jnp.dot(dkv, wv_ref[:, sl], preferred_element_type=F32).astype(BF16)


def _inproj_odd(xin, modarr, w1, qnorm, kvnorm, wuq, wk, we, wv, tab_w, tab_q, tab_k):
    bsz, t, d = xin.shape
    n1 = w1.shape[1]
    hw = D_HEADS * LANES

    def tab_spec():
        return pl.BlockSpec((3, TM, LANES), lambda b, j: (0, jnp.maximum(j - 1, 0), 0))

    def full(a):
        return pl.BlockSpec(a.shape, lambda b, j: (0,) * a.ndim)

    def out(width):
        return pl.BlockSpec((None, TM, width), lambda b, j: (b, j, 0))

    widths = (C_Q_HEADS * C_HEAD_DIM, LANES, LANES, hw, hw, hw)
    return pl.pallas_call(
        _inproj_odd_kernel,
        grid=(bsz, t // TM),
        in_specs=[
            pl.BlockSpec((None, TM, d), lambda b, j: (b, j, 0)),
            pl.BlockSpec((None, None, 6, d), lambda b, j: (b, jnp.minimum(j, 1), 0, 0)),
            full(w1), full(qnorm), full(kvnorm), full(wuq), full(wk), full(we), full(wv),
            tab_spec(), tab_spec(), tab_spec(),
        ],
        out_specs=[out(w) for w in widths],
        out_shape=[jax.ShapeDtypeStruct((bsz, t, w), BF16) for w in widths],
        compiler_params=_cparams(2),
        name="inproj_odd",
    )(xin, modarr, w1, qnorm, kvnorm, wuq, wk, we, wv, tab_w, tab_q, tab_k)


def _win_kernel(sink_ref, q_ref, k_ref, v_ref, o_ref, klo_scr, khi_scr, *, n_ctx):
    i = pl.program_id(1)
    wdw = C_WINDOW
    t = k_ref.shape[0]
    lane = lax.broadcasted_iota(jnp.int32, (t, LANES), 1)

    @pl.when(i == 0)
    def _():
        kk = k_ref[...]
        klo_scr[...] = jnp.where(lane < C_HEAD_DIM, kk, jnp.zeros_like(kk))
        khi_scr[...] = jnp.where(lane >= C_HEAD_DIM, kk, jnp.zeros_like(kk))

    n_lat_blk = (t - n_ctx) // wdw
    blk0 = jnp.clip(i - 1, 0, n_lat_blk - 3)
    r0 = pl.multiple_of(n_ctx + blk0 * wdw, wdw)
    kpos = blk0 * wdw + lax.broadcasted_iota(jnp.int32, (wdw, 3 * wdw), 1)
    qpos = i * wdw + lax.broadcasted_iota(jnp.int32, (wdw, 3 * wdw), 0)
    near = jnp.abs(kpos - qpos) <= wdw
    v_loc = v_ref[pl.ds(r0, 3 * wdw), :]
    v_ctx = v_ref[0:n_ctx, :]
    olane = lax.broadcasted_iota(jnp.int32, (wdw, LANES), 1)
    n_rep = C_Q_HEADS // C_KV_HEADS
    for r in range(n_rep):
        q = q_ref[:, r * LANES:(r + 1) * LANES]
        outs = []
        for g, k_scr in enumerate((klo_scr, khi_scr)):
            k_loc = k_scr[pl.ds(r0, 3 * wdw), :]
            k_ctx = k_scr[0:n_ctx, :]
            s_loc = lax.dot_general(q, k_loc, (((1,), (1,)), ((), ())), preferred_element_type=F32)
            s_loc = jnp.where(near, s_loc, -jnp.inf)
            s_ctx = lax.dot_general(q, k_ctx, (((1,), (1,)), ((), ())), preferred_element_type=F32)
            sink = sink_ref[g * n_rep + r]
            m = jnp.maximum(jnp.maximum(jnp.max(s_loc, axis=-1, keepdims=True),
                                        jnp.max(s_ctx, axis=-1, keepdims=True)), sink)
            p_loc = jnp.exp(s_loc - m)
            p_ctx = jnp.exp(s_ctx - m)
            den = (jnp.sum(p_loc, axis=-1, keepdims=True) + jnp.sum(p_ctx, axis=-1, keepdims=True)
                   + jnp.exp(sink - m))
            pv = (jnp.dot(p_loc.astype(BF16), v_loc, preferred_element_type=F32)
                  + jnp.dot(p_ctx.astype(BF16), v_ctx, preferred_element_type=F32))
            outs.append(pv / den)
        o_ref[:, r * LANES:(r + 1) * LANES] = jnp.where(olane < C_HEAD_DIM, outs[0], outs[1]).astype(BF16)


def _win_attention(sink, qw, kw, vw, n_ctx):
    bsz, t, nq = qw.shape
    n_lat = t - n_ctx
    ctx_blk = n_ctx // C_WINDOW
    grid_spec = pltpu.PrefetchScalarGridSpec(
        num_scalar_prefetch=1,
        grid=(bsz, n_lat // C_WINDOW),
        in_specs=[
            pl.BlockSpec((None, C_WINDOW, nq), lambda b, i, s: (b, i + ctx_blk, 0)),
            pl.BlockSpec((None, t, LANES), lambda b, i, s: (b, 0, 0)),
            pl.BlockSpec((None, t, LANES), lambda b, i, s: (b, 0, 0)),
        ],
        out_specs=pl.BlockSpec((None, C_WINDOW, nq), lambda b, i, s: (b, i, 0)),
        scratch_shapes=[pltpu.VMEM((t, LANES), BF16), pltpu.VMEM((t, LANES), BF16)],
    )
    return pl.pallas_call(
        functools.partial(_win_kernel, n_ctx=n_ctx),
        grid_spec=grid_spec,
        out_shape=jax.ShapeDtypeStruct((bsz, n_lat, nq), BF16),
        compiler_params=_cparams(2),
        name="win_attention",
    )(sink, qw, kw, vw)


MLA_TQ = 256


def _mla_kernel(q_ref, k_ref, v_ref, o_ref):
    for hp in range(D_HEADS // 2):
        acc = None
        for hh in (2 * hp, 2 * hp + 1):
            sl = slice(hh * LANES, (hh + 1) * LANES)
            s = lax.dot_general(q_ref[:, sl], k_ref[:, sl], (((1,), (1,)), ((), ())),
                                preferred_element_type=F32)
            m = jnp.max(s, axis=-1, keepdims=True)
            p = jnp.exp(s - m)
            den = jnp.sum(p, axis=-1, keepdims=True)
            pv = jnp.dot(p.astype(BF16), v_ref[:, sl], preferred_element_type=F32) / den
            acc = pv if acc is None else acc + pv
        o_ref[:, hp * LANES:(hp + 1) * LANES] = acc.astype(BF16)


def _mla_attention(qm, km, vm, n_ctx):
    bsz, t, hw = qm.shape
    n_lat = t - n_ctx
    ctx_blk = n_ctx // MLA_TQ
    ow = D_HEADS * D_V
    return pl.pallas_call(
        _mla_kernel,
        grid=(bsz, n_lat // MLA_TQ),
        in_specs=[
            pl.BlockSpec((None, MLA_TQ, hw), lambda b, i: (b, i + ctx_blk, 0)),
            pl.BlockSpec((None, t, hw), lambda b, i: (b, 0, 0)),
            pl.BlockSpec((None, t, hw), lambda b, i: (b, 0, 0)),
        ],
        out_specs=pl.BlockSpec((None, MLA_TQ, ow), lambda b, i: (b, i, 0)),
        out_shape=jax.ShapeDtypeStruct((bsz, n_lat, ow), BF16),
        compiler_params=_cparams(2),
        name="mla_attention",
    )(qm, km, vm)


def _rope_tables(n_tokens, rot_dim, group, offset):
    t = jnp.arange(n_tokens)
    rows = (t // GRID_W).astype(F32)
    cols = (t % GRID_W).astype(F32)
    n_freq = rot_dim // 4
    inv_freq = ROPE_BASE ** (-jnp.arange(n_freq, dtype=F32) / n_freq)
    ang = jnp.concatenate([rows[:, None] * inv_freq, cols[:, None] * inv_freq], -1)
    cos, sin = jnp.cos(ang), jnp.sin(ang)
    half = rot_dim // 2
    c = jnp.ones((n_tokens, LANES), F32)
    s1 = jnp.zeros((n_tokens, LANES), F32)
    s2 = jnp.zeros((n_tokens, LANES), F32)
    for start in range(offset, LANES, group):
        c = c.at[:, start:start + half].set(cos).at[:, start + half:start + rot_dim].set(cos)
        s1 = s1.at[:, start:start + half].set(-sin)
        s2 = s2.at[:, start + half:start + rot_dim].set(sin)
    return jnp.stack([c, s1, s2])


def _odd_weights(w_in, wuq, wukv, w_out):
    d = w_in.shape[0]
    nq = C_Q_HEADS * C_HEAD_DIM
    nkv = C_KV_HEADS * C_HEAD_DIM
    n_rep = C_Q_HEADS // C_KV_HEADS
    order = [g * n_rep + r for r in range(n_rep) for g in range(C_KV_HEADS)]
    cq = w_in[:, :nq].reshape(d, C_Q_HEADS, C_HEAD_DIM)[:, order].reshape(d, nq)
    rest = w_in[:, nq:nq + 2 * nkv + D_Q_RANK + D_KV_RANK]
    krope = jnp.pad(w_in[:, nq + 2 * nkv + D_Q_RANK + D_KV_RANK:], ((0, 0), (0, LANES - D_ROPE)))
    w1 = jnp.concatenate([cq, rest, krope], axis=1).astype(BF16)
    qh = wuq.reshape(D_Q_RANK, D_HEADS, D_NOPE + D_ROPE)
    wuq_p = jnp.pad(qh, ((0, 0), (0, 0), (0, LANES - D_NOPE - D_ROPE))).reshape(D_Q_RANK, D_HEADS * LANES)
    kvh = wukv.reshape(D_KV_RANK, D_HEADS, D_NOPE + D_V)
    wk_p = jnp.pad(kvh[:, :, :D_NOPE], ((0, 0), (0, 0), (0, LANES - D_NOPE))).reshape(D_KV_RANK, D_HEADS * LANES)
    e_blk = jnp.zeros((LANES, LANES), F32).at[jnp.arange(D_ROPE), D_NOPE + jnp.arange(D_ROPE)].set(1.0)
    we = jnp.tile(e_blk, (1, D_HEADS))
    vh = kvh[:, :, D_NOPE:]
    even = (jnp.arange(D_HEADS) % 2 == 0)[None, :, None]
    wv_p = jnp.where(even, jnp.pad(vh, ((0, 0), (0, 0), (0, D_V))),
                     jnp.pad(vh, ((0, 0), (0, 0), (D_V, 0)))).reshape(D_KV_RANK, D_HEADS * LANES)
    wo_win = w_out[:nq].reshape(C_Q_HEADS, C_HEAD_DIM, -1)[jnp.array(order)].reshape(nq, -1)
    wo = jnp.concatenate([wo_win, w_out[nq:]], axis=0).astype(BF16)
    return w1, wuq_p.astype(BF16), wk_p.astype(BF16), we.astype(BF16), wv_p.astype(BF16), wo


def kernel(x, c, ctx, c_ctx, ada_w, ada_b, ln_g, ln_b, ev_w_in, ev_a_conv, ev_b_conv, ev_b_alog, ev_b_dtbias, ev_b_norm, ev_w_out, od_w_in, od_c_sink, od_d_qnorm, od_d_kvnorm, od_d_wuq, od_d_wukv, od_w_out, router_w, router_bias, moe_w_gate, moe_w_up, moe_w_down):
    bsz, n_lat, d = x.shape
    n_ctx = ctx.shape[1]
    assert n_ctx == TM and n_lat % TM == 0 and n_lat % GRID_W == 0
    assert ada_w.shape[0] == DEPTH and bsz + 1 <= 40
    t = n_ctx + n_lat

    cs = jnp.zeros((40, d), F32).at[:bsz].set(c).at[bsz].set(c_ctx)
    mods = _ada_mod(cs, ada_w, ada_b)

    def modarr(layer):
        m = mods[layer].reshape(40, 6, d)
        return jnp.stack([jnp.broadcast_to(m[bsz], (bsz, 6, d)), m[:bsz]], axis=1)

    rwt = router_w.T
    rbias = router_bias.reshape(N_EXPERTS, 1)
    xcat = jnp.concatenate([ctx, x], axis=1)

    mod0 = modarr(0)
    n_main = 3 * A_WIDTH + 4 * B_WIDTH
    w_main = ev_w_in[0][:, :n_main].astype(BF16)
    w_small = jnp.pad(ev_w_in[0][:, n_main:], ((0, 0), (0, LANES - 4 * B_HEADS))).astype(BF16)
    p, small = _inproj_even(xcat, mod0, w_main, w_small)
    alog_pad = jnp.zeros((1, LANES), F32).at[0, 8:16].set(ev_b_alog[0].reshape(-1))
    dtb_pad = jnp.zeros((1, LANES), F32).at[0, 8:16].set(ev_b_dtbias[0].reshape(-1))
    ya, u, w, qe, ket, att, dec = _even_prep(p, small, ev_a_conv[0], ev_b_conv[0], alog_pad, dtb_pad,
                                             GDN_CHUNK)
    o_fwd, o_rev = _gdn_rec(u, w, qe, ket, att, dec, GDN_CHUNK)
    x1, h1, gates = _outproj(ya, None, xcat, mod0, ev_w_out[0].astype(BF16),
                             ln_g[0, 0].reshape(1, d), ln_b[0, 0].reshape(1, d), rwt, rbias, 0,
                             gdn=(o_fwd, o_rev, p, ev_b_norm[0].reshape(1, B_HEAD_DIM)))
    f = _moe_dense(h1.reshape(bsz * t, d), gates.reshape(bsz * t, LANES),
                   moe_w_gate[0].astype(BF16), moe_w_up[0].astype(BF16), moe_w_down[0].astype(BF16))
    x2 = _ln2(x1, f.reshape(bsz, t, d), mod0, ln_g[0, 1].reshape(1, d), ln_b[0, 1].reshape(1, d), 0)

    mod1 = modarr(1)
    w1, wuq_p, wk_p, we, wv_p, wo = _odd_weights(od_w_in[0], od_d_wuq[0], od_d_wukv[0], od_w_out[0])
    tab_w = _rope_tables(n_lat, C_HEAD_DIM, C_HEAD_DIM, 0)
    tab_q = _rope_tables(n_lat, D_ROPE, LANES, D_NOPE)
    tab_k = _rope_tables(n_lat, D_ROPE, LANES, 0)
    qw, kw, vw, qm, km, vm = _inproj_odd(
        x2, mod1, w1, od_d_qnorm[0].reshape(1, -1), od_d_kvnorm[0].reshape(1, -1),
        wuq_p, wk_p, we, wv_p, tab_w, tab_q, tab_k)
    y_win = _win_attention(od_c_sink[0], qw, kw, vw, n_ctx)
    y_mla = _mla_attention(qm, km, vm, n_ctx)
    x3, h3, gates1 = _outproj(y_win, y_mla, x2, mod1, wo, ln_g[1, 0].reshape(1, d),
                              ln_b[1, 0].reshape(1, d), rwt, rbias, n_ctx // TM)
    f1 = _moe_dense(h3.reshape(bsz * n_lat, d), gates1.reshape(bsz * n_lat, LANES),
                    moe_w_gate[1].astype(BF16), moe_w_up[1].astype(BF16), moe_w_down[1].astype(BF16))
    return _ln2(x3, f1.reshape(bsz, n_lat, d), mod1, ln_g[1, 1].reshape(1, d), ln_b[1, 1].reshape(1, d), 1)
```

```python
import functools

import numpy as np
import jax
import jax.numpy as jnp
from jax import lax
from jax.experimental import pallas as pl
from jax.experimental.pallas import tpu as pltpu

F32 = jnp.float32
BF16 = jnp.bfloat16
HIGHEST = lax.Precision.HIGHEST

DEPTH = 2
GRID_W = 64
DEEPNORM_ALPHA = (2.0 * DEPTH) ** 0.25
LN_EPS = 1e-5
RMS_EPS = 1e-6
ROPE_BASE = 10000.0
B_HEADS = 4
B_HEAD_DIM = 128
B_WIDTH = 512
A_WIDTH = 512
C_Q_HEADS = 8
C_KV_HEADS = 2
C_HEAD_DIM = 64
C_WINDOW = 128
D_HEADS = 8
D_NOPE = 64
D_ROPE = 32
D_V = 64
D_Q_RANK = 384
D_KV_RANK = 256
N_EXPERTS = 16
N_GROUPS = 4
PER_GROUP = N_EXPERTS // N_GROUPS
D_EXPERT = 512
N_PAIRS = PER_GROUP * (PER_GROUP - 1) // 2
N_CLASSES = N_GROUPS * N_PAIRS
CLS_PAD = 32

LANES = 128
TM = 256
GDN_CHUNK = 64
MOE_TM = 256
VMEM_LIMIT = 56 * 1024 * 1024


def _cparams(n_axes, vmem=VMEM_LIMIT):
    return pltpu.CompilerParams(dimension_semantics=("arbitrary",) * n_axes, vmem_limit_bytes=vmem)


def _sigmoid(x):
    return 1.0 / (1.0 + jnp.exp(-x))


def _silu(x):
    return x * _sigmoid(x)


def _softplus(x):
    return jnp.maximum(x, 0.0) + jnp.log(1.0 + jnp.exp(-jnp.abs(x)))


def _layer_norm(v, g, b):
    mu = jnp.mean(v, axis=-1, keepdims=True)
    d = v - mu
    var = jnp.mean(d * d, axis=-1, keepdims=True)
    return d * lax.rsqrt(var + LN_EPS) * g + b


def _ada_kernel(c_ref, w_ref, b_ref, o_ref):
    s = _silu(c_ref[...])
    o_ref[...] = jnp.dot(s, w_ref[...], precision=HIGHEST, preferred_element_type=F32) + b_ref[...]


def _ada_mod(cs, ada_w, ada_b):
    depth, d, n6 = ada_w.shape
    rows = cs.shape[0]
    tn = 1536
    return pl.pallas_call(
        _ada_kernel,
        grid=(depth, n6 // tn),
        in_specs=[
            pl.BlockSpec((rows, d), lambda l, n: (0, 0)),
            pl.BlockSpec((None, d, tn), lambda l, n: (l, 0, n)),
            pl.BlockSpec((None, 1, tn), lambda l, n: (l, 0, n)),
        ],
        out_specs=pl.BlockSpec((None, rows, tn), lambda l, n: (l, 0, n)),
        out_shape=jax.ShapeDtypeStruct((depth, rows, n6), F32),
        compiler_params=_cparams(2),
        name="ada_mod",
    )(cs, ada_w, ada_b.reshape(depth, 1, n6))


def _inproj_even_kernel(x_ref, mod_ref, wm_ref, ws_ref, p_ref, s_ref):
    mod = mod_ref[...]
    h = (x_ref[...] * (1.0 + mod[1:2]) + mod[0:1]).astype(BF16)
    p_ref[...] = jnp.dot(h, wm_ref[...], preferred_element_type=F32).astype(BF16)
    s_ref[...] = jnp.dot(h, ws_ref[...], preferred_element_type=F32)


def _inproj_even(xcat, modarr, w_main, w_small):
    bsz, t, d = xcat.shape
    nm = w_main.shape[1]
    return pl.pallas_call(
        _inproj_even_kernel,
        grid=(bsz, t // TM),
        in_specs=[
            pl.BlockSpec((None, TM, d), lambda b, j: (b, j, 0)),
            pl.BlockSpec((None, None, 6, d), lambda b, j: (b, jnp.minimum(j, 1), 0, 0)),
            pl.BlockSpec((d, nm), lambda b, j: (0, 0)),
            pl.BlockSpec((d, LANES), lambda b, j: (0, 0)),
        ],
        out_specs=[
            pl.BlockSpec((None, TM, nm), lambda b, j: (b, j, 0)),
            pl.BlockSpec((None, TM, LANES), lambda b, j: (b, j, 0)),
        ],
        out_shape=[
            jax.ShapeDtypeStruct((bsz, t, nm), BF16),
            jax.ShapeDtypeStruct((bsz, t, LANES), F32),
        ],
        compiler_params=_cparams(2),
        name="inproj_even",
    )(xcat, modarr, w_main, w_small)


HALO = 16


def _conv3(z, zp, zn, w):
    n = z.shape[0]
    rows = lax.broadcasted_iota(jnp.int32, z.shape, 0)
    zprev = jnp.where(rows == 0, zp, pltpu.roll(z, 1, 0))
    znext = jnp.where(rows == n - 1, zn, pltpu.roll(z, n - 1, 0))
    return w[0:1] * zprev + w[1:2] * z + w[2:3] * znext


def _even_prep_kernel(p_ref, pp_ref, pn_ref, s_ref, aw_ref, bw_ref, alog_ref, dtb_ref,
                      ya_ref, u_ref, w_ref, qe_ref, ket_ref, att_ref, dec_ref,
                      q_scr, k_scr, v_scr, *, chunk):
    j = pl.program_id(1)
    nj = pl.num_programs(1)
    prev_on = jnp.where(jnp.logical_and(j != 0, j != 1), 1.0, 0.0)
    next_on = jnp.where(jnp.logical_and(j != 0, j != nj - 1), 1.0, 0.0)
    prow = pp_ref[...].astype(F32)[HALO - 1:HALO] * prev_on
    nrow = pn_ref[...].astype(F32)[0:1] * next_on

    def seg(lo, hi):
        return p_ref[:, lo:hi].astype(F32), prow[:, lo:hi], nrow[:, lo:hi]

    a0, _, _ = seg(0, A_WIDTH)
    a1, a1p, a1n = seg(A_WIDTH, 2 * A_WIDTH)
    a2, a2p, a2n = seg(2 * A_WIDTH, 3 * A_WIDTH)
    ya_ref[...] = (a0 * _conv3(a1 * a2, a1p * a2p, a1n * a2n, aw_ref[...])).astype(BF16)

    base = 3 * A_WIDTH
    for which in range(3):
        lo = base + which * B_WIDTH
        z, zp, zn = seg(lo, lo + B_WIDTH)
        c = _silu(_conv3(z, zp, zn, bw_ref[:, which * B_WIDTH:(which + 1) * B_WIDTH]))
        for h in range(B_HEADS):
            ch = c[:, h * B_HEAD_DIM:(h + 1) * B_HEAD_DIM]
            if which < 2:
                ss = jnp.sum(ch * ch, axis=-1, keepdims=True)
                ch = ch * lax.rsqrt(ss + 1e-6)
                if which == 0:
                    ch = ch * (B_HEAD_DIM ** -0.5)
            sl = slice(h * B_HEAD_DIM, (h + 1) * B_HEAD_DIM)
            if which == 0:
                q_scr[:, sl] = ch.astype(BF16)
            elif which == 1:
                k_scr[:, sl] = ch.astype(BF16)
            else:
                v_scr[:, sl] = ch

    s = s_ref[...]
    beta = _sigmoid(s)
    g = -jnp.exp(alog_ref[...]) * _softplus(s + dtb_ref[...])
    n = s.shape[0]
    nck = n // chunk
    ri = lax.broadcasted_iota(jnp.int32, (n, n), 0)
    ci = lax.broadcasted_iota(jnp.int32, (n, n), 1)
    same = (ri // chunk) == (ci // chunk)
    m_fwd = jnp.where(jnp.logical_and(same, ci <= ri), 1.0, 0.0)
    m_rev = jnp.where(jnp.logical_and(same, ci >= ri), 1.0, 0.0)
    m_all = jnp.where(same, 1.0, 0.0)
    gc_f = jnp.dot(m_fwd, g, precision=HIGHEST, preferred_element_type=F32)
    gc_r = jnp.dot(m_rev, g, precision=HIGHEST, preferred_element_type=F32)
    tot = jnp.dot(m_all, g, precision=HIGHEST, preferred_element_type=F32)
    lane = lax.broadcasted_iota(jnp.int32, s.shape, 1)
    gc = jnp.where(lane >= 8 + B_HEADS, gc_r, gc_f)
    e_gc = jnp.exp(gc)
    e_rest = jnp.exp(tot - gc)
    gct = gc.T
    e_tot = jnp.exp(tot)

    ri = lax.broadcasted_iota(jnp.int32, (1, chunk, chunk), 1)
    ci = lax.broadcasted_iota(jnp.int32, (1, chunk, chunk), 2)
    eye = jnp.where(ri == ci, 1.0, 0.0)
    n_sq = int(np.log2(chunk))
    nt_batched = (((2,), (2,)), ((0,), (0,)))
    nn_batched = (((2,), (1,)), ((0,), (0,)))
    kk, qk, kf, qf = [], [], [], []
    for h in range(B_HEADS):
        hs = slice(h * B_HEAD_DIM, (h + 1) * B_HEAD_DIM)
        k3 = k_scr[:, hs].reshape(nck, chunk, B_HEAD_DIM)
        q3 = q_scr[:, hs].reshape(nck, chunk, B_HEAD_DIM)
        kk.append(lax.dot_general(k3, k3, nt_batched, preferred_element_type=F32))
        qk.append(lax.dot_general(q3, k3, nt_batched, preferred_element_type=F32))
        kf.append(k_scr[:, hs].astype(F32))
        qf.append(q_scr[:, hs].astype(F32))
    a_all, rhs_all = [], []
    for d in range(2):
        incl = (ci <= ri) if d == 0 else (ci >= ri)
        strict = (ci < ri) if d == 0 else (ci > ri)
        for h in range(B_HEADS):
            chain = d * B_HEADS + h
            hs = slice(h * B_HEAD_DIM, (h + 1) * B_HEAD_DIM)
            bcol = beta[:, chain:chain + 1]
            e1 = e_gc[:, 8 + chain:9 + chain]
            e2 = e_rest[:, 8 + chain:9 + chain]
            gcol = gc[:, 8 + chain:9 + chain].reshape(nck, chunk, 1)
            grow = jnp.stack([gct[8 + chain:9 + chain, cc * chunk:(cc + 1) * chunk]
                              for cc in range(nck)], axis=0)
            decay = jnp.exp(jnp.where(incl, gcol - grow, -jnp.inf))
            a_all.append(jnp.where(strict, bcol.reshape(nck, chunk, 1) * kk[h] * decay, 0.0))
            att_ref[d, h] = (qk[h] * decay).reshape(n, chunk).astype(BF16)
            qe_ref[d, h] = (qf[h] * e1).astype(BF16)
            ket = (kf[h] * e2).T
            for cc in range(nck):
                ket_ref[d, h, cc] = ket[:, cc * chunk:(cc + 1) * chunk].astype(BF16)
                dec_ref[cc, chain] = jnp.broadcast_to(
                    e_tot[cc * chunk:cc * chunk + 1, 8 + chain:9 + chain], (1, LANES))
            rhs = jnp.concatenate([(v_scr[:, hs] * bcol).astype(BF16),
                                   (kf[h] * (bcol * e1)).astype(BF16)], axis=1)
            rhs_all.append(rhs.reshape(nck, chunk, 2 * B_HEAD_DIM))
    npow = -jnp.concatenate(a_all, axis=0)
    tinv = eye + npow
    for _ in range(n_sq - 1):
        nb = npow.astype(BF16)
        npow = lax.dot_general(nb, nb, nn_batched, preferred_element_type=F32)
        tinv = tinv + lax.dot_general(tinv.astype(BF16), npow.astype(BF16), nn_batched,
                                      preferred_element_type=F32)
    uw = lax.dot_general(tinv.astype(BF16), jnp.concatenate(rhs_all, axis=0), nn_batched,
                         preferred_element_type=F32)
    for d in range(2):
        for h in range(B_HEADS):
            blk = uw[(d * B_HEADS + h) * nck:(d * B_HEADS + h + 1) * nck]
            u_ref[d, h] = blk[:, :, :B_HEAD_DIM].reshape(n, B_HEAD_DIM).astype(BF16)
            w_ref[d, h] = blk[:, :, B_HEAD_DIM:].reshape(n, B_HEAD_DIM).astype(BF16)


def _even_prep(p, small, a_conv, b_conv, alog_pad, dtb_pad, chunk):
    bsz, t, nm = p.shape
    nhb = TM // HALO
    last_hb = t // HALO - 1
    nck = TM // chunk
    nc = t // chunk

    def per_dir():
        return (pl.BlockSpec((None, 2, B_HEADS, TM, B_HEAD_DIM), lambda b, j: (b, 0, 0, j, 0)),
                jax.ShapeDtypeStruct((bsz, 2, B_HEADS, t, B_HEAD_DIM), BF16))

    outs = [
        (pl.BlockSpec((None, TM, A_WIDTH), lambda b, j: (b, j, 0)),
         jax.ShapeDtypeStruct((bsz, t, A_WIDTH), BF16)),
        per_dir(), per_dir(), per_dir(),
        (pl.BlockSpec((None, 2, B_HEADS, nck, B_HEAD_DIM, chunk), lambda b, j: (b, 0, 0, j, 0, 0)),
         jax.ShapeDtypeStruct((bsz, 2, B_HEADS, nc, B_HEAD_DIM, chunk), BF16)),
        (pl.BlockSpec((None, 2, B_HEADS, TM, chunk), lambda b, j: (b, 0, 0, j, 0)),
         jax.ShapeDtypeStruct((bsz, 2, B_HEADS, t, chunk), BF16)),
        (pl.BlockSpec((None, nck, 2 * B_HEADS, 1, LANES), lambda b, j: (b, j, 0, 0, 0)),
         jax.ShapeDtypeStruct((bsz, nc, 2 * B_HEADS, 1, LANES), F32)),
    ]
    return pl.pallas_call(
        functools.partial(_even_prep_kernel, chunk=chunk),
        grid=(bsz, t // TM),
        in_specs=[
            pl.BlockSpec((None, TM, nm), lambda b, j: (b, j, 0)),
            pl.BlockSpec((None, HALO, nm), lambda b, j: (b, jnp.maximum(j * nhb - 1, 0), 0)),
            pl.BlockSpec((None, HALO, nm), lambda b, j: (b, jnp.minimum((j + 1) * nhb, last_hb), 0)),
            pl.BlockSpec((None, TM, LANES), lambda b, j: (b, j, 0)),
            pl.BlockSpec((3, A_WIDTH), lambda b, j: (0, 0)),
            pl.BlockSpec((3, 3 * B_WIDTH), lambda b, j: (0, 0)),
            pl.BlockSpec((1, LANES), lambda b, j: (0, 0)),
            pl.BlockSpec((1, LANES), lambda b, j: (0, 0)),
        ],
        out_specs=[o[0] for o in outs],
        out_shape=[o[1] for o in outs],
        scratch_shapes=[pltpu.VMEM((TM, B_WIDTH), BF16), pltpu.VMEM((TM, B_WIDTH), BF16),
                        pltpu.VMEM((TM, B_WIDTH), F32)],
        compiler_params=_cparams(2),
        name="even_prep",
    )(p, p, p, small, a_conv, b_conv, alog_pad, dtb_pad)


def _gdn_rec_kernel(uf, wf, qf, kf, af, df, ur, wr, qr, kr, ar, dr, of_ref, or_ref, s_scr, *, chunk):
    @pl.when(pl.program_id(1) == 0)
    def _():
        s_scr[...] = jnp.zeros_like(s_scr)

    nck = uf.shape[1] // chunk
    nn_batched = (((2,), (1,)), ((0,), (0,)))

    def both(fwd, rev):
        return jnp.concatenate([fwd, rev], axis=0)

    for cc in range(nck):
        cr = nck - 1 - cc
        rf = slice(cc * chunk, (cc + 1) * chunk)
        rr = slice(cr * chunk, (cr + 1) * chunk)
        s = s_scr[...]
        sb = s.astype(BF16)
        u = both(uf[:, rf, :], ur[:, rr, :]).astype(F32)
        v_new = u - lax.dot_general(both(wf[:, rf, :], wr[:, rr, :]), sb, nn_batched,
                                    preferred_element_type=F32)
        vb = v_new.astype(BF16)
        o = (lax.dot_general(both(qf[:, rf, :], qr[:, rr, :]), sb, nn_batched,
                             preferred_element_type=F32)
             + lax.dot_general(both(af[:, rf, :], ar[:, rr, :]), vb, nn_batched,
                               preferred_element_type=F32))
        dec = both(df[cc, 0:B_HEADS], dr[cr, B_HEADS:2 * B_HEADS])
        s_scr[...] = s * dec + lax.dot_general(both(kf[:, cc], kr[:, cr]), vb, nn_batched,
                                               preferred_element_type=F32)
        for h in range(B_HEADS):
            hs = slice(h * B_HEAD_DIM, (h + 1) * B_HEAD_DIM)
            of_ref[rf, hs] = o[h].astype(BF16)
            or_ref[rr, hs] = o[B_HEADS + h].astype(BF16)


def _gdn_rec(u, w, qe, ket, att, dec, chunk):
    bsz, _, _, t, _ = u.shape
    nt = t // TM
    nck = TM // chunk

    def tile(d, s):
        return s if d == 0 else jnp.where(s == 0, 0, nt - s)

    in_specs = []
    for d in range(2):
        for _ in range(3):
            in_specs.append(pl.BlockSpec((None, None, B_HEADS, TM, B_HEAD_DIM),
                                         lambda b, s, d=d: (b, d, 0, tile(d, s), 0)))
        in_specs.append(pl.BlockSpec((None, None, B_HEADS, nck, B_HEAD_DIM, chunk),
                                     lambda b, s, d=d: (b, d, 0, tile(d, s), 0, 0)))
        in_specs.append(pl.BlockSpec((None, None, B_HEADS, TM, chunk),
                                     lambda b, s, d=d: (b, d, 0, tile(d, s), 0)))
        in_specs.append(pl.BlockSpec((None, nck, 2 * B_HEADS, 1, LANES),
                                     lambda b, s, d=d: (b, tile(d, s), 0, 0, 0)))
    return pl.pallas_call(
        functools.partial(_gdn_rec_kernel, chunk=chunk),
        grid=(bsz, nt),
        in_specs=in_specs,
        out_specs=[pl.BlockSpec((None, TM, B_WIDTH), lambda b, s, d=d: (b, tile(d, s), 0))
                   for d in range(2)],
        out_shape=[jax.ShapeDtypeStruct((bsz, t, B_WIDTH), BF16)] * 2,
        scratch_shapes=[pltpu.VMEM((2 * B_HEADS, B_HEAD_DIM, B_HEAD_DIM), F32)],
        compiler_params=_cparams(2),
        name="gdn_rec",
    )(u, w, qe, ket, att, dec, u, w, qe, ket, att, dec)


def _route(logits_t, bias, sel_scr):
    aff = _sigmoid(logits_t)
    sel_scr[...] = aff + bias
    n = logits_t.shape[1]
    sel = [sel_scr[e:e + 1, :] for e in range(N_EXPERTS)]
    affr = [aff[e:e + 1, :] for e in range(N_EXPERTS)]
    in_top2 = []
    gscore = []
    for g in range(N_GROUPS):
        ids = range(g * PER_GROUP, (g + 1) * PER_GROUP)
        gs = jnp.zeros((1, n), F32)
        for e in ids:
            rank = jnp.zeros((1, n), F32)
            for e2 in ids:
                if e2 == e:
                    continue
                ahead = (sel[e2] >= sel[e]) if e2 < e else (sel[e2] > sel[e])
                rank = rank + jnp.where(ahead, 1.0, 0.0)
            top = rank < 2.0
            in_top2.append(top)
            gs = gs + jnp.where(top, sel[e], 0.0)
        gscore.append(gs)
    best = jnp.zeros((1, n), jnp.int32)
    bestv = gscore[0]
    for g in range(1, N_GROUPS):
        better = gscore[g] > bestv
        best = jnp.where(better, g, best)
        bestv = jnp.where(better, gscore[g], bestv)
    chosen = [in_top2[e] & (best == e // PER_GROUP) for e in range(N_EXPERTS)]
    denom = jnp.zeros((1, n), F32)
    for e in range(N_EXPERTS):
        denom = denom + jnp.where(chosen[e], affr[e], 0.0)
    lo = jnp.full((1, n), float(PER_GROUP), F32)
    hi = jnp.zeros((1, n), F32)
    w_lo = jnp.zeros((1, n), F32)
    w_hi = jnp.zeros((1, n), F32)
    for e in reversed(range(N_EXPERTS)):
        pos = float(e % PER_GROUP)
        lo = jnp.where(chosen[e], jnp.minimum(lo, pos), lo)
    for e in range(N_EXPERTS):
        pos = float(e % PER_GROUP)
        hi = jnp.where(chosen[e], jnp.maximum(hi, pos), hi)
    for e in range(N_EXPERTS):
        pos = float(e % PER_GROUP)
        gate = affr[e] / denom
        w_lo = jnp.where(chosen[e] & (lo == pos), gate, w_lo)
        w_hi = jnp.where(chosen[e] & (hi == pos), gate, w_hi)
    pair = lo * (2 * PER_GROUP - 1 - lo) * 0.5 + hi - lo - 1.0
    cls = best.astype(F32) * float(N_PAIRS) + pair
    return cls, w_lo, w_hi


def _outproj_even_kernel(ya_ref, of_ref, or_ref, gate_ref, bn_ref, *rest):
    parts = []
    for h in range(B_HEADS):
        hs = slice(h * B_HEAD_DIM, (h + 1) * B_HEAD_DIM)
        o = of_ref[:, hs].astype(F32) + or_ref[:, hs].astype(F32)
        y = o * lax.rsqrt(jnp.mean(o * o, axis=-1, keepdims=True) + RMS_EPS) * bn_ref[...]
        parts.append((y * _silu(gate_ref[:, hs].astype(F32))).astype(BF16))
    _outproj_body(ya_ref[...], jnp.concatenate(parts, axis=1), *rest)


def _outproj_kernel(ya_ref, yb_ref, *rest):
    _outproj_body(ya_ref[...], yb_ref[...], *rest)


def _outproj_body(ya, yb, x_ref, mod_ref, w_ref, lng_ref, lnb_ref, rwt_ref, rb_ref,
                  xo_ref, h_ref, info_ref, cnt_ref, sel_scr, run_scr):
    mod = mod_ref[...]
    wa = w_ref[0:ya.shape[1], :]
    wb = w_ref[ya.shape[1]:, :]
    y = (jnp.dot(ya, wa, preferred_element_type=F32)
         + jnp.dot(yb, wb, preferred_element_type=F32))
    xn = _layer_norm(DEEPNORM_ALPHA * x_ref[...] + mod[2:3] * y, lng_ref[...], lnb_ref[...])
    xo_ref[...] = xn
    h = xn * (1.0 + mod[4:5]) + mod[3:4]
    h_ref[...] = h
    logits_t = lax.dot_general(rwt_ref[...], h, (((1,), (1,)), ((), ())),
                               precision=HIGHEST, preferred_element_type=F32)
    cls, w_lo, w_hi = _route(logits_t, rb_ref[...], sel_scr)

    @pl.when(jnp.logical_and(pl.program_id(0) == 0, pl.program_id(1) == 0))
    def _():
        run_scr[...] = jnp.zeros_like(run_scr)

    n = cls.shape[1]
    crow = lax.broadcasted_iota(jnp.int32, (CLS_PAD, n), 0).astype(F32)
    onehot = jnp.where(crow == cls, 1.0, 0.0)
    si = lax.broadcasted_iota(jnp.int32, (n, n), 0)
    ti = lax.broadcasted_iota(jnp.int32, (n, n), 1)
    before = jnp.where(si < ti, 1.0, 0.0).astype(BF16)
    cum = jnp.dot(onehot.astype(BF16), before, preferred_element_type=F32)
    run = run_scr[...]
    rank = jnp.sum(onehot * (cum + run[:, 0:1]), axis=0, keepdims=True)
    run = run + jnp.sum(onehot, axis=1, keepdims=True)
    run_scr[...] = run
    cnt_ref[...] = run
    info_ref[...] = jnp.zeros_like(info_ref)
    info_ref[0:1, :] = cls
    info_ref[1:2, :] = rank
    info_ref[2:3, :] = w_lo
    info_ref[3:4, :] = w_hi


def _outproj(ya, yb, xres, modarr, w_out, ln_g, ln_b, rwt, rbias, row_blk0, gdn=None):
    bsz, n, wa = ya.shape
    d = xres.shape[2]

    def tok(width, col_blk=0):
        return pl.BlockSpec((None, TM, width), lambda b, j: (b, j, col_blk))

    if gdn is None:
        body, mix_args, mix_specs = _outproj_kernel, (ya, yb), [tok(wa), tok(yb.shape[2])]
    else:
        o_fwd, o_rev, p, b_norm = gdn
        gate_blk = (3 * A_WIDTH + 3 * B_WIDTH) // B_WIDTH
        body, mix_args = _outproj_even_kernel, (ya, o_fwd, o_rev, p, b_norm)
        mix_specs = [tok(wa), tok(B_WIDTH), tok(B_WIDTH), tok(B_WIDTH, gate_blk),
                     pl.BlockSpec((1, B_HEAD_DIM), lambda b, j: (0, 0))]
    return pl.pallas_call(
        body,
        grid=(bsz, n // TM),
        in_specs=mix_specs + [
            pl.BlockSpec((None, TM, d), lambda b, j: (b, j + row_blk0, 0)),
            pl.BlockSpec((None, None, 6, d), lambda b, j: (b, jnp.minimum(j + row_blk0, 1), 0, 0)),
            pl.BlockSpec(w_out.shape, lambda b, j: (0, 0)),
            pl.BlockSpec((1, d), lambda b, j: (0, 0)),
            pl.BlockSpec((1, d), lambda b, j: (0, 0)),
            pl.BlockSpec((N_EXPERTS, d), lambda b, j: (0, 0)),
            pl.BlockSpec((N_EXPERTS, 1), lambda b, j: (0, 0)),
        ],
        out_specs=[
            pl.BlockSpec((None, TM, d), lambda b, j: (b, j, 0)),
            pl.BlockSpec((None, TM, d), lambda b, j: (b, j, 0)),
            pl.BlockSpec((None, None, 8, TM), lambda b, j: (b, j, 0, 0)),
            pl.BlockSpec((CLS_PAD, LANES), lambda b, j: (0, 0)),
        ],
        out_shape=[
            jax.ShapeDtypeStruct((bsz, n, d), F32),
            jax.ShapeDtypeStruct((bsz, n, d), F32),
            jax.ShapeDtypeStruct((bsz, n // TM, 8, TM), F32),
            jax.ShapeDtypeStruct((CLS_PAD, LANES), F32),
        ],
        scratch_shapes=[pltpu.VMEM((N_EXPERTS, TM), F32), pltpu.VMEM((CLS_PAD, LANES), F32)],
        compiler_params=_cparams(2),
        name="outproj",
    )(*mix_args, xres, modarr, w_out, ln_g, ln_b, rwt, rbias)


def _moe_kernel(elo_ref, ehi_ref, nused_ref, gcur_ref, gnxt_ref, sdst_ref, wt_ref, h_hbm,
                wg_lo, wu_lo, wd_lo, wg_hi, wu_hi, wd_hi, out_hbm,
                xbuf, obuf, wt_scr, gsem, ssem):
    i = pl.program_id(0)
    last = pl.num_programs(0) - 1
    n_used = nused_ref[0]
    slot = i % 2
    mt = xbuf.shape[1]

    def gather_start(idx_ref, s):
        def body(r, carry):
            pltpu.make_async_copy(h_hbm.at[pl.ds(idx_ref[0, r], 1)], xbuf.at[s, pl.ds(r, 1)],
                                  gsem.at[s]).start()
            return carry
        lax.fori_loop(0, mt, body, 0, unroll=8)

    def gather_wait(s):
        pltpu.make_async_copy(h_hbm.at[pl.ds(0, mt)], xbuf.at[s], gsem.at[s]).wait()

    def scatter_start(s):
        def body(r, carry):
            pltpu.make_async_copy(obuf.at[s, pl.ds(r, 1)], out_hbm.at[pl.ds(sdst_ref[0, r], 1)],
                                  ssem.at[s]).start()
            return carry
        lax.fori_loop(0, mt, body, 0, unroll=8)

    def scatter_wait(s):
        pltpu.make_async_copy(obuf.at[s], out_hbm.at[pl.ds(0, mt)], ssem.at[s]).wait()

    @pl.when(i == 0)
    def _():
        gather_start(gcur_ref, 0)
        obuf[...] = jnp.zeros_like(obuf)
        n_tok = out_hbm.shape[0] - 2 * mt
        for s in range(2):
            spare = pltpu.make_async_copy(obuf.at[s], out_hbm.at[pl.ds(n_tok + s * mt, mt)], ssem.at[s])
            spare.start()
            spare.wait()

    @pl.when(i + 1 < n_used)
    def _():
        gather_start(gnxt_ref, 1 - slot)

    @pl.when(jnp.logical_and(i >= 2, i - 2 < n_used))
    def _():
        scatter_wait(slot)

    @pl.when(i < n_used)
    def _():
        gather_wait(slot)
        x = xbuf[slot].astype(BF16)
        wt_scr[...] = jnp.zeros_like(wt_scr)
        wt_scr[0:8, :] = wt_ref[...]
        wt = wt_scr[...].T
        acc = None
        for col, (wg, wu, wd) in enumerate(((wg_lo, wu_lo, wd_lo), (wg_hi, wu_hi, wd_hi))):
            gate = jnp.dot(x, wg[...], preferred_element_type=F32)
            up = jnp.dot(x, wu[...], preferred_element_type=F32)
            act = (_silu(gate) * up).astype(BF16)
            y = wt[:, col:col + 1] * jnp.dot(act, wd[...], preferred_element_type=F32)
            acc = y if acc is None else acc + y
        obuf[slot] = acc
        scatter_start(slot)

    @pl.when(i == last)
    def _():
        @pl.when(jnp.logical_and(i >= 1, i - 1 < n_used))
        def _():
            scatter_wait(1 - slot)

        @pl.when(i < n_used)
        def _():
            scatter_wait(slot)


def _moe_plan(info, counts, n_tok):
    mt = MOE_TM
    n_tiles = n_tok // mt + N_CLASSES
    n_pad = n_tiles * mt
    cls = info[:, :, 0, :].reshape(-1).astype(jnp.int32)
    rank = info[:, :, 1, :].reshape(-1).astype(jnp.int32)
    w_lo = info[:, :, 2, :].reshape(-1)
    w_hi = info[:, :, 3, :].reshape(-1)
    cnt = counts[:N_CLASSES, 0].astype(jnp.int32)
    padded = ((cnt + mt - 1) // mt) * mt
    ends = jnp.cumsum(padded)
    dest = (ends - padded)[cls] + rank
    tok = jnp.arange(n_tok, dtype=jnp.int32)
    pos = jnp.arange(n_pad, dtype=jnp.int32)
    gsrc = jnp.zeros((n_pad,), jnp.int32).at[dest].set(tok)
    sdst = (n_tok + ((pos // mt) % 2) * mt + pos % mt).at[dest].set(tok)
    wl = jnp.zeros((n_pad,), F32).at[dest].set(w_lo)
    wh = jnp.zeros((n_pad,), F32).at[dest].set(w_hi)
    wts = jnp.zeros((n_tiles, 8, mt), F32).at[:, 0].set(wl.reshape(n_tiles, mt)).at[:, 1].set(
        wh.reshape(n_tiles, mt))
    n_used = ends[-1] // mt
    tidx = jnp.arange(n_tiles, dtype=jnp.int32)
    tcls = jnp.minimum(jnp.searchsorted(ends, tidx * mt, side="right"), N_CLASSES - 1).astype(jnp.int32)
    tcls = jnp.where(tidx < n_used, tcls, tcls[jnp.maximum(n_used - 1, 0)])
    lo_tab = jnp.array([a for a in range(PER_GROUP) for _ in range(a + 1, PER_GROUP)], jnp.int32)
    hi_tab = jnp.array([b for a in range(PER_GROUP) for b in range(a + 1, PER_GROUP)], jnp.int32)
    elo = (tcls // N_PAIRS) * PER_GROUP + lo_tab[tcls % N_PAIRS]
    ehi = (tcls // N_PAIRS) * PER_GROUP + hi_tab[tcls % N_PAIRS]
    return (elo, ehi, n_used.reshape(1).astype(jnp.int32), gsrc.reshape(n_tiles, 1, mt),
            sdst.reshape(n_tiles, 1, mt), wts)


def _moe(h, info, counts, wg, wu, wd):
    n, d = h.shape
    ne, _, de = wg.shape
    mt = MOE_TM
    elo, ehi, n_used, gsrc, sdst, wts = _moe_plan(info, counts, n)
    n_tiles = gsrc.shape[0]

    def smem_rows(offset):
        return pl.BlockSpec((None, 1, mt), lambda i, *_: (jnp.minimum(i + offset, n_tiles - 1), 0, 0),
                            memory_space=pltpu.SMEM)

    def expert(which, shape):
        if which == 0:
            return pl.BlockSpec((None,) + shape, lambda i, lo, hi, nu: (lo[i], 0, 0))
        return pl.BlockSpec((None,) + shape, lambda i, lo, hi, nu: (hi[i], 0, 0))

    grid_spec = pltpu.PrefetchScalarGridSpec(
        num_scalar_prefetch=3,
        grid=(n_tiles,),
        in_specs=[
            smem_rows(0), smem_rows(1), smem_rows(0),
            pl.BlockSpec((None, 8, mt), lambda i, *_: (i, 0, 0)),
            pl.BlockSpec(memory_space=pl.ANY),
            expert(0, (d, de)), expert(0, (d, de)), expert(0, (de, d)),
            expert(1, (d, de)), expert(1, (d, de)), expert(1, (de, d)),
        ],
        out_specs=pl.BlockSpec(memory_space=pl.ANY),
        scratch_shapes=[
            pltpu.VMEM((2, mt, d), F32), pltpu.VMEM((2, mt, d), F32), pltpu.VMEM((LANES, mt), F32),
            pltpu.SemaphoreType.DMA((2,)), pltpu.SemaphoreType.DMA((2,)),
        ],
    )
    return pl.pallas_call(
        _moe_kernel,
        grid_spec=grid_spec,
        out_shape=jax.ShapeDtypeStruct((n + 2 * mt, d), F32),
        compiler_params=_cparams(1),
        name="moe",
    )(elo, ehi, n_used, gsrc, gsrc, sdst, wts, h, wg, wu, wd, wg, wu, wd)


def _ln2_kernel(x_ref, f_ref, mod_ref, lng_ref, lnb_ref, o_ref):
    mod = mod_ref[...]
    v = DEEPNORM_ALPHA * x_ref[...] + mod[5:6] * f_ref[...].astype(F32)
    o_ref[...] = _layer_norm(v, lng_ref[...], lnb_ref[...])


def _ln2(x, f, modarr, ln_g, ln_b, kind0):
    bsz, n, d = x.shape
    nj = n // TM
    return pl.pallas_call(
        _ln2_kernel,
        grid=(bsz, nj),
        in_specs=[
            pl.BlockSpec((None, TM, d), lambda b, j: (b, j, 0)),
            pl.BlockSpec((TM, d), lambda b, j: (b * nj + j, 0)),
            pl.BlockSpec((None, None, 6, d), lambda b, j: (b, jnp.minimum(j + kind0, 1), 0, 0)),
            pl.BlockSpec((1, d), lambda b, j: (0, 0)),
            pl.BlockSpec((1, d), lambda b, j: (0, 0)),
        ],
        out_specs=pl.BlockSpec((None, TM, d), lambda b, j: (b, j, 0)),
        out_shape=jax.ShapeDtypeStruct((bsz, n, d), F32),
        compiler_params=_cparams(2),
        name="ln2",
    )(x, f, modarr, ln_g, ln_b)


def _rope(x, c, s1, s2, shift):
    w = x.shape[1]
    return x * c + pltpu.roll(x, w - shift, 1) * s1 + pltpu.roll(x, shift, 1) * s2


def _rms(x, g):
    return x * lax.rsqrt(jnp.mean(x * x, axis=-1, keepdims=True) + RMS_EPS) * g


def _inproj_odd_kernel(x_ref, mod_ref, w_ref, qn_ref, kvn_ref, wuq_ref, wk_ref, we_ref, wv_ref,
                       tw_ref, tq_ref, tk_ref, qw_ref, kw_ref, vw_ref, qm_ref, km_ref, vm_ref):
    j = pl.program_id(1)
    is_ctx = j == 0
    mod = mod_ref[...]
    h = (x_ref[...] * (1.0 + mod[1:2]) + mod[0:1]).astype(BF16)
    p = jnp.dot(h, w_ref[...], preferred_element_type=F32)

    def tables(t_ref):
        c = jnp.where(is_ctx, 1.0, t_ref[0])
        s1 = jnp.where(is_ctx, 0.0, t_ref[1])
        s2 = jnp.where(is_ctx, 0.0, t_ref[2])
        return c, s1, s2

    cw, s1w, s2w = tables(tw_ref)
    nq = C_Q_HEADS * C_HEAD_DIM
    for r in range(nq // LANES):
        blk = _rope(p[:, r * LANES:(r + 1) * LANES], cw, s1w, s2w, C_HEAD_DIM // 2)
        qw_ref[:, r * LANES:(r + 1) * LANES] = (blk * (C_HEAD_DIM ** -0.5)).astype(BF16)
    kw_ref[...] = _rope(p[:, nq:nq + LANES], cw, s1w, s2w, C_HEAD_DIM // 2).astype(BF16)
    vw_ref[...] = p[:, nq + LANES:nq + 2 * LANES].astype(BF16)

    o = nq + 2 * LANES
    dq = _rms(p[:, o:o + D_Q_RANK], qn_ref[...]).astype(BF16)
    o += D_Q_RANK
    dkv = _rms(p[:, o:o + D_KV_RANK], kvn_ref[...]).astype(BF16)
    o += D_KV_RANK
    cq, s1q, s2q = tables(tq_ref)
    ck, s1k, s2k = tables(tk_ref)
    krope = _rope(p[:, o:o + LANES], ck, s1k, s2k, D_ROPE // 2).astype(BF16)
    scale = (D_NOPE + D_ROPE) ** -0.5
    for hh in range(D_HEADS):
        sl = slice(hh * LANES, (hh + 1) * LANES)
        qh = jnp.dot(dq, wuq_ref[:, sl], preferred_element_type=F32)
        qm_ref[:, sl] = (_rope(qh, cq, s1q, s2q, D_ROPE // 2) * scale).astype(BF16)
        kh = (jnp.dot(dkv, wk_ref[:, sl], preferred_element_type=F32)
              + jnp.dot(krope, we_ref[:, sl], preferred_element_type=F32))
        km_ref[:, sl] = kh.astype(BF16)
        vm_ref[:, sl] = jnp.dot(dkv, wv_ref[:, sl], preferred_element_type=F32).astype(BF16)


def _inproj_odd(xin, modarr, w1, qnorm, kvnorm, wuq, wk, we, wv, tab_w, tab_q, tab_k):
    bsz, t, d = xin.shape
    n1 = w1.shape[1]
    hw = D_HEADS * LANES

    def tab_spec():
        return pl.BlockSpec((3, TM, LANES), lambda b, j: (0, jnp.maximum(j - 1, 0), 0))

    def full(a):
        return pl.BlockSpec(a.shape, lambda b, j: (0,) * a.ndim)

    def out(width):
        return pl.BlockSpec((None, TM, width), lambda b, j: (b, j, 0))

    widths = (C_Q_HEADS * C_HEAD_DIM, LANES, LANES, hw, hw, hw)
    return pl.pallas_call(
        _inproj_odd_kernel,
        grid=(bsz, t // TM),
        in_specs=[
            pl.BlockSpec((None, TM, d), lambda b, j: (b, j, 0)),
            pl.BlockSpec((None, None, 6, d), lambda b, j: (b, jnp.minimum(j, 1), 0, 0)),
            full(w1), full(qnorm), full(kvnorm), full(wuq), full(wk), full(we), full(wv),
            tab_spec(), tab_spec(), tab_spec(),
        ],
        out_specs=[out(w) for w in widths],
        out_shape=[jax.ShapeDtypeStruct((bsz, t, w), BF16) for w in widths],
        compiler_params=_cparams(2),
        name="inproj_odd",
    )(xin, modarr, w1, qnorm, kvnorm, wuq, wk, we, wv, tab_w, tab_q, tab_k)


def _win_kernel(sink_ref, q_ref, k_ref, v_ref, o_ref, klo_scr, khi_scr, *, n_ctx):
    i = pl.program_id(1)
    wdw = C_WINDOW
    t = k_ref.shape[0]
    lane = lax.broadcasted_iota(jnp.int32, (t, LANES), 1)

    @pl.when(i == 0)
    def _():
        kk = k_ref[...]
        klo_scr[...] = jnp.where(lane < C_HEAD_DIM, kk, jnp.zeros_like(kk))
        khi_scr[...] = jnp.where(lane >= C_HEAD_DIM, kk, jnp.zeros_like(kk))

    n_lat_blk = (t - n_ctx) // wdw
    blk0 = jnp.clip(i - 1, 0, n_lat_blk - 3)
    r0 = pl.multiple_of(n_ctx + blk0 * wdw, wdw)
    kpos = blk0 * wdw + lax.broadcasted_iota(jnp.int32, (wdw, 3 * wdw), 1)
    qpos = i * wdw + lax.broadcasted_iota(jnp.int32, (wdw, 3 * wdw), 0)
    near = jnp.abs(kpos - qpos) <= wdw
    v_loc = v_ref[pl.ds(r0, 3 * wdw), :]
    v_ctx = v_ref[0:n_ctx, :]
    olane = lax.broadcasted_iota(jnp.int32, (wdw, LANES), 1)
    n_rep = C_Q_HEADS // C_KV_HEADS
    for r in range(n_rep):
        q = q_ref[:, r * LANES:(r + 1) * LANES]
        outs = []
        for g, k_scr in enumerate((klo_scr, khi_scr)):
            k_loc = k_scr[pl.ds(r0, 3 * wdw), :]
            k_ctx = k_scr[0:n_ctx, :]
            s_loc = lax.dot_general(q, k_loc, (((1,), (1,)), ((), ())), preferred_element_type=F32)
            s_loc = jnp.where(near, s_loc, -jnp.inf)
            s_ctx = lax.dot_general(q, k_ctx, (((1,), (1,)), ((), ())), preferred_element_type=F32)
            sink = sink_ref[g * n_rep + r]
            m = jnp.maximum(jnp.maximum(jnp.max(s_loc, axis=-1, keepdims=True),
                                        jnp.max(s_ctx, axis=-1, keepdims=True)), sink)
            p_loc = jnp.exp(s_loc - m)
            p_ctx = jnp.exp(s_ctx - m)
            den = (jnp.sum(p_loc, axis=-1, keepdims=True) + jnp.sum(p_ctx, axis=-1, keepdims=True)
                   + jnp.exp(sink - m))
            pv = (jnp.dot(p_loc.astype(BF16), v_loc, preferred_element_type=F32)
                  + jnp.dot(p_ctx.astype(BF16), v_ctx, preferred_element_type=F32))
            outs.append(pv / den)
        o_ref[:, r * LANES:(r + 1) * LANES] = jnp.where(olane < C_HEAD_DIM, outs[0], outs[1]).astype(BF16)


def _win_attention(sink, qw, kw, vw, n_ctx):
    bsz, t, nq = qw.shape
    n_lat = t - n_ctx
    ctx_blk = n_ctx // C_WINDOW
    grid_spec = pltpu.PrefetchScalarGridSpec(
        num_scalar_prefetch=1,
        grid=(bsz, n_lat // C_WINDOW),
        in_specs=[
            pl.BlockSpec((None, C_WINDOW, nq), lambda b, i, s: (b, i + ctx_blk, 0)),
            pl.BlockSpec((None, t, LANES), lambda b, i, s: (b, 0, 0)),
            pl.BlockSpec((None, t, LANES), lambda b, i, s: (b, 0, 0)),
        ],
        out_specs=pl.BlockSpec((None, C_WINDOW, nq), lambda b, i, s: (b, i, 0)),
        scratch_shapes=[pltpu.VMEM((t, LANES), BF16), pltpu.VMEM((t, LANES), BF16)],
    )
    return pl.pallas_call(
        functools.partial(_win_kernel, n_ctx=n_ctx),
        grid_spec=grid_spec,
        out_shape=jax.ShapeDtypeStruct((bsz, n_lat, nq), BF16),
        compiler_params=_cparams(2),
        name="win_attention",
    )(sink, qw, kw, vw)


MLA_TQ = 256


def _mla_kernel(q_ref, k_ref, v_ref, o_ref):
    for hp in range(D_HEADS // 2):
        acc = None
        for hh in (2 * hp, 2 * hp + 1):
            sl = slice(hh * LANES, (hh + 1) * LANES)
            s = lax.dot_general(q_ref[:, sl], k_ref[:, sl], (((1,), (1,)), ((), ())),
                                preferred_element_type=F32)
            m = jnp.max(s, axis=-1, keepdims=True)
            p = jnp.exp(s - m)
            den = jnp.sum(p, axis=-1, keepdims=True)
            pv = jnp.dot(p.astype(BF16), v_ref[:, sl], preferred_element_type=F32) / den
            acc = pv if acc is None else acc + pv
        o_ref[:, hp * LANES:(hp + 1) * LANES] = acc.astype(BF16)


def _mla_attention(qm, km, vm, n_ctx):
    bsz, t, hw = qm.shape
    n_lat = t - n_ctx
    ctx_blk = n_ctx // MLA_TQ
    ow = D_HEADS * D_V
    return pl.pallas_call(
        _mla_kernel,
        grid=(bsz, n_lat // MLA_TQ),
        in_specs=[
            pl.BlockSpec((None, MLA_TQ, hw), lambda b, i: (b, i + ctx_blk, 0)),
            pl.BlockSpec((None, t, hw), lambda b, i: (b, 0, 0)),
            pl.BlockSpec((None, t, hw), lambda b, i: (b, 0, 0)),
        ],
        out_specs=pl.BlockSpec((None, MLA_TQ, ow), lambda b, i: (b, i, 0)),
        out_shape=jax.ShapeDtypeStruct((bsz, n_lat, ow), BF16),
        compiler_params=_cparams(2),
        name="mla_attention",
    )(qm, km, vm)


def _rope_tables(n_tokens, rot_dim, group, offset):
    t = jnp.arange(n_tokens)
    rows = (t // GRID_W).astype(F32)
    cols = (t % GRID_W).astype(F32)
    n_freq = rot_dim // 4
    inv_freq = ROPE_BASE ** (-jnp.arange(n_freq, dtype=F32) / n_freq)
    ang = jnp.concatenate([rows[:, None] * inv_freq, cols[:, None] * inv_freq], -1)
    cos, sin = jnp.cos(ang), jnp.sin(ang)
    half = rot_dim // 2
    c = jnp.ones((n_tokens, LANES), F32)
    s1 = jnp.zeros((n_tokens, LANES), F32)
    s2 = jnp.zeros((n_tokens, LANES), F32)
    for start in range(offset, LANES, group):
        c = c.at[:, start:start + half].set(cos).at[:, start + half:start + rot_dim].set(cos)
        s1 = s1.at[:, start:start + half].set(-sin)
        s2 = s2.at[:, start + half:start + rot_dim].set(sin)
    return jnp.stack([c, s1, s2])


def _odd_weights(w_in, wuq, wukv, w_out):
    d = w_in.shape[0]
    nq = C_Q_HEADS * C_HEAD_DIM
    nkv = C_KV_HEADS * C_HEAD_DIM
    n_rep = C_Q_HEADS // C_KV_HEADS
    order = [g * n_rep + r for r in range(n_rep) for g in range(C_KV_HEADS)]
    cq = w_in[:, :nq].reshape(d, C_Q_HEADS, C_HEAD_DIM)[:, order].reshape(d, nq)
    rest = w_in[:, nq:nq + 2 * nkv + D_Q_RANK + D_KV_RANK]
    krope = jnp.pad(w_in[:, nq + 2 * nkv + D_Q_RANK + D_KV_RANK:], ((0, 0), (0, LANES - D_ROPE)))
    w1 = jnp.concatenate([cq, rest, krope], axis=1).astype(BF16)
    qh = wuq.reshape(D_Q_RANK, D_HEADS, D_NOPE + D_ROPE)
    wuq_p = jnp.pad(qh, ((0, 0), (0, 0), (0, LANES - D_NOPE - D_ROPE))).reshape(D_Q_RANK, D_HEADS * LANES)
    kvh = wukv.reshape(D_KV_RANK, D_HEADS, D_NOPE + D_V)
    wk_p = jnp.pad(kvh[:, :, :D_NOPE], ((0, 0), (0, 0), (0, LANES - D_NOPE))).reshape(D_KV_RANK, D_HEADS * LANES)
    e_blk = jnp.zeros((LANES, LANES), F32).at[jnp.arange(D_ROPE), D_NOPE + jnp.arange(D_ROPE)].set(1.0)
    we = jnp.tile(e_blk, (1, D_HEADS))
    vh = kvh[:, :, D_NOPE:]
    even = (jnp.arange(D_HEADS) % 2 == 0)[None, :, None]
    wv_p = jnp.where(even, jnp.pad(vh, ((0, 0), (0, 0), (0, D_V))),
                     jnp.pad(vh, ((0, 0), (0, 0), (D_V, 0)))).reshape(D_KV_RANK, D_HEADS * LANES)
    wo_win = w_out[:nq].reshape(C_Q_HEADS, C_HEAD_DIM, -1)[jnp.array(order)].reshape(nq, -1)
    wo = jnp.concatenate([wo_win, w_out[nq:]], axis=0).astype(BF16)
    return w1, wuq_p.astype(BF16), wk_p.astype(BF16), we.astype(BF16), wv_p.astype(BF16), wo


def kernel(x, c, ctx, c_ctx, ada_w, ada_b, ln_g, ln_b, ev_w_in, ev_a_conv, ev_b_conv, ev_b_alog, ev_b_dtbias, ev_b_norm, ev_w_out, od_w_in, od_c_sink, od_d_qnorm, od_d_kvnorm, od_d_wuq, od_d_wukv, od_w_out, router_w, router_bias, moe_w_gate, moe_w_up, moe_w_down):
    bsz, n_lat, d = x.shape
    n_ctx = ctx.shape[1]
    assert n_ctx == TM and n_lat % TM == 0 and n_lat % GRID_W == 0
    assert ada_w.shape[0] == DEPTH and bsz + 1 <= 40
    t = n_ctx + n_lat

    cs = jnp.zeros((40, d), F32).at[:bsz].set(c).at[bsz].set(c_ctx)
    mods = _ada_mod(cs, ada_w, ada_b)

    def modarr(layer):
        m = mods[layer].reshape(40, 6, d)
        return jnp.stack([jnp.broadcast_to(m[bsz], (bsz, 6, d)), m[:bsz]], axis=1)

    rwt = router_w.T
    rbias = router_bias.reshape(N_EXPERTS, 1)
    xcat = jnp.concatenate([ctx, x], axis=1)

    mod0 = modarr(0)
    n_main = 3 * A_WIDTH + 4 * B_WIDTH
    w_main = ev_w_in[0][:, :n_main].astype(BF16)
    w_small = jnp.pad(ev_w_in[0][:, n_main:], ((0, 0), (0, LANES - 4 * B_HEADS))).astype(BF16)
    p, small = _inproj_even(xcat, mod0, w_main, w_small)
    alog_pad = jnp.zeros((1, LANES), F32).at[0, 8:16].set(ev_b_alog[0].reshape(-1))
    dtb_pad = jnp.zeros((1, LANES), F32).at[0, 8:16].set(ev_b_dtbias[0].reshape(-1))
    ya, u, w, qe, ket, att, dec = _even_prep(p, small, ev_a_conv[0], ev_b_conv[0], alog_pad, dtb_pad,
                                             GDN_CHUNK)
    o_fwd, o_rev = _gdn_rec(u, w, qe, ket, att, dec, GDN_CHUNK)
    x1, h1, info0, cnt0 = _outproj(ya, None, xcat, mod0, ev_w_out[0].astype(BF16),
                             ln_g[0, 0].reshape(1, d), ln_b[0, 0].reshape(1, d), rwt, rbias, 0,
                             gdn=(o_fwd, o_rev, p, ev_b_norm[0].reshape(1, B_HEAD_DIM)))
    f = _moe(h1.reshape(bsz * t, d), info0, cnt0,
             moe_w_gate[0].astype(BF16), moe_w_up[0].astype(BF16), moe_w_down[0].astype(BF16))
    x2 = _ln2(x1, f, mod0, ln_g[0, 1].reshape(1, d), ln_b[0, 1].reshape(1, d), 0)

    mod1 = modarr(1)
    w1, wuq_p, wk_p, we, wv_p, wo = _odd_weights(od_w_in[0], od_d_wuq[0], od_d_wukv[0], od_w_out[0])
    tab_w = _rope_tables(n_lat, C_HEAD_DIM, C_HEAD_DIM, 0)
    tab_q = _rope_tables(n_lat, D_ROPE, LANES, D_NOPE)
    tab_k = _rope_tables(n_lat, D_ROPE, LANES, 0)
    qw, kw, vw, qm, km, vm = _inproj_odd(
        x2, mod1, w1, od_d_qnorm[0].reshape(1, -1), od_d_kvnorm[0].reshape(1, -1),
        wuq_p, wk_p, we, wv_p, tab_w, tab_q, tab_k)
    y_win = _win_attention(od_c_sink[0], qw, kw, vw, n_ctx)
    y_mla = _mla_attention(qm, km, vm, n_ctx)
    x3, h3, info1, cnt1 = _outproj(y_win, y_mla, x2, mod1, wo, ln_g[1, 0].reshape(1, d),
                                   ln_b[1, 0].reshape(1, d), rwt, rbias, n_ctx // TM)
    f1 = _moe(h3.reshape(bsz * n_lat, d), info1, cnt1,
              moe_w_gate[1].astype(BF16), moe_w_up[1].astype(BF16), moe_w_down[1].astype(BF16))
    return _ln2(x3, f1, mod1, ln_g[1, 1].reshape(1, d), ln_b[1, 1].reshape(1, d), 1)
```

```python
import functools
import math

import numpy as np
import jax
import jax.numpy as jnp
from jax import lax
from jax.experimental import pallas as pl
from jax.experimental.pallas import tpu as pltpu

F32 = jnp.float32
BF16 = jnp.bfloat16
HIGHEST = lax.Precision.HIGHEST

DEPTH = 2
GRID_W = 64
DEEPNORM_ALPHA = (2.0 * DEPTH) ** 0.25
LN_EPS = 1e-5
RMS_EPS = 1e-6
ROPE_BASE = 10000.0
B_HEADS = 4
B_HEAD_DIM = 128
B_WIDTH = 512
A_WIDTH = 512
C_Q_HEADS = 8
C_KV_HEADS = 2
C_HEAD_DIM = 64
C_WINDOW = 128
D_HEADS = 8
D_NOPE = 64
D_ROPE = 32
D_V = 64
D_Q_RANK = 384
D_KV_RANK = 256
N_EXPERTS = 16
N_GROUPS = 4
PER_GROUP = N_EXPERTS // N_GROUPS
D_EXPERT = 512
N_PAIRS = PER_GROUP * (PER_GROUP - 1) // 2
N_CLASSES = N_GROUPS * N_PAIRS
CLS_PAD = 32

LANES = 128
TM = 256
GDN_CHUNK = 64
MOE_TM = 256
VMEM_LIMIT = 56 * 1024 * 1024


def _cparams(n_axes, vmem=VMEM_LIMIT):
    return pltpu.CompilerParams(dimension_semantics=("arbitrary",) * n_axes, vmem_limit_bytes=vmem)


def _sigmoid(x):
    return 1.0 / (1.0 + jnp.exp(-x))


def _silu(x):
    return x * _sigmoid(x)


def _softplus(x):
    return jnp.maximum(x, 0.0) + jnp.log(1.0 + jnp.exp(-jnp.abs(x)))


def _layer_norm(v, g, b):
    mu = jnp.mean(v, axis=-1, keepdims=True)
    d = v - mu
    var = jnp.mean(d * d, axis=-1, keepdims=True)
    return d * lax.rsqrt(var + LN_EPS) * g + b


def _ada_kernel(c_ref, w_ref, b_ref, o_ref):
    s = _silu(c_ref[...])
    o_ref[...] = jnp.dot(s, w_ref[...], precision=HIGHEST, preferred_element_type=F32) + b_ref[...]


def _ada_mod(cs, ada_w, ada_b):
    depth, d, n6 = ada_w.shape
    rows = cs.shape[0]
    tn = 1536
    return pl.pallas_call(
        _ada_kernel,
        grid=(depth, n6 // tn),
        in_specs=[
            pl.BlockSpec((rows, d), lambda l, n: (0, 0)),
            pl.BlockSpec((None, d, tn), lambda l, n: (l, 0, n)),
            pl.BlockSpec((None, 1, tn), lambda l, n: (l, 0, n)),
        ],
        out_specs=pl.BlockSpec((None, rows, tn), lambda l, n: (l, 0, n)),
        out_shape=jax.ShapeDtypeStruct((depth, rows, n6), F32),
        compiler_params=_cparams(2),
        name="ada_mod",
    )(cs, ada_w, ada_b.reshape(depth, 1, n6))


def _inproj_even_kernel(c_ref, x_ref, mod_ref, wm_ref, ws_ref, p_ref, s_ref):
    mod = mod_ref[...]
    xv = jnp.where(pl.program_id(1) == 0, c_ref[...], x_ref[...])
    h = (xv * (1.0 + mod[1:2]) + mod[0:1]).astype(BF16)
    p_ref[...] = jnp.dot(h, wm_ref[...], preferred_element_type=F32).astype(BF16)
    s_ref[...] = jnp.dot(h, ws_ref[...], preferred_element_type=F32)


def _inproj_even(ctx, x, modarr, w_main, w_small):
    bsz, n_lat, d = x.shape
    t = ctx.shape[1] + n_lat
    nm = w_main.shape[1]
    return pl.pallas_call(
        _inproj_even_kernel,
        grid=(bsz, t // TM),
        in_specs=[
            pl.BlockSpec((None, TM, d), lambda b, j: (b, 0, 0)),
            pl.BlockSpec((None, TM, d), lambda b, j: (b, jnp.maximum(j - 1, 0), 0)),
            pl.BlockSpec((None, None, 6, d), lambda b, j: (b, jnp.minimum(j, 1), 0, 0)),
            pl.BlockSpec((d, nm), lambda b, j: (0, 0)),
            pl.BlockSpec((d, LANES), lambda b, j: (0, 0)),
        ],
        out_specs=[
            pl.BlockSpec((None, TM, nm), lambda b, j: (b, j, 0)),
            pl.BlockSpec((None, TM, LANES), lambda b, j: (b, j, 0)),
        ],
        out_shape=[
            jax.ShapeDtypeStruct((bsz, t, nm), BF16),
            jax.ShapeDtypeStruct((bsz, t, LANES), F32),
        ],
        compiler_params=_cparams(2),
        name="inproj_even",
    )(ctx, x, modarr, w_main, w_small)


HALO = 16


def _conv3(z, zp, zn, w):
    n = z.shape[0]
    rows = lax.broadcasted_iota(jnp.int32, z.shape, 0)
    zprev = jnp.where(rows == 0, zp, pltpu.roll(z, 1, 0))
    znext = jnp.where(rows == n - 1, zn, pltpu.roll(z, n - 1, 0))
    return w[0:1] * zprev + w[1:2] * z + w[2:3] * znext


def _even_prep_kernel(p_ref, pp_ref, pn_ref, s_ref, aw_ref, bw_ref, alog_ref, dtb_ref,
                      ya_ref, u_ref, w_ref, qe_ref, ket_ref, att_ref, dec_ref,
                      q_scr, k_scr, v_scr, *, chunk):
    j = pl.program_id(1)
    nj = pl.num_programs(1)
    prev_on = jnp.where(jnp.logical_and(j != 0, j != 1), 1.0, 0.0)
    next_on = jnp.where(jnp.logical_and(j != 0, j != nj - 1), 1.0, 0.0)
    prow = pp_ref[...].astype(F32)[HALO - 1:HALO] * prev_on
    nrow = pn_ref[...].astype(F32)[0:1] * next_on

    def seg(lo, hi):
        return p_ref[:, lo:hi].astype(F32), prow[:, lo:hi], nrow[:, lo:hi]

    a0, _, _ = seg(0, A_WIDTH)
    a1, a1p, a1n = seg(A_WIDTH, 2 * A_WIDTH)
    a2, a2p, a2n = seg(2 * A_WIDTH, 3 * A_WIDTH)
    ya_ref[...] = (a0 * _conv3(a1 * a2, a1p * a2p, a1n * a2n, aw_ref[...])).astype(BF16)

    base = 3 * A_WIDTH
    for which in range(3):
        lo = base + which * B_WIDTH
        z, zp, zn = seg(lo, lo + B_WIDTH)
        c = _silu(_conv3(z, zp, zn, bw_ref[:, which * B_WIDTH:(which + 1) * B_WIDTH]))
        for h in range(B_HEADS):
            ch = c[:, h * B_HEAD_DIM:(h + 1) * B_HEAD_DIM]
            if which < 2:
                ss = jnp.sum(ch * ch, axis=-1, keepdims=True)
                ch = ch * lax.rsqrt(ss + 1e-6)
                if which == 0:
                    ch = ch * (B_HEAD_DIM ** -0.5)
            sl = slice(h * B_HEAD_DIM, (h + 1) * B_HEAD_DIM)
            if which == 0:
                q_scr[:, sl] = ch.astype(BF16)
            elif which == 1:
                k_scr[:, sl] = ch.astype(BF16)
            else:
                v_scr[:, sl] = ch

    s = s_ref[...]
    beta = _sigmoid(s)
    g = -jnp.exp(alog_ref[...]) * _softplus(s + dtb_ref[...])
    n = s.shape[0]
    nck = n // chunk
    ri = lax.broadcasted_iota(jnp.int32, (n, n), 0)
    ci = lax.broadcasted_iota(jnp.int32, (n, n), 1)
    same = (ri // chunk) == (ci // chunk)
    m_fwd = jnp.where(jnp.logical_and(same, ci <= ri), 1.0, 0.0)
    m_rev = jnp.where(jnp.logical_and(same, ci >= ri), 1.0, 0.0)
    m_all = jnp.where(same, 1.0, 0.0)
    gc_f = jnp.dot(m_fwd, g, precision=HIGHEST, preferred_element_type=F32)
    gc_r = jnp.dot(m_rev, g, precision=HIGHEST, preferred_element_type=F32)
    tot = jnp.dot(m_all, g, precision=HIGHEST, preferred_element_type=F32)
    lane = lax.broadcasted_iota(jnp.int32, s.shape, 1)
    gc = jnp.where(lane >= 8 + B_HEADS, gc_r, gc_f)
    e_gc = jnp.exp(gc)
    e_rest = jnp.exp(tot - gc)
    gct = gc.T
    e_tot = jnp.exp(tot)

    ri = lax.broadcasted_iota(jnp.int32, (1, chunk, chunk), 1)
    ci = lax.broadcasted_iota(jnp.int32, (1, chunk, chunk), 2)
    eye = jnp.where(ri == ci, 1.0, 0.0)
    n_sq = int(np.log2(chunk))
    nt_batched = (((2,), (2,)), ((0,), (0,)))
    nn_batched = (((2,), (1,)), ((0,), (0,)))
    kk, qk, kf, qf = [], [], [], []
    for h in range(B_HEADS):
        hs = slice(h * B_HEAD_DIM, (h + 1) * B_HEAD_DIM)
        k3 = k_scr[:, hs].reshape(nck, chunk, B_HEAD_DIM)
        q3 = q_scr[:, hs].reshape(nck, chunk, B_HEAD_DIM)
        kk.append(lax.dot_general(k3, k3, nt_batched, preferred_element_type=F32))
        qk.append(lax.dot_general(q3, k3, nt_batched, preferred_element_type=F32))
        kf.append(k_scr[:, hs].astype(F32))
        qf.append(q_scr[:, hs].astype(F32))
    a_all, rhs_all = [], []
    for d in range(2):
        incl = (ci <= ri) if d == 0 else (ci >= ri)
        strict = (ci < ri) if d == 0 else (ci > ri)
        for h in range(B_HEADS):
            chain = d * B_HEADS + h
            hs = slice(h * B_HEAD_DIM, (h + 1) * B_HEAD_DIM)
            bcol = beta[:, chain:chain + 1]
            e1 = e_gc[:, 8 + chain:9 + chain]
            e2 = e_rest[:, 8 + chain:9 + chain]
            gcol = gc[:, 8 + chain:9 + chain].reshape(nck, chunk, 1)
            grow = jnp.stack([gct[8 + chain:9 + chain, cc * chunk:(cc + 1) * chunk]
                              for cc in range(nck)], axis=0)
            decay = jnp.exp(jnp.where(incl, gcol - grow, -jnp.inf))
            a_all.append(jnp.where(strict, bcol.reshape(nck, chunk, 1) * kk[h] * decay, 0.0))
            att_ref[d, h] = (qk[h] * decay).reshape(n, chunk).astype(BF16)
            qe_ref[d, h] = (qf[h] * e1).astype(BF16)
            ket = (kf[h] * e2).T
            for cc in range(nck):
                ket_ref[d, h, cc] = ket[:, cc * chunk:(cc + 1) * chunk].astype(BF16)
                dec_ref[cc, chain] = jnp.broadcast_to(
                    e_tot[cc * chunk:cc * chunk + 1, 8 + chain:9 + chain], (1, LANES))
            rhs = jnp.concatenate([(v_scr[:, hs] * bcol).astype(BF16),
                                   (kf[h] * (bcol * e1)).astype(BF16)], axis=1)
            rhs_all.append(rhs.reshape(nck, chunk, 2 * B_HEAD_DIM))
    npow = -jnp.concatenate(a_all, axis=0)
    tinv = eye + npow
    for _ in range(n_sq - 1):
        nb = npow.astype(BF16)
        npow = lax.dot_general(nb, nb, nn_batched, preferred_element_type=F32)
        tinv = tinv + lax.dot_general(tinv.astype(BF16), npow.astype(BF16), nn_batched,
                                      preferred_element_type=F32)
    uw = lax.dot_general(tinv.astype(BF16), jnp.concatenate(rhs_all, axis=0), nn_batched,
                         preferred_element_type=F32)
    for d in range(2):
        for h in range(B_HEADS):
            blk = uw[(d * B_HEADS + h) * nck:(d * B_HEADS + h + 1) * nck]
            u_ref[d, h] = blk[:, :, :B_HEAD_DIM].reshape(n, B_HEAD_DIM).astype(BF16)
            w_ref[d, h] = blk[:, :, B_HEAD_DIM:].reshape(n, B_HEAD_DIM).astype(BF16)


def _even_prep(p, small, a_conv, b_conv, alog_pad, dtb_pad, chunk):
    bsz, t, nm = p.shape
    nhb = TM // HALO
    last_hb = t // HALO - 1
    nck = TM // chunk
    nc = t // chunk

    def per_dir():
        return (pl.BlockSpec((None, 2, B_HEADS, TM, B_HEAD_DIM), lambda b, j: (b, 0, 0, j, 0)),
                jax.ShapeDtypeStruct((bsz, 2, B_HEADS, t, B_HEAD_DIM), BF16))

    outs = [
        (pl.BlockSpec((None, TM, A_WIDTH), lambda b, j: (b, j, 0)),
         jax.ShapeDtypeStruct((bsz, t, A_WIDTH), BF16)),
        per_dir(), per_dir(), per_dir(),
        (pl.BlockSpec((None, 2, B_HEADS, nck, B_HEAD_DIM, chunk), lambda b, j: (b, 0, 0, j, 0, 0)),
         jax.ShapeDtypeStruct((bsz, 2, B_HEADS, nc, B_HEAD_DIM, chunk), BF16)),
        (pl.BlockSpec((None, 2, B_HEADS, TM, chunk), lambda b, j: (b, 0, 0, j, 0)),
         jax.ShapeDtypeStruct((bsz, 2, B_HEADS, t, chunk), BF16)),
        (pl.BlockSpec((None, nck, 2 * B_HEADS, 1, LANES), lambda b, j: (b, j, 0, 0, 0)),
         jax.ShapeDtypeStruct((bsz, nc, 2 * B_HEADS, 1, LANES), F32)),
    ]
    return pl.pallas_call(
        functools.partial(_even_prep_kernel, chunk=chunk),
        grid=(bsz, t // TM),
        in_specs=[
            pl.BlockSpec((None, TM, nm), lambda b, j: (b, j, 0)),
            pl.BlockSpec((None, HALO, nm), lambda b, j: (b, jnp.maximum(j * nhb - 1, 0), 0)),
            pl.BlockSpec((None, HALO, nm), lambda b, j: (b, jnp.minimum((j + 1) * nhb, last_hb), 0)),
            pl.BlockSpec((None, TM, LANES), lambda b, j: (b, j, 0)),
            pl.BlockSpec((3, A_WIDTH), lambda b, j: (0, 0)),
            pl.BlockSpec((3, 3 * B_WIDTH), lambda b, j: (0, 0)),
            pl.BlockSpec((1, LANES), lambda b, j: (0, 0)),
            pl.BlockSpec((1, LANES), lambda b, j: (0, 0)),
        ],
        out_specs=[o[0] for o in outs],
        out_shape=[o[1] for o in outs],
        scratch_shapes=[pltpu.VMEM((TM, B_WIDTH), BF16), pltpu.VMEM((TM, B_WIDTH), BF16),
                        pltpu.VMEM((TM, B_WIDTH), F32)],
        compiler_params=_cparams(2),
        name="even_prep",
    )(p, p, p, small, a_conv, b_conv, alog_pad, dtb_pad)


def _gdn_rec_kernel(uf, wf, qf, kf, af, df, ur, wr, qr, kr, ar, dr, of_ref, or_ref, s_scr, *, chunk):
    @pl.when(pl.program_id(1) == 0)
    def _():
        s_scr[...] = jnp.zeros_like(s_scr)

    nck = uf.shape[1] // chunk
    nn_batched = (((2,), (1,)), ((0,), (0,)))

    def both(fwd, rev):
        return jnp.concatenate([fwd, rev], axis=0)

    for cc in range(nck):
        cr = nck - 1 - cc
        rf = slice(cc * chunk, (cc + 1) * chunk)
        rr = slice(cr * chunk, (cr + 1) * chunk)
        s = s_scr[...]
        sb = s.astype(BF16)
        u = both(uf[:, rf, :], ur[:, rr, :]).astype(F32)
        v_new = u - lax.dot_general(both(wf[:, rf, :], wr[:, rr, :]), sb, nn_batched,
                                    preferred_element_type=F32)
        vb = v_new.astype(BF16)
        o = (lax.dot_general(both(qf[:, rf, :], qr[:, rr, :]), sb, nn_batched,
                             preferred_element_type=F32)
             + lax.dot_general(both(af[:, rf, :], ar[:, rr, :]), vb, nn_batched,
                               preferred_element_type=F32))
        dec = both(df[cc, 0:B_HEADS], dr[cr, B_HEADS:2 * B_HEADS])
        s_scr[...] = s * dec + lax.dot_general(both(kf[:, cc], kr[:, cr]), vb, nn_batched,
                                               preferred_element_type=F32)
        for h in range(B_HEADS):
            hs = slice(h * B_HEAD_DIM, (h + 1) * B_HEAD_DIM)
            of_ref[rf, hs] = o[h].astype(BF16)
            or_ref[rr, hs] = o[B_HEADS + h].astype(BF16)


def _gdn_rec(u, w, qe, ket, att, dec, chunk):
    bsz, _, _, t, _ = u.shape
    nt = t // TM
    nck = TM // chunk

    def tile(d, s):
        return s if d == 0 else jnp.where(s == 0, 0, nt - s)

    in_specs = []
    for d in range(2):
        for _ in range(3):
            in_specs.append(pl.BlockSpec((None, None, B_HEADS, TM, B_HEAD_DIM),
                                         lambda b, s, d=d: (b, d, 0, tile(d, s), 0)))
        in_specs.append(pl.BlockSpec((None, None, B_HEADS, nck, B_HEAD_DIM, chunk),
                                     lambda b, s, d=d: (b, d, 0, tile(d, s), 0, 0)))
        in_specs.append(pl.BlockSpec((None, None, B_HEADS, TM, chunk),
                                     lambda b, s, d=d: (b, d, 0, tile(d, s), 0)))
        in_specs.append(pl.BlockSpec((None, nck, 2 * B_HEADS, 1, LANES),
                                     lambda b, s, d=d: (b, tile(d, s), 0, 0, 0)))
    return pl.pallas_call(
        functools.partial(_gdn_rec_kernel, chunk=chunk),
        grid=(bsz, nt),
        in_specs=in_specs,
        out_specs=[pl.BlockSpec((None, TM, B_WIDTH), lambda b, s, d=d: (b, tile(d, s), 0))
                   for d in range(2)],
        out_shape=[jax.ShapeDtypeStruct((bsz, t, B_WIDTH), BF16)] * 2,
        scratch_shapes=[pltpu.VMEM((2 * B_HEADS, B_HEAD_DIM, B_HEAD_DIM), F32)],
        compiler_params=_cparams(2),
        name="gdn_rec",
    )(u, w, qe, ket, att, dec, u, w, qe, ket, att, dec)


def _route(logits_t, bias, sel_scr):
    aff = _sigmoid(logits_t)
    sel_scr[...] = aff + bias
    n = logits_t.shape[1]
    sel = [sel_scr[e:e + 1, :] for e in range(N_EXPERTS)]
    affr = [aff[e:e + 1, :] for e in range(N_EXPERTS)]
    in_top2 = []
    gscore = []
    for g in range(N_GROUPS):
        ids = range(g * PER_GROUP, (g + 1) * PER_GROUP)
        gs = jnp.zeros((1, n), F32)
        for e in ids:
            rank = jnp.zeros((1, n), F32)
            for e2 in ids:
                if e2 == e:
                    continue
                ahead = (sel[e2] >= sel[e]) if e2 < e else (sel[e2] > sel[e])
                rank = rank + jnp.where(ahead, 1.0, 0.0)
            top = rank < 2.0
            in_top2.append(top)
            gs = gs + jnp.where(top, sel[e], 0.0)
        gscore.append(gs)
    best = jnp.zeros((1, n), jnp.int32)
    bestv = gscore[0]
    for g in range(1, N_GROUPS):
        better = gscore[g] > bestv
        best = jnp.where(better, g, best)
        bestv = jnp.where(better, gscore[g], bestv)
    chosen = [in_top2[e] & (best == e // PER_GROUP) for e in range(N_EXPERTS)]
    denom = jnp.zeros((1, n), F32)
    for e in range(N_EXPERTS):
        denom = denom + jnp.where(chosen[e], affr[e], 0.0)
    lo = jnp.full((1, n), float(PER_GROUP), F32)
    hi = jnp.zeros((1, n), F32)
    w_lo = jnp.zeros((1, n), F32)
    w_hi = jnp.zeros((1, n), F32)
    for e in reversed(range(N_EXPERTS)):
        pos = float(e % PER_GROUP)
        lo = jnp.where(chosen[e], jnp.minimum(lo, pos), lo)
    for e in range(N_EXPERTS):
        pos = float(e % PER_GROUP)
        hi = jnp.where(chosen[e], jnp.maximum(hi, pos), hi)
    for e in range(N_EXPERTS):
        pos = float(e % PER_GROUP)
        gate = affr[e] / denom
        w_lo = jnp.where(chosen[e] & (lo == pos), gate, w_lo)
        w_hi = jnp.where(chosen[e] & (hi == pos), gate, w_hi)
    pair = lo * (2 * PER_GROUP - 1 - lo) * 0.5 + hi - lo - 1.0
    cls = best.astype(F32) * float(N_PAIRS) + pair
    return cls, w_lo, w_hi


def _outproj_even_kernel(ya_ref, of_ref, or_ref, gate_ref, bn_ref, c_ref, x_ref, *rest):
    xres = jnp.where(pl.program_id(1) == 0, c_ref[...], x_ref[...])
    parts = []
    for h in range(B_HEADS):
        hs = slice(h * B_HEAD_DIM, (h + 1) * B_HEAD_DIM)
        o = of_ref[:, hs].astype(F32) + or_ref[:, hs].astype(F32)
        y = o * lax.rsqrt(jnp.mean(o * o, axis=-1, keepdims=True) + RMS_EPS) * bn_ref[...]
        parts.append((y * _silu(gate_ref[:, hs].astype(F32))).astype(BF16))
    _outproj_body(ya_ref[...], jnp.concatenate(parts, axis=1), xres, *rest)


def _outproj_kernel(ya_ref, yb_ref, x_ref, *rest):
    _outproj_body(ya_ref[...], yb_ref[...], x_ref[...], *rest)


def _outproj_body(ya, yb, xres, mod_ref, w_ref, lng_ref, lnb_ref, rwt_ref, rb_ref,
                  xo_ref, h_ref, info_ref, cnt_ref, sel_scr, run_scr, wt_scr):
    mod = mod_ref[...]
    wa = w_ref[0:ya.shape[1], :]
    wb = w_ref[ya.shape[1]:, :]
    y = (jnp.dot(ya, wa, preferred_element_type=F32)
         + jnp.dot(yb, wb, preferred_element_type=F32))
    xn = _layer_norm(DEEPNORM_ALPHA * xres + mod[2:3] * y, lng_ref[...], lnb_ref[...])
    xo_ref[...] = xn
    h = xn * (1.0 + mod[4:5]) + mod[3:4]
    d = h.shape[1]
    h_ref[:, 0:d] = h
    logits_t = lax.dot_general(rwt_ref[...], h, (((1,), (1,)), ((), ())),
                               precision=HIGHEST, preferred_element_type=F32)
    cls, w_lo, w_hi = _route(logits_t, rb_ref[...], sel_scr)

    @pl.when(jnp.logical_and(pl.program_id(0) == 0, pl.program_id(1) == 0))
    def _():
        run_scr[...] = jnp.zeros_like(run_scr)

    n = cls.shape[1]
    crow = lax.broadcasted_iota(jnp.int32, (CLS_PAD, n), 0).astype(F32)
    onehot = jnp.where(crow == cls, 1.0, 0.0)
    si = lax.broadcasted_iota(jnp.int32, (n, n), 0)
    ti = lax.broadcasted_iota(jnp.int32, (n, n), 1)
    before = jnp.where(si < ti, 1.0, 0.0).astype(BF16)
    cum = jnp.dot(onehot.astype(BF16), before, preferred_element_type=F32)
    run = run_scr[...]
    rank = jnp.sum(onehot * (cum + run[:, 0:1]), axis=0, keepdims=True)
    run = run + jnp.sum(onehot, axis=1, keepdims=True)
    run_scr[...] = run
    cnt_ref[...] = run
    info_ref[...] = jnp.zeros_like(info_ref)
    info_ref[0:1, :] = cls
    info_ref[1:2, :] = rank
    wt_scr[...] = jnp.zeros_like(wt_scr)
    wt_scr[0:1, :] = w_lo
    wt_scr[1:2, :] = w_hi
    h_ref[:, d:] = wt_scr[...].T


def _outproj(ya, yb, xres, modarr, w_out, ln_g, ln_b, rwt, rbias, row_blk0, gdn=None):
    bsz, n, wa = ya.shape

    def tok(width, col_blk=0):
        return pl.BlockSpec((None, TM, width), lambda b, j: (b, j, col_blk))

    if gdn is None:
        d = xres.shape[2]
        body, mix_args = _outproj_kernel, (ya, yb, xres)
        mix_specs = [tok(wa), tok(yb.shape[2]),
                     pl.BlockSpec((None, TM, d), lambda b, j: (b, j + row_blk0, 0))]
    else:
        o_fwd, o_rev, p, b_norm = gdn
        ctx, x = xres
        d = x.shape[2]
        gate_blk = (3 * A_WIDTH + 3 * B_WIDTH) // B_WIDTH
        body, mix_args = _outproj_even_kernel, (ya, o_fwd, o_rev, p, b_norm, ctx, x)
        mix_specs = [tok(wa), tok(B_WIDTH), tok(B_WIDTH), tok(B_WIDTH, gate_blk),
                     pl.BlockSpec((1, B_HEAD_DIM), lambda b, j: (0, 0)),
                     pl.BlockSpec((None, TM, d), lambda b, j: (b, 0, 0)),
                     pl.BlockSpec((None, TM, d), lambda b, j: (b, jnp.maximum(j - 1, 0), 0))]
    return pl.pallas_call(
        body,
        grid=(bsz, n // TM),
        in_specs=mix_specs + [
            pl.BlockSpec((None, None, 6, d), lambda b, j: (b, jnp.minimum(j + row_blk0, 1), 0, 0)),
            pl.BlockSpec(w_out.shape, lambda b, j: (0, 0)),
            pl.BlockSpec((1, d), lambda b, j: (0, 0)),
            pl.BlockSpec((1, d), lambda b, j: (0, 0)),
            pl.BlockSpec((N_EXPERTS, d), lambda b, j: (0, 0)),
            pl.BlockSpec((N_EXPERTS, 1), lambda b, j: (0, 0)),
        ],
        out_specs=[
            pl.BlockSpec((None, TM, d), lambda b, j: (b, j, 0)),
            pl.BlockSpec((None, TM, d + LANES), lambda b, j: (b, j, 0)),
            pl.BlockSpec((None, None, 8, TM), lambda b, j: (b, j, 0, 0)),
            pl.BlockSpec((CLS_PAD, LANES), lambda b, j: (0, 0)),
        ],
        out_shape=[
            jax.ShapeDtypeStruct((bsz, n, d), F32),
            jax.ShapeDtypeStruct((bsz, n, d + LANES), F32),
            jax.ShapeDtypeStruct((bsz, n // TM, 8, TM), F32),
            jax.ShapeDtypeStruct((CLS_PAD, LANES), F32),
        ],
        scratch_shapes=[pltpu.VMEM((N_EXPERTS, TM), F32), pltpu.VMEM((CLS_PAD, LANES), F32),
                        pltpu.VMEM((LANES, TM), F32)],
        compiler_params=_cparams(2),
        name="outproj",
    )(*mix_args, modarr, w_out, ln_g, ln_b, rwt, rbias)


SCATTER_ROWS = 2048


def _row_scatter_kernel(dst_ref, src_hbm, init_hbm, out_hbm, sem):
    del init_hbm
    rows = dst_ref.shape[1]
    base = pl.program_id(0) * rows

    def body(r, carry):
        pltpu.make_async_copy(src_hbm.at[pl.ds(base + r, 1)], out_hbm.at[pl.ds(dst_ref[0, r], 1)],
                              sem).start()
        return carry

    lax.fori_loop(0, rows, body, 0, unroll=8)
    pltpu.make_async_copy(src_hbm.at[pl.ds(0, rows)], out_hbm.at[pl.ds(0, rows)], sem).wait()


def _row_scatter(src, dest, n_out):
    n, width = src.shape
    rows = math.gcd(n, SCATTER_ROWS)
    return pl.pallas_call(
        _row_scatter_kernel,
        grid=(n // rows,),
        in_specs=[
            pl.BlockSpec((None, 1, rows), lambda j: (j, 0, 0), memory_space=pltpu.SMEM),
            pl.BlockSpec(memory_space=pl.ANY),
            pl.BlockSpec(memory_space=pl.ANY),
        ],
        out_specs=pl.BlockSpec(memory_space=pl.ANY),
        out_shape=jax.ShapeDtypeStruct((n_out, width), src.dtype),
        scratch_shapes=[pltpu.SemaphoreType.DMA(())],
        input_output_aliases={2: 0},
        compiler_params=_cparams(1),
        name="row_scatter",
    )(dest.reshape(n // rows, 1, rows), src, jnp.zeros((n_out, width), src.dtype))


def _moe_kernel(elo_ref, ehi_ref, nused_ref, x_ref, wg_lo, wu_lo, wd_lo, wg_hi, wu_hi, wd_hi, o_ref):
    used = pl.program_id(0) < nused_ref[0]
    d = o_ref.shape[1]

    @pl.when(used)
    def _():
        x = x_ref[:, 0:d].astype(BF16)
        acc = None
        for col, (wg, wu, wd) in enumerate(((wg_lo, wu_lo, wd_lo), (wg_hi, wu_hi, wd_hi))):
            gate = jnp.dot(x, wg[...], preferred_element_type=F32)
            up = jnp.dot(x, wu[...], preferred_element_type=F32)
            act = (_silu(gate) * up).astype(BF16)
            y = x_ref[:, d + col:d + col + 1] * jnp.dot(act, wd[...], preferred_element_type=F32)
            acc = y if acc is None else acc + y
        o_ref[...] = acc

    @pl.when(jnp.logical_not(used))
    def _():
        o_ref[...] = jnp.zeros_like(o_ref)


def _moe_plan(info, counts, n_tok):
    mt = MOE_TM
    n_tiles = n_tok // mt + N_CLASSES
    cls = info[:, :, 0, :].reshape(-1).astype(jnp.int32)
    rank = info[:, :, 1, :].reshape(-1).astype(jnp.int32)
    cnt = counts[:N_CLASSES, 0].astype(jnp.int32)
    padded = ((cnt + mt - 1) // mt) * mt
    ends = jnp.cumsum(padded)
    starts = ends - padded
    classes = jnp.arange(N_CLASSES, dtype=jnp.int32)
    dest = jnp.sum(jnp.where(cls[:, None] == classes[None, :], starts[None, :], 0), axis=1) + rank
    n_used = ends[-1] // mt
    tidx = jnp.arange(n_tiles, dtype=jnp.int32)
    tidx = jnp.minimum(tidx, n_used - 1)
    tcls = jnp.sum((ends[None, :] <= (tidx * mt)[:, None]).astype(jnp.int32), axis=1)
    tcls = jnp.minimum(tcls, N_CLASSES - 1)
    pairs = [(a, b) for a in range(PER_GROUP) for b in range(a + 1, PER_GROUP)]
    pair = tcls % N_PAIRS
    lo = sum(jnp.where(pair == k, a, 0) for k, (a, _) in enumerate(pairs))
    hi = sum(jnp.where(pair == k, b, 0) for k, (_, b) in enumerate(pairs))
    group = tcls // N_PAIRS
    return (group * PER_GROUP + lo, group * PER_GROUP + hi, n_used.reshape(1).astype(jnp.int32),
            dest.astype(jnp.int32))


def _moe(h_ext, info, counts, wg, wu, wd):
    n, width = h_ext.shape
    d = width - LANES
    ne, _, de = wg.shape
    mt = MOE_TM
    elo, ehi, n_used, dest = _moe_plan(info, counts, n)
    n_tiles = n // mt + N_CLASSES
    h_sorted = _row_scatter(h_ext, dest, n_tiles * mt)

    def expert(which, shape):
        if which == 0:
            return pl.BlockSpec((None,) + shape, lambda i, lo, hi, nu: (lo[i], 0, 0))
        return pl.BlockSpec((None,) + shape, lambda i, lo, hi, nu: (hi[i], 0, 0))

    grid_spec = pltpu.PrefetchScalarGridSpec(
        num_scalar_prefetch=3,
        grid=(n_tiles,),
        in_specs=[
            pl.BlockSpec((mt, width), lambda i, lo, hi, nu: (jnp.minimum(i, nu[0] - 1), 0)),
            expert(0, (d, de)), expert(0, (d, de)), expert(0, (de, d)),
            expert(1, (d, de)), expert(1, (d, de)), expert(1, (de, d)),
        ],
        out_specs=pl.BlockSpec((mt, d), lambda i, *_: (i, 0)),
    )
    f_sorted = pl.pallas_call(
        _moe_kernel,
        grid_spec=grid_spec,
        out_shape=jax.ShapeDtypeStruct((n_tiles * mt, d), F32),
        compiler_params=_cparams(1),
        name="moe",
    )(elo, ehi, n_used, h_sorted, wg, wu, wd, wg, wu, wd)
    return f_sorted, dest


def _ln2_kernel(cur_ref, nxt_ref, x_ref, f_hbm, mod_ref, lng_ref, lnb_ref, o_ref, fbuf, sem):
    nj = pl.num_programs(1)
    step = pl.program_id(0) * nj + pl.program_id(1)
    n_steps = pl.num_programs(0) * nj
    slot = step % 2
    rows = fbuf.shape[1]

    def gather_start(idx_ref, s):
        def body(r, carry):
            pltpu.make_async_copy(f_hbm.at[pl.ds(idx_ref[0, r], 1)], fbuf.at[s, pl.ds(r, 1)],
                                  sem.at[s]).start()
            return carry
        lax.fori_loop(0, rows, body, 0, unroll=8)

    @pl.when(step == 0)
    def _():
        gather_start(cur_ref, 0)

    @pl.when(step + 1 < n_steps)
    def _():
        gather_start(nxt_ref, 1 - slot)

    pltpu.make_async_copy(f_hbm.at[pl.ds(0, rows)], fbuf.at[slot], sem.at[slot]).wait()
    mod = mod_ref[...]
    v = DEEPNORM_ALPHA * x_ref[...] + mod[5:6] * fbuf[slot]
    o_ref[...] = _layer_norm(v, lng_ref[...], lnb_ref[...])


def _ln2(x, f_sorted, dest, modarr, ln_g, ln_b, kind0):
    bsz, n, d = x.shape
    nj = n // TM
    n_steps = bsz * nj

    def idx_rows(offset):
        return pl.BlockSpec((None, 1, TM), lambda b, j: (jnp.minimum(b * nj + j + offset, n_steps - 1), 0, 0),
                            memory_space=pltpu.SMEM)

    dest3 = dest.reshape(n_steps, 1, TM)
    return pl.pallas_call(
        _ln2_kernel,
        grid=(bsz, nj),
        in_specs=[
            idx_rows(0), idx_rows(1),
            pl.BlockSpec((None, TM, d), lambda b, j: (b, j, 0)),
            pl.BlockSpec(memory_space=pl.ANY),
            pl.BlockSpec((None, None, 6, d), lambda b, j: (b, jnp.minimum(j + kind0, 1), 0, 0)),
            pl.BlockSpec((1, d), lambda b, j: (0, 0)),
            pl.BlockSpec((1, d), lambda b, j: (0, 0)),
        ],
        out_specs=pl.BlockSpec((None, TM, d), lambda b, j: (b, j, 0)),
        out_shape=jax.ShapeDtypeStruct((bsz, n, d), F32),
        scratch_shapes=[pltpu.VMEM((2, TM, d), F32), pltpu.SemaphoreType.DMA((2,))],
        compiler_params=_cparams(2),
        name="ln2",
    )(dest3, dest3, x, f_sorted, modarr, ln_g, ln_b)


def _rope(x, c, s1, s2, shift):
    w = x.shape[1]
    return x * c + pltpu.roll(x, w - shift, 1) * s1 + pltpu.roll(x, shift, 1) * s2


def _rms(x, g):
    return x * lax.rsqrt(jnp.mean(x * x, axis=-1, keepdims=True) + RMS_EPS) * g


def _inproj_odd_kernel(x_ref, mod_ref, w_ref, qn_ref, kvn_ref, wuq_ref, wk_ref, we_ref, wv_ref,
                       tw_ref, tq_ref, tk_ref, qw_ref, kw_ref, vw_ref, qm_ref, km_ref, vm_ref):
    j = pl.program_id(1)
    is_ctx = j == 0
    mod = mod_ref[...]
    h = (x_ref[...] * (1.0 + mod[1:2]) + mod[0:1]).astype(BF16)
    p = jnp.dot(h, w_ref[...], preferred_element_type=F32)

    def tables(t_ref):
        c = jnp.where(is_ctx, 1.0, t_ref[0])
        s1 = jnp.where(is_ctx, 0.0, t_ref[1])
        s2 = jnp.where(is_ctx, 0.0, t_ref[2])
        return c, s1, s2

    cw, s1w, s2w = tables(tw_ref)
    nq = C_Q_HEADS * C_HEAD_DIM
    for r in range(nq // LANES):
        blk = _rope(p[:, r * LANES:(r + 1) * LANES], cw, s1w, s2w, C_HEAD_DIM // 2)
        qw_ref[:, r * LANES:(r + 1) * LANES] = (blk * (C_HEAD_DIM ** -0.5)).astype(BF16)
    kw_ref[...] = _rope(p[:, nq:nq + LANES], cw, s1w, s2w, C_HEAD_DIM // 2).astype(BF16)
    vw_ref[...] = p[:, nq + LANES:nq + 2 * LANES].astype(BF16)

    o = nq + 2 * LANES
    dq = _rms(p[:, o:o + D_Q_RANK], qn_ref[...]).astype(BF16)
    o += D_Q_RANK
    dkv = _rms(p[:, o:o + D_KV_RANK], kvn_ref[...]).astype(BF16)
    o += D_KV_RANK
    cq, s1q, s2q = tables(tq_ref)
    ck, s1k, s2k = tables(tk_ref)
    krope = _rope(p[:, o:o + LANES], ck, s1k, s2k, D_ROPE // 2).astype(BF16)
    scale = (D_NOPE + D_ROPE) ** -0.5
    for hh in range(D_HEADS):
        sl = slice(hh * LANES, (hh + 1) * LANES)
        qh = jnp.dot(dq, wuq_ref[:, sl], preferred_element_type=F32)
        qm_ref[:, sl] = (_rope(qh, cq, s1q, s2q, D_ROPE // 2) * scale).astype(BF16)
        kh = (jnp.dot(dkv, wk_ref[:, sl], preferred_element_type=F32)
              + jnp.dot(krope, we_ref[:, sl], preferred_element_type=F32))
        km_ref[:, sl] = kh.astype(BF16)
        vm_ref[:, sl] = jnp.dot(dkv, wv_ref[:, sl], preferred_element_type=F32).astype(BF16)


def _inproj_odd(xin, modarr, w1, qnorm, kvnorm, wuq, wk, we, wv, tab_w, tab_q, tab_k):
    bsz, t, d = xin.shape
    n1 = w1.shape[1]
    hw = D_HEADS * LANES

    def tab_spec():
        return pl.BlockSpec((3, TM, LANES), lambda b, j: (0, jnp.maximum(j - 1, 0), 0))

    def full(a):
        return pl.BlockSpec(a.shape, lambda b, j: (0,) * a.ndim)

    def out(width):
        return pl.BlockSpec((None, TM, width), lambda b, j: (b, j, 0))

    widths = (C_Q_HEADS * C_HEAD_DIM, LANES, LANES, hw, hw, hw)
    return pl.pallas_call(
        _inproj_odd_kernel,
        grid=(bsz, t // TM),
        in_specs=[
            pl.BlockSpec((None, TM, d), lambda b, j: (b, j, 0)),
            pl.BlockSpec((None, None, 6, d), lambda b, j: (b, jnp.minimum(j, 1), 0, 0)),
            full(w1), full(qnorm), full(kvnorm), full(wuq), full(wk), full(we), full(wv),
            tab_spec(), tab_spec(), tab_spec(),
        ],
        out_specs=[out(w) for w in widths],
        out_shape=[jax.ShapeDtypeStruct((bsz, t, w), BF16) for w in widths],
        compiler_params=_cparams(2),
        name="inproj_odd",
    )(xin, modarr, w1, qnorm, kvnorm, wuq, wk, we, wv, tab_w, tab_q, tab_k)


WIN_TQ = 256


def _win_kernel(sink_ref, q_ref, k_ref, v_ref, o_ref, klo_scr, khi_scr, *, n_ctx):
    i = pl.program_id(1)
    wdw = C_WINDOW
    t = k_ref.shape[0]
    lane = lax.broadcasted_iota(jnp.int32, (t, LANES), 1)

    @pl.when(i == 0)
    def _():
        kk = k_ref[...]
        klo_scr[...] = jnp.where(lane < C_HEAD_DIM, kk, jnp.zeros_like(kk))
        khi_scr[...] = jnp.where(lane >= C_HEAD_DIM, kk, jnp.zeros_like(kk))

    tq = q_ref.shape[0]
    span = tq + 2 * wdw
    n_lat_blk = (t - n_ctx) // wdw
    blk0 = jnp.clip(i * (tq // wdw) - 1, 0, n_lat_blk - span // wdw)
    r0 = pl.multiple_of(n_ctx + blk0 * wdw, wdw)
    kpos = blk0 * wdw + lax.broadcasted_iota(jnp.int32, (tq, span), 1)
    qpos = i * tq + lax.broadcasted_iota(jnp.int32, (tq, span), 0)
    near = jnp.abs(kpos - qpos) <= wdw
    v_loc = v_ref[pl.ds(r0, span), :]
    v_ctx = v_ref[0:n_ctx, :]
    olane = lax.broadcasted_iota(jnp.int32, (tq, LANES), 1)
    n_rep = C_Q_HEADS // C_KV_HEADS
    for r in range(n_rep):
        q = q_ref[:, r * LANES:(r + 1) * LANES]
        outs = []
        for g, k_scr in enumerate((klo_scr, khi_scr)):
            k_loc = k_scr[pl.ds(r0, span), :]
            k_ctx = k_scr[0:n_ctx, :]
            s_loc = lax.dot_general(q, k_loc, (((1,), (1,)), ((), ())), preferred_element_type=F32)
            s_loc = jnp.where(near, s_loc, -jnp.inf)
            s_ctx = lax.dot_general(q, k_ctx, (((1,), (1,)), ((), ())), preferred_element_type=F32)
            sink = sink_ref[g * n_rep + r]
            m = jnp.maximum(jnp.maximum(jnp.max(s_loc, axis=-1, keepdims=True),
                                        jnp.max(s_ctx, axis=-1, keepdims=True)), sink)
            p_loc = jnp.exp(s_loc - m)
            p_ctx = jnp.exp(s_ctx - m)
            den = (jnp.sum(p_loc, axis=-1, keepdims=True) + jnp.sum(p_ctx, axis=-1, keepdims=True)
                   + jnp.exp(sink - m))
            pv = (jnp.dot(p_loc.astype(BF16), v_loc, preferred_element_type=F32)
                  + jnp.dot(p_ctx.astype(BF16), v_ctx, preferred_element_type=F32))
            outs.append(pv / den)
        o_ref[:, r * LANES:(r + 1) * LANES] = jnp.where(olane < C_HEAD_DIM, outs[0], outs[1]).astype(BF16)


def _win_attention(sink, qw, kw, vw, n_ctx):
    bsz, t, nq = qw.shape
    n_lat = t - n_ctx
    ctx_blk = n_ctx // WIN_TQ
    grid_spec = pltpu.PrefetchScalarGridSpec(
        num_scalar_prefetch=1,
        grid=(bsz, n_lat // WIN_TQ),
        in_specs=[
            pl.BlockSpec((None, WIN_TQ, nq), lambda b, i, s: (b, i + ctx_blk, 0)),
            pl.BlockSpec((None, t, LANES), lambda b, i, s: (b, 0, 0)),
            pl.BlockSpec((None, t, LANES), lambda b, i, s: (b, 0, 0)),
        ],
        out_specs=pl.BlockSpec((None, WIN_TQ, nq), lambda b, i, s: (b, i, 0)),
        scratch_shapes=[pltpu.VMEM((t, LANES), BF16), pltpu.VMEM((t, LANES), BF16)],
    )
    return pl.pallas_call(
        functools.partial(_win_kernel, n_ctx=n_ctx),
        grid_spec=grid_spec,
        out_shape=jax.ShapeDtypeStruct((bsz, n_lat, nq), BF16),
        compiler_params=_cparams(2),
        name="win_attention",
    )(sink, qw, kw, vw)


MLA_TQ = 256


def _mla_kernel(q_ref, k_ref, v_ref, o_ref):
    for hp in range(D_HEADS // 2):
        acc = None
        for hh in (2 * hp, 2 * hp + 1):
            sl = slice(hh * LANES, (hh + 1) * LANES)
            s = lax.dot_general(q_ref[:, sl], k_ref[:, sl], (((1,), (1,)), ((), ())),
                                preferred_element_type=F32)
            m = jnp.max(s, axis=-1, keepdims=True)
            p = jnp.exp(s - m)
            den = jnp.sum(p, axis=-1, keepdims=True)
            pv = jnp.dot(p.astype(BF16), v_ref[:, sl], preferred_element_type=F32) / den
            acc = pv if acc is None else acc + pv
        o_ref[:, hp * LANES:(hp + 1) * LANES] = acc.astype(BF16)


def _mla_attention(qm, km, vm, n_ctx):
    bsz, t, hw = qm.shape
    n_lat = t - n_ctx
    ctx_blk = n_ctx // MLA_TQ
    ow = D_HEADS * D_V
    return pl.pallas_call(
        _mla_kernel,
        grid=(bsz, n_lat // MLA_TQ),
        in_specs=[
            pl.BlockSpec((None, MLA_TQ, hw), lambda b, i: (b, i + ctx_blk, 0)),
            pl.BlockSpec((None, t, hw), lambda b, i: (b, 0, 0)),
            pl.BlockSpec((None, t, hw), lambda b, i: (b, 0, 0)),
        ],
        out_specs=pl.BlockSpec((None, MLA_TQ, ow), lambda b, i: (b, i, 0)),
        out_shape=jax.ShapeDtypeStruct((bsz, n_lat, ow), BF16),
        compiler_params=_cparams(2),
        name="mla_attention",
    )(qm, km, vm)


def _rope_tables(n_tokens, rot_dim, group, offset):
    t = jnp.arange(n_tokens)
    rows = (t // GRID_W).astype(F32)
    cols = (t % GRID_W).astype(F32)
    n_freq = rot_dim // 4
    inv_freq = ROPE_BASE ** (-jnp.arange(n_freq, dtype=F32) / n_freq)
    ang = jnp.concatenate([rows[:, None] * inv_freq, cols[:, None] * inv_freq], -1)
    cos, sin = jnp.cos(ang), jnp.sin(ang)
    half = rot_dim // 2
    c = jnp.ones((n_tokens, LANES), F32)
    s1 = jnp.zeros((n_tokens, LANES), F32)
    s2 = jnp.zeros((n_tokens, LANES), F32)
    for start in range(offset, LANES, group):
        c = c.at[:, start:start + half].set(cos).at[:, start + half:start + rot_dim].set(cos)
        s1 = s1.at[:, start:start + half].set(-sin)
        s2 = s2.at[:, start + half:start + rot_dim].set(sin)
    return jnp.stack([c, s1, s2])


def _odd_weights(w_in, wuq, wukv, w_out):
    d = w_in.shape[0]
    nq = C_Q_HEADS * C_HEAD_DIM
    nkv = C_KV_HEADS * C_HEAD_DIM
    n_rep = C_Q_HEADS // C_KV_HEADS
    order = [g * n_rep + r for r in range(n_rep) for g in range(C_KV_HEADS)]
    cq = w_in[:, :nq].reshape(d, C_Q_HEADS, C_HEAD_DIM)[:, order].reshape(d, nq)
    rest = w_in[:, nq:nq + 2 * nkv + D_Q_RANK + D_KV_RANK]
    krope = jnp.pad(w_in[:, nq + 2 * nkv + D_Q_RANK + D_KV_RANK:], ((0, 0), (0, LANES - D_ROPE)))
    w1 = jnp.concatenate([cq, rest, krope], axis=1).astype(BF16)
    qh = wuq.reshape(D_Q_RANK, D_HEADS, D_NOPE + D_ROPE)
    wuq_p = jnp.pad(qh, ((0, 0), (0, 0), (0, LANES - D_NOPE - D_ROPE))).reshape(D_Q_RANK, D_HEADS * LANES)
    kvh = wukv.reshape(D_KV_RANK, D_HEADS, D_NOPE + D_V)
    wk_p = jnp.pad(kvh[:, :, :D_NOPE], ((0, 0), (0, 0), (0, LANES - D_NOPE))).reshape(D_KV_RANK, D_HEADS * LANES)
    e_blk = jnp.zeros((LANES, LANES), F32).at[jnp.arange(D_ROPE), D_NOPE + jnp.arange(D_ROPE)].set(1.0)
    we = jnp.tile(e_blk, (1, D_HEADS))
    vh = kvh[:, :, D_NOPE:]
    even = (jnp.arange(D_HEADS) % 2 == 0)[None, :, None]
    wv_p = jnp.where(even, jnp.pad(vh, ((0, 0), (0, 0), (0, D_V))),
                     jnp.pad(vh, ((0, 0), (0, 0), (D_V, 0)))).reshape(D_KV_RANK, D_HEADS * LANES)
    wo_win = w_out[:nq].reshape(C_Q_HEADS, C_HEAD_DIM, -1)[jnp.array(order)].reshape(nq, -1)
    wo = jnp.concatenate([wo_win, w_out[nq:]], axis=0).astype(BF16)
    return w1, wuq_p.astype(BF16), wk_p.astype(BF16), we.astype(BF16), wv_p.astype(BF16), wo


def kernel(x, c, ctx, c_ctx, ada_w, ada_b, ln_g, ln_b, ev_w_in, ev_a_conv, ev_b_conv, ev_b_alog, ev_b_dtbias, ev_b_norm, ev_w_out, od_w_in, od_c_sink, od_d_qnorm, od_d_kvnorm, od_d_wuq, od_d_wukv, od_w_out, router_w, router_bias, moe_w_gate, moe_w_up, moe_w_down):
    bsz, n_lat, d = x.shape
    n_ctx = ctx.shape[1]
    assert n_ctx == TM and n_lat % TM == 0 and n_lat % GRID_W == 0
    assert ada_w.shape[0] == DEPTH and bsz + 1 <= 40
    t = n_ctx + n_lat

    cs = jnp.zeros((40, d), F32).at[:bsz].set(c).at[bsz].set(c_ctx)
    mods = _ada_mod(cs, ada_w, ada_b)

    def modarr(layer):
        m = mods[layer].reshape(40, 6, d)
        return jnp.stack([jnp.broadcast_to(m[bsz], (bsz, 6, d)), m[:bsz]], axis=1)

    rwt = router_w.T
    rbias = router_bias.reshape(N_EXPERTS, 1)

    mod0 = modarr(0)
    n_main = 3 * A_WIDTH + 4 * B_WIDTH
    w_main = ev_w_in[0][:, :n_main].astype(BF16)
    w_small = jnp.pad(ev_w_in[0][:, n_main:], ((0, 0), (0, LANES - 4 * B_HEADS))).astype(BF16)
    p, small = _inproj_even(ctx, x, mod0, w_main, w_small)
    alog_pad = jnp.zeros((1, LANES), F32).at[0, 8:16].set(ev_b_alog[0].reshape(-1))
    dtb_pad = jnp.zeros((1, LANES), F32).at[0, 8:16].set(ev_b_dtbias[0].reshape(-1))
    ya, u, w, qe, ket, att, dec = _even_prep(p, small, ev_a_conv[0], ev_b_conv[0], alog_pad, dtb_pad,
                                             GDN_CHUNK)
    o_fwd, o_rev = _gdn_rec(u, w, qe, ket, att, dec, GDN_CHUNK)
    x1, h1, info0, cnt0 = _outproj(ya, None, (ctx, x), mod0, ev_w_out[0].astype(BF16),
                             ln_g[0, 0].reshape(1, d), ln_b[0, 0].reshape(1, d), rwt, rbias, 0,
                             gdn=(o_fwd, o_rev, p, ev_b_norm[0].reshape(1, B_HEAD_DIM)))
    f, dest = _moe(h1.reshape(bsz * t, d + LANES), info0, cnt0,
                   moe_w_gate[0].astype(BF16), moe_w_up[0].astype(BF16), moe_w_down[0].astype(BF16))
    x2 = _ln2(x1, f, dest, mod0, ln_g[0, 1].reshape(1, d), ln_b[0, 1].reshape(1, d), 0)

    mod1 = modarr(1)
    w1, wuq_p, wk_p, we, wv_p, wo = _odd_weights(od_w_in[0], od_d_wuq[0], od_d_wukv[0], od_w_out[0])
    tab_w = _rope_tables(n_lat, C_HEAD_DIM, C_HEAD_DIM, 0)
    tab_q = _rope_tables(n_lat, D_ROPE, LANES, D_NOPE)
    tab_k = _rope_tables(n_lat, D_ROPE, LANES, 0)
    qw, kw, vw, qm, km, vm = _inproj_odd(
        x2, mod1, w1, od_d_qnorm[0].reshape(1, -1), od_d_kvnorm[0].reshape(1, -1),
        wuq_p, wk_p, we, wv_p, tab_w, tab_q, tab_k)
    y_win = _win_attention(od_c_sink[0], qw, kw, vw, n_ctx)
    y_mla = _mla_attention(qm, km, vm, n_ctx)
    x3, h3, info1, cnt1 = _outproj(y_win, y_mla, x2, mod1, wo, ln_g[1, 0].reshape(1, d),
                                   ln_b[1, 0].reshape(1, d), rwt, rbias, n_ctx // TM)
    f1, dest1 = _moe(h3.reshape(bsz * n_lat, d + LANES), info1, cnt1,
                     moe_w_gate[1].astype(BF16), moe_w_up[1].astype(BF16), moe_w_down[1].astype(BF16))
    return _ln2(x3, f1, dest1, mod1, ln_g[1, 1].reshape(1, d), ln_b[1, 1].reshape(1, d), 1)
```

```python
import functools
import math

import numpy as np
import jax
import jax.numpy as jnp
from jax import lax
from jax.experimental import pallas as pl
from jax.experimental.pallas import tpu as pltpu

F32 = jnp.float32
BF16 = jnp.bfloat16
HIGHEST = lax.Precision.HIGHEST

DEPTH = 2
GRID_W = 64
DEEPNORM_ALPHA = (2.0 * DEPTH) ** 0.25
LN_EPS = 1e-5
RMS_EPS = 1e-6
ROPE_BASE = 10000.0
B_HEADS = 4
B_HEAD_DIM = 128
B_WIDTH = 512
A_WIDTH = 512
C_Q_HEADS = 8
C_KV_HEADS = 2
C_HEAD_DIM = 64
C_WINDOW = 128
D_HEADS = 8
D_NOPE = 64
D_ROPE = 32
D_V = 64
D_Q_RANK = 384
D_KV_RANK = 256
N_EXPERTS = 16
N_GROUPS = 4
PER_GROUP = N_EXPERTS // N_GROUPS
D_EXPERT = 512
N_PAIRS = PER_GROUP * (PER_GROUP - 1) // 2
N_CLASSES = N_GROUPS * N_PAIRS
CLS_PAD = 32

LANES = 128
TM = 256
GDN_CHUNK = 64
MOE_TM = 256
VMEM_LIMIT = 56 * 1024 * 1024


def _cparams(n_axes, vmem=VMEM_LIMIT):
    return pltpu.CompilerParams(dimension_semantics=("arbitrary",) * n_axes, vmem_limit_bytes=vmem)


def _sigmoid(x):
    return 1.0 / (1.0 + jnp.exp(-x))


def _silu(x):
    return x * _sigmoid(x)


def _softplus(x):
    return jnp.maximum(x, 0.0) + jnp.log(1.0 + jnp.exp(-jnp.abs(x)))


def _layer_norm(v, g, b):
    mu = jnp.mean(v, axis=-1, keepdims=True)
    d = v - mu
    var = jnp.mean(d * d, axis=-1, keepdims=True)
    return d * lax.rsqrt(var + LN_EPS) * g + b


def _ada_kernel(c_ref, w_ref, b_ref, o_ref):
    s = _silu(c_ref[...])
    o_ref[...] = jnp.dot(s, w_ref[...], precision=HIGHEST, preferred_element_type=F32) + b_ref[...]


def _ada_mod(cs, ada_w, ada_b):
    depth, d, n6 = ada_w.shape
    rows = cs.shape[0]
    tn = 1536
    return pl.pallas_call(
        _ada_kernel,
        grid=(depth, n6 // tn),
        in_specs=[
            pl.BlockSpec((rows, d), lambda l, n: (0, 0)),
            pl.BlockSpec((None, d, tn), lambda l, n: (l, 0, n)),
            pl.BlockSpec((None, 1, tn), lambda l, n: (l, 0, n)),
        ],
        out_specs=pl.BlockSpec((None, rows, tn), lambda l, n: (l, 0, n)),
        out_shape=jax.ShapeDtypeStruct((depth, rows, n6), F32),
        compiler_params=_cparams(2),
        name="ada_mod",
    )(cs, ada_w, ada_b.reshape(depth, 1, n6))


def _inproj_even_kernel(c_ref, x_ref, mod_ref, wm_ref, ws_ref, p_ref, s_ref):
    mod = mod_ref[...]
    xv = jnp.where(pl.program_id(1) == 0, c_ref[...], x_ref[...])
    h = (xv * (1.0 + mod[1:2]) + mod[0:1]).astype(BF16)
    p_ref[...] = jnp.dot(h, wm_ref[...], preferred_element_type=F32).astype(BF16)
    s_ref[...] = jnp.dot(h, ws_ref[...], preferred_element_type=F32)


def _inproj_even(ctx, x, modarr, w_main, w_small):
    bsz, n_lat, d = x.shape
    t = ctx.shape[1] + n_lat
    nm = w_main.shape[1]
    return pl.pallas_call(
        _inproj_even_kernel,
        grid=(bsz, t // TM),
        in_specs=[
            pl.BlockSpec((None, TM, d), lambda b, j: (b, 0, 0)),
            pl.BlockSpec((None, TM, d), lambda b, j: (b, jnp.maximum(j - 1, 0), 0)),
            pl.BlockSpec((None, None, 6, d), lambda b, j: (b, jnp.minimum(j, 1), 0, 0)),
            pl.BlockSpec((d, nm), lambda b, j: (0, 0)),
            pl.BlockSpec((d, LANES), lambda b, j: (0, 0)),
        ],
        out_specs=[
            pl.BlockSpec((None, TM, nm), lambda b, j: (b, j, 0)),
            pl.BlockSpec((None, TM, LANES), lambda b, j: (b, j, 0)),
        ],
        out_shape=[
            jax.ShapeDtypeStruct((bsz, t, nm), BF16),
            jax.ShapeDtypeStruct((bsz, t, LANES), F32),
        ],
        compiler_params=_cparams(2),
        name="inproj_even",
    )(ctx, x, modarr, w_main, w_small)


HALO = 16


def _conv3(z, zp, zn, w):
    n = z.shape[0]
    rows = lax.broadcasted_iota(jnp.int32, z.shape, 0)
    zprev = jnp.where(rows == 0, zp, pltpu.roll(z, 1, 0))
    znext = jnp.where(rows == n - 1, zn, pltpu.roll(z, n - 1, 0))
    return w[0:1] * zprev + w[1:2] * z + w[2:3] * znext


def _even_prep_kernel(p_ref, pp_ref, pn_ref, s_ref, aw_ref, bw_ref, alog_ref, dtb_ref,
                      ya_ref, u_ref, w_ref, qe_ref, ket_ref, att_ref, dec_ref,
                      q_scr, k_scr, v_scr, *, chunk):
    j = pl.program_id(1)
    nj = pl.num_programs(1)
    prev_on = jnp.where(jnp.logical_and(j != 0, j != 1), 1.0, 0.0)
    next_on = jnp.where(jnp.logical_and(j != 0, j != nj - 1), 1.0, 0.0)
    prow = pp_ref[...].astype(F32)[HALO - 1:HALO] * prev_on
    nrow = pn_ref[...].astype(F32)[0:1] * next_on

    def seg(lo, hi):
        return p_ref[:, lo:hi].astype(F32), prow[:, lo:hi], nrow[:, lo:hi]

    a0, _, _ = seg(0, A_WIDTH)
    a1, a1p, a1n = seg(A_WIDTH, 2 * A_WIDTH)
    a2, a2p, a2n = seg(2 * A_WIDTH, 3 * A_WIDTH)
    ya_ref[...] = (a0 * _conv3(a1 * a2, a1p * a2p, a1n * a2n, aw_ref[...])).astype(BF16)

    base = 3 * A_WIDTH
    for which in range(3):
        lo = base + which * B_WIDTH
        z, zp, zn = seg(lo, lo + B_WIDTH)
        c = _silu(_conv3(z, zp, zn, bw_ref[:, which * B_WIDTH:(which + 1) * B_WIDTH]))
        for h in range(B_HEADS):
            ch = c[:, h * B_HEAD_DIM:(h + 1) * B_HEAD_DIM]
            if which < 2:
                ss = jnp.sum(ch * ch, axis=-1, keepdims=True)
                ch = ch * lax.rsqrt(ss + 1e-6)
                if which == 0:
                    ch = ch * (B_HEAD_DIM ** -0.5)
            sl = slice(h * B_HEAD_DIM, (h + 1) * B_HEAD_DIM)
            if which == 0:
                q_scr[:, sl] = ch.astype(BF16)
            elif which == 1:
                k_scr[:, sl] = ch.astype(BF16)
            else:
                v_scr[:, sl] = ch

    s = s_ref[...]
    beta = _sigmoid(s)
    g = -jnp.exp(alog_ref[...]) * _softplus(s + dtb_ref[...])
    n = s.shape[0]
    nck = n // chunk
    ri = lax.broadcasted_iota(jnp.int32, (n, n), 0)
    ci = lax.broadcasted_iota(jnp.int32, (n, n), 1)
    same = (ri // chunk) == (ci // chunk)
    m_fwd = jnp.where(jnp.logical_and(same, ci <= ri), 1.0, 0.0)
    m_rev = jnp.where(jnp.logical_and(same, ci >= ri), 1.0, 0.0)
    m_all = jnp.where(same, 1.0, 0.0)
    gc_f = jnp.dot(m_fwd, g, precision=HIGHEST, preferred_element_type=F32)
    gc_r = jnp.dot(m_rev, g, precision=HIGHEST, preferred_element_type=F32)
    tot = jnp.dot(m_all, g, precision=HIGHEST, preferred_element_type=F32)
    lane = lax.broadcasted_iota(jnp.int32, s.shape, 1)
    gc = jnp.where(lane >= 8 + B_HEADS, gc_r, gc_f)
    e_gc = jnp.exp(gc)
    e_rest = jnp.exp(tot - gc)
    gct = gc.T
    e_tot = jnp.exp(tot)

    ri = lax.broadcasted_iota(jnp.int32, (1, chunk, chunk), 1)
    ci = lax.broadcasted_iota(jnp.int32, (1, chunk, chunk), 2)
    eye = jnp.where(ri == ci, 1.0, 0.0)
    n_sq = int(np.log2(chunk))
    nt_batched = (((2,), (2,)), ((0,), (0,)))
    nn_batched = (((2,), (1,)), ((0,), (0,)))
    kk, qk, kf, qf = [], [], [], []
    for h in range(B_HEADS):
        hs = slice(h * B_HEAD_DIM, (h + 1) * B_HEAD_DIM)
        k3 = k_scr[:, hs].reshape(nck, chunk, B_HEAD_DIM)
        q3 = q_scr[:, hs].reshape(nck, chunk, B_HEAD_DIM)
        kk.append(lax.dot_general(k3, k3, nt_batched, preferred_element_type=F32))
        qk.append(lax.dot_general(q3, k3, nt_batched, preferred_element_type=F32))
        kf.append(k_scr[:, hs].astype(F32))
        qf.append(q_scr[:, hs].astype(F32))
    a_all, rhs_all = [], []
    for d in range(2):
        incl = (ci <= ri) if d == 0 else (ci >= ri)
        strict = (ci < ri) if d == 0 else (ci > ri)
        for h in range(B_HEADS):
            chain = d * B_HEADS + h
            hs = slice(h * B_HEAD_DIM, (h + 1) * B_HEAD_DIM)
            bcol = beta[:, chain:chain + 1]
            e1 = e_gc[:, 8 + chain:9 + chain]
            e2 = e_rest[:, 8 + chain:9 + chain]
            gcol = gc[:, 8 + chain:9 + chain].reshape(nck, chunk, 1)
            grow = jnp.stack([gct[8 + chain:9 + chain, cc * chunk:(cc + 1) * chunk]
                              for cc in range(nck)], axis=0)
            decay = jnp.exp(jnp.where(incl, gcol - grow, -jnp.inf))
            a_all.append(jnp.where(strict, bcol.reshape(nck, chunk, 1) * kk[h] * decay, 0.0))
            att_ref[d, h] = (qk[h] * decay).reshape(n, chunk).astype(BF16)
            qe_ref[d, h] = (qf[h] * e1).astype(BF16)
            ket = (kf[h] * e2).T
            for cc in range(nck):
                ket_ref[d, h, cc] = ket[:, cc * chunk:(cc + 1) * chunk].astype(BF16)
                dec_ref[cc, chain] = jnp.broadcast_to(
                    e_tot[cc * chunk:cc * chunk + 1, 8 + chain:9 + chain], (1, LANES))
            rhs = jnp.concatenate([(v_scr[:, hs] * bcol).astype(BF16),
                                   (kf[h] * (bcol * e1)).astype(BF16)], axis=1)
            rhs_all.append(rhs.reshape(nck, chunk, 2 * B_HEAD_DIM))
    npow = -jnp.concatenate(a_all, axis=0)
    tinv = eye + npow
    for _ in range(n_sq - 1):
        nb = npow.astype(BF16)
        npow = lax.dot_general(nb, nb, nn_batched, preferred_element_type=F32)
        tinv = tinv + lax.dot_general(tinv.astype(BF16), npow.astype(BF16), nn_batched,
                                      preferred_element_type=F32)
    uw = lax.dot_general(tinv.astype(BF16), jnp.concatenate(rhs_all, axis=0), nn_batched,
                         preferred_element_type=F32)
    for d in range(2):
        for h in range(B_HEADS):
            blk = uw[(d * B_HEADS + h) * nck:(d * B_HEADS + h + 1) * nck]
            u_ref[d, h] = blk[:, :, :B_HEAD_DIM].reshape(n, B_HEAD_DIM).astype(BF16)
            w_ref[d, h] = blk[:, :, B_HEAD_DIM:].reshape(n, B_HEAD_DIM).astype(BF16)


def _even_prep(p, small, a_conv, b_conv, alog_pad, dtb_pad, chunk):
    bsz, t, nm = p.shape
    nhb = TM // HALO
    last_hb = t // HALO - 1
    nck = TM // chunk
    nc = t // chunk

    def per_dir():
        return (pl.BlockSpec((None, 2, B_HEADS, TM, B_HEAD_DIM), lambda b, j: (b, 0, 0, j, 0)),
                jax.ShapeDtypeStruct((bsz, 2, B_HEADS, t, B_HEAD_DIM), BF16))

    outs = [
        (pl.BlockSpec((None, TM, A_WIDTH), lambda b, j: (b, j, 0)),
         jax.ShapeDtypeStruct((bsz, t, A_WIDTH), BF16)),
        per_dir(), per_dir(), per_dir(),
        (pl.BlockSpec((None, 2, B_HEADS, nck, B_HEAD_DIM, chunk), lambda b, j: (b, 0, 0, j, 0, 0)),
         jax.ShapeDtypeStruct((bsz, 2, B_HEADS, nc, B_HEAD_DIM, chunk), BF16)),
        (pl.BlockSpec((None, 2, B_HEADS, TM, chunk), lambda b, j: (b, 0, 0, j, 0)),
         jax.ShapeDtypeStruct((bsz, 2, B_HEADS, t, chunk), BF16)),
        (pl.BlockSpec((None, nck, 2 * B_HEADS, 1, LANES), lambda b, j: (b, j, 0, 0, 0)),
         jax.ShapeDtypeStruct((bsz, nc, 2 * B_HEADS, 1, LANES), F32)),
    ]
    return pl.pallas_call(
        functools.partial(_even_prep_kernel, chunk=chunk),
        grid=(bsz, t // TM),
        in_specs=[
            pl.BlockSpec((None, TM, nm), lambda b, j: (b, j, 0)),
            pl.BlockSpec((None, HALO, nm), lambda b, j: (b, jnp.maximum(j * nhb - 1, 0), 0)),
            pl.BlockSpec((None, HALO, nm), lambda b, j: (b, jnp.minimum((j + 1) * nhb, last_hb), 0)),
            pl.BlockSpec((None, TM, LANES), lambda b, j: (b, j, 0)),
            pl.BlockSpec((3, A_WIDTH), lambda b, j: (0, 0)),
            pl.BlockSpec((3, 3 * B_WIDTH), lambda b, j: (0, 0)),
            pl.BlockSpec((1, LANES), lambda b, j: (0, 0)),
            pl.BlockSpec((1, LANES), lambda b, j: (0, 0)),
        ],
        out_specs=[o[0] for o in outs],
        out_shape=[o[1] for o in outs],
        scratch_shapes=[pltpu.VMEM((TM, B_WIDTH), BF16), pltpu.VMEM((TM, B_WIDTH), BF16),
                        pltpu.VMEM((TM, B_WIDTH), F32)],
        compiler_params=_cparams(2),
        name="even_prep",
    )(p, p, p, small, a_conv, b_conv, alog_pad, dtb_pad)


def _gdn_rec_kernel(uf, wf, qf, kf, af, df, ur, wr, qr, kr, ar, dr, of_ref, or_ref, s_scr, *, chunk):
    @pl.when(pl.program_id(1) == 0)
    def _():
        s_scr[...] = jnp.zeros_like(s_scr)

    nck = uf.shape[1] // chunk
    nn_batched = (((2,), (1,)), ((0,), (0,)))

    def both(fwd, rev):
        return jnp.concatenate([fwd, rev], axis=0)

    for cc in range(nck):
        cr = nck - 1 - cc
        rf = slice(cc * chunk, (cc + 1) * chunk)
        rr = slice(cr * chunk, (cr + 1) * chunk)
        s = s_scr[...]
        sb = s.astype(BF16)
        u = both(uf[:, rf, :], ur[:, rr, :]).astype(F32)
        v_new = u - lax.dot_general(both(wf[:, rf, :], wr[:, rr, :]), sb, nn_batched,
                                    preferred_element_type=F32)
        vb = v_new.astype(BF16)
        o = (lax.dot_general(both(qf[:, rf, :], qr[:, rr, :]), sb, nn_batched,
                             preferred_element_type=F32)
             + lax.dot_general(both(af[:, rf, :], ar[:, rr, :]), vb, nn_batched,
                               preferred_element_type=F32))
        dec = both(df[cc, 0:B_HEADS], dr[cr, B_HEADS:2 * B_HEADS])
        s_scr[...] = s * dec + lax.dot_general(both(kf[:, cc], kr[:, cr]), vb, nn_batched,
                                               preferred_element_type=F32)
        for h in range(B_HEADS):
            hs = slice(h * B_HEAD_DIM, (h + 1) * B_HEAD_DIM)
            of_ref[rf, hs] = o[h].astype(BF16)
            or_ref[rr, hs] = o[B_HEADS + h].astype(BF16)


def _gdn_rec(u, w, qe, ket, att, dec, chunk):
    bsz, _, _, t, _ = u.shape
    nt = t // TM
    nck = TM // chunk

    def tile(d, s):
        return s if d == 0 else jnp.where(s == 0, 0, nt - s)

    in_specs = []
    for d in range(2):
        for _ in range(3):
            in_specs.append(pl.BlockSpec((None, None, B_HEADS, TM, B_HEAD_DIM),
                                         lambda b, s, d=d: (b, d, 0, tile(d, s), 0)))
        in_specs.append(pl.BlockSpec((None, None, B_HEADS, nck, B_HEAD_DIM, chunk),
                                     lambda b, s, d=d: (b, d, 0, tile(d, s), 0, 0)))
        in_specs.append(pl.BlockSpec((None, None, B_HEADS, TM, chunk),
                                     lambda b, s, d=d: (b, d, 0, tile(d, s), 0)))
        in_specs.append(pl.BlockSpec((None, nck, 2 * B_HEADS, 1, LANES),
                                     lambda b, s, d=d: (b, tile(d, s), 0, 0, 0)))
    return pl.pallas_call(
        functools.partial(_gdn_rec_kernel, chunk=chunk),
        grid=(bsz, nt),
        in_specs=in_specs,
        out_specs=[pl.BlockSpec((None, TM, B_WIDTH), lambda b, s, d=d: (b, tile(d, s), 0))
                   for d in range(2)],
        out_shape=[jax.ShapeDtypeStruct((bsz, t, B_WIDTH), BF16)] * 2,
        scratch_shapes=[pltpu.VMEM((2 * B_HEADS, B_HEAD_DIM, B_HEAD_DIM), F32)],
        compiler_params=_cparams(2),
        name="gdn_rec",
    )(u, w, qe, ket, att, dec, u, w, qe, ket, att, dec)


def _route(logits_t, bias, sel_scr):
    aff = _sigmoid(logits_t)
    sel_scr[...] = aff + bias
    n = logits_t.shape[1]
    sel = [sel_scr[e:e + 1, :] for e in range(N_EXPERTS)]
    affr = [aff[e:e + 1, :] for e in range(N_EXPERTS)]
    in_top2 = []
    gscore = []
    for g in range(N_GROUPS):
        ids = range(g * PER_GROUP, (g + 1) * PER_GROUP)
        gs = jnp.zeros((1, n), F32)
        for e in ids:
            rank = jnp.zeros((1, n), F32)
            for e2 in ids:
                if e2 == e:
                    continue
                ahead = (sel[e2] >= sel[e]) if e2 < e else (sel[e2] > sel[e])
                rank = rank + jnp.where(ahead, 1.0, 0.0)
            top = rank < 2.0
            in_top2.append(top)
            gs = gs + jnp.where(top, sel[e], 0.0)
        gscore.append(gs)
    best = jnp.zeros((1, n), jnp.int32)
    bestv = gscore[0]
    for g in range(1, N_GROUPS):
        better = gscore[g] > bestv
        best = jnp.where(better, g, best)
        bestv = jnp.where(better, gscore[g], bestv)
    chosen = [in_top2[e] & (best == e // PER_GROUP) for e in range(N_EXPERTS)]
    denom = jnp.zeros((1, n), F32)
    for e in range(N_EXPERTS):
        denom = denom + jnp.where(chosen[e], affr[e], 0.0)
    lo = jnp.full((1, n), float(PER_GROUP), F32)
    hi = jnp.zeros((1, n), F32)
    w_lo = jnp.zeros((1, n), F32)
    w_hi = jnp.zeros((1, n), F32)
    for e in reversed(range(N_EXPERTS)):
        pos = float(e % PER_GROUP)
        lo = jnp.where(chosen[e], jnp.minimum(lo, pos), lo)
    for e in range(N_EXPERTS):
        pos = float(e % PER_GROUP)
        hi = jnp.where(chosen[e], jnp.maximum(hi, pos), hi)
    for e in range(N_EXPERTS):
        pos = float(e % PER_GROUP)
        gate = affr[e] / denom
        w_lo = jnp.where(chosen[e] & (lo == pos), gate, w_lo)
        w_hi = jnp.where(chosen[e] & (hi == pos), gate, w_hi)
    pair = lo * (2 * PER_GROUP - 1 - lo) * 0.5 + hi - lo - 1.0
    cls = best.astype(F32) * float(N_PAIRS) + pair
    return cls, w_lo, w_hi


def _outproj_even_kernel(ya_ref, of_ref, or_ref, gate_ref, bn_ref, c_ref, x_ref, *rest):
    xres = jnp.where(pl.program_id(1) == 0, c_ref[...], x_ref[...])
    parts = []
    for h in range(B_HEADS):
        hs = slice(h * B_HEAD_DIM, (h + 1) * B_HEAD_DIM)
        o = of_ref[:, hs].astype(F32) + or_ref[:, hs].astype(F32)
        y = o * lax.rsqrt(jnp.mean(o * o, axis=-1, keepdims=True) + RMS_EPS) * bn_ref[...]
        parts.append((y * _silu(gate_ref[:, hs].astype(F32))).astype(BF16))
    _outproj_body(ya_ref[...], jnp.concatenate(parts, axis=1), xres, *rest)


def _outproj_kernel(ya_ref, yb_ref, x_ref, *rest):
    _outproj_body(ya_ref[...], yb_ref[...], x_ref[...], *rest)


def _outproj_body(ya, yb, xres, mod_ref, w_ref, lng_ref, lnb_ref, rwt_ref, rb_ref,
                  xo_ref, h_ref, info_ref, cnt_ref, sel_scr, run_scr, wt_scr):
    mod = mod_ref[...]
    wa = w_ref[0:ya.shape[1], :]
    wb = w_ref[ya.shape[1]:, :]
    y = (jnp.dot(ya, wa, preferred_element_type=F32)
         + jnp.dot(yb, wb, preferred_element_type=F32))
    xn = _layer_norm(DEEPNORM_ALPHA * xres + mod[2:3] * y, lng_ref[...], lnb_ref[...])
    xo_ref[...] = xn
    h = xn * (1.0 + mod[4:5]) + mod[3:4]
    d = h.shape[1]
    h_ref[:, 0:d] = h
    logits_t = lax.dot_general(rwt_ref[...], h, (((1,), (1,)), ((), ())),
                               precision=HIGHEST, preferred_element_type=F32)
    cls, w_lo, w_hi = _route(logits_t, rb_ref[...], sel_scr)

    @pl.when(jnp.logical_and(pl.program_id(0) == 0, pl.program_id(1) == 0))
    def _():
        run_scr[...] = jnp.zeros_like(run_scr)

    n = cls.shape[1]
    crow = lax.broadcasted_iota(jnp.int32, (CLS_PAD, n), 0).astype(F32)
    onehot = jnp.where(crow == cls, 1.0, 0.0)
    si = lax.broadcasted_iota(jnp.int32, (n, n), 0)
    ti = lax.broadcasted_iota(jnp.int32, (n, n), 1)
    before = jnp.where(si < ti, 1.0, 0.0).astype(BF16)
    cum = jnp.dot(onehot.astype(BF16), before, preferred_element_type=F32)
    run = run_scr[...]
    rank = jnp.sum(onehot * (cum + run[:, 0:1]), axis=0, keepdims=True)
    run = run + jnp.sum(onehot, axis=1, keepdims=True)
    run_scr[...] = run
    cnt_ref[...] = run
    info_ref[...] = jnp.zeros_like(info_ref)
    info_ref[0:1, :] = cls
    info_ref[1:2, :] = rank
    wt_scr[...] = jnp.zeros_like(wt_scr)
    wt_scr[0:1, :] = w_lo
    wt_scr[1:2, :] = w_hi
    h_ref[:, d:] = wt_scr[...].T


def _outproj(ya, yb, xres, modarr, w_out, ln_g, ln_b, rwt, rbias, row_blk0, gdn=None):
    bsz, n, wa = ya.shape

    def tok(width, col_blk=0):
        return pl.BlockSpec((None, TM, width), lambda b, j: (b, j, col_blk))

    if gdn is None:
        d = xres.shape[2]
        body, mix_args = _outproj_kernel, (ya, yb, xres)
        mix_specs = [tok(wa), tok(yb.shape[2]),
                     pl.BlockSpec((None, TM, d), lambda b, j: (b, j + row_blk0, 0))]
    else:
        o_fwd, o_rev, p, b_norm = gdn
        ctx, x = xres
        d = x.shape[2]
        gate_blk = (3 * A_WIDTH + 3 * B_WIDTH) // B_WIDTH
        body, mix_args = _outproj_even_kernel, (ya, o_fwd, o_rev, p, b_norm, ctx, x)
        mix_specs = [tok(wa), tok(B_WIDTH), tok(B_WIDTH), tok(B_WIDTH, gate_blk),
                     pl.BlockSpec((1, B_HEAD_DIM), lambda b, j: (0, 0)),
                     pl.BlockSpec((None, TM, d), lambda b, j: (b, 0, 0)),
                     pl.BlockSpec((None, TM, d), lambda b, j: (b, jnp.maximum(j - 1, 0), 0))]
    return pl.pallas_call(
        body,
        grid=(bsz, n // TM),
        in_specs=mix_specs + [
            pl.BlockSpec((None, None, 6, d), lambda b, j: (b, jnp.minimum(j + row_blk0, 1), 0, 0)),
            pl.BlockSpec(w_out.shape, lambda b, j: (0, 0)),
            pl.BlockSpec((1, d), lambda b, j: (0, 0)),
            pl.BlockSpec((1, d), lambda b, j: (0, 0)),
            pl.BlockSpec((N_EXPERTS, d), lambda b, j: (0, 0)),
            pl.BlockSpec((N_EXPERTS, 1), lambda b, j: (0, 0)),
        ],
        out_specs=[
            pl.BlockSpec((None, TM, d), lambda b, j: (b, j, 0)),
            pl.BlockSpec((None, TM, d + LANES), lambda b, j: (b, j, 0)),
            pl.BlockSpec((None, None, 8, TM), lambda b, j: (b, j, 0, 0)),
            pl.BlockSpec((CLS_PAD, LANES), lambda b, j: (0, 0)),
        ],
        out_shape=[
            jax.ShapeDtypeStruct((bsz, n, d), F32),
            jax.ShapeDtypeStruct((bsz, n, d + LANES), F32),
            jax.ShapeDtypeStruct((bsz, n // TM, 8, TM), F32),
            jax.ShapeDtypeStruct((CLS_PAD, LANES), F32),
        ],
        scratch_shapes=[pltpu.VMEM((N_EXPERTS, TM), F32), pltpu.VMEM((CLS_PAD, LANES), F32),
                        pltpu.VMEM((LANES, TM), F32)],
        compiler_params=_cparams(2),
        name="outproj",
    )(*mix_args, modarr, w_out, ln_g, ln_b, rwt, rbias)


SCATTER_ROWS = 512


def _row_scatter_kernel(dst_ref, src_ref, init_hbm, out_hbm, sem):
    del init_hbm
    rows = src_ref.shape[0]

    def body(r, carry):
        pltpu.make_async_copy(src_ref.at[pl.ds(r, 1)], out_hbm.at[pl.ds(dst_ref[0, r], 1)],
                              sem).start()
        return carry

    lax.fori_loop(0, rows, body, 0, unroll=8)
    pltpu.make_async_copy(src_ref, out_hbm.at[pl.ds(0, rows)], sem).wait()


def _row_scatter(src, dest, n_out):
    n, width = src.shape
    rows = math.gcd(n, SCATTER_ROWS)
    return pl.pallas_call(
        _row_scatter_kernel,
        grid=(n // rows,),
        in_specs=[
            pl.BlockSpec((None, 1, rows), lambda j: (j, 0, 0), memory_space=pltpu.SMEM),
            pl.BlockSpec((rows, width), lambda j: (j, 0)),
            pl.BlockSpec(memory_space=pl.ANY),
        ],
        out_specs=pl.BlockSpec(memory_space=pl.ANY),
        out_shape=jax.ShapeDtypeStruct((n_out, width), src.dtype),
        scratch_shapes=[pltpu.SemaphoreType.DMA(())],
        input_output_aliases={2: 0},
        compiler_params=_cparams(1),
        name="row_scatter",
    )(dest.reshape(n // rows, 1, rows), src, jnp.zeros((n_out, width), src.dtype))


def _moe_kernel(elo_ref, ehi_ref, nused_ref, x_ref, wg_lo, wu_lo, wd_lo, wg_hi, wu_hi, wd_hi, o_ref):
    used = pl.program_id(0) < nused_ref[0]
    d = o_ref.shape[1]

    @pl.when(used)
    def _():
        x = x_ref[:, 0:d].astype(BF16)
        acc = None
        for col, (wg, wu, wd) in enumerate(((wg_lo, wu_lo, wd_lo), (wg_hi, wu_hi, wd_hi))):
            gate = jnp.dot(x, wg[...], preferred_element_type=F32)
            up = jnp.dot(x, wu[...], preferred_element_type=F32)
            act = (_silu(gate) * up).astype(BF16)
            y = x_ref[:, d + col:d + col + 1] * jnp.dot(act, wd[...], preferred_element_type=F32)
            acc = y if acc is None else acc + y
        o_ref[...] = acc

    @pl.when(jnp.logical_not(used))
    def _():
        o_ref[...] = jnp.zeros_like(o_ref)


def _moe_plan(info, counts, n_tok):
    mt = MOE_TM
    n_tiles = n_tok // mt + N_CLASSES
    cls = info[:, :, 0, :].reshape(-1).astype(jnp.int32)
    rank = info[:, :, 1, :].reshape(-1).astype(jnp.int32)
    cnt = counts[:N_CLASSES, 0].astype(jnp.int32)
    padded = ((cnt + mt - 1) // mt) * mt
    ends = jnp.cumsum(padded)
    starts = ends - padded
    classes = jnp.arange(N_CLASSES, dtype=jnp.int32)
    dest = jnp.sum(jnp.where(cls[:, None] == classes[None, :], starts[None, :], 0), axis=1) + rank
    n_used = ends[-1] // mt
    tidx = jnp.arange(n_tiles, dtype=jnp.int32)
    tidx = jnp.minimum(tidx, n_used - 1)
    tcls = jnp.sum((ends[None, :] <= (tidx * mt)[:, None]).astype(jnp.int32), axis=1)
    tcls = jnp.minimum(tcls, N_CLASSES - 1)
    pairs = [(a, b) for a in range(PER_GROUP) for b in range(a + 1, PER_GROUP)]
    pair = tcls % N_PAIRS
    lo = sum(jnp.where(pair == k, a, 0) for k, (a, _) in enumerate(pairs))
    hi = sum(jnp.where(pair == k, b, 0) for k, (_, b) in enumerate(pairs))
    group = tcls // N_PAIRS
    return (group * PER_GROUP + lo, group * PER_GROUP + hi, n_used.reshape(1).astype(jnp.int32),
            dest.astype(jnp.int32))


def _moe(h_ext, info, counts, wg, wu, wd):
    n, width = h_ext.shape
    d = width - LANES
    ne, _, de = wg.shape
    mt = MOE_TM
    elo, ehi, n_used, dest = _moe_plan(info, counts, n)
    n_tiles = n // mt + N_CLASSES
    h_sorted = _row_scatter(h_ext, dest, n_tiles * mt)

    def expert(which, shape):
        if which == 0:
            return pl.BlockSpec((None,) + shape, lambda i, lo, hi, nu: (lo[i], 0, 0))
        return pl.BlockSpec((None,) + shape, lambda i, lo, hi, nu: (hi[i], 0, 0))

    grid_spec = pltpu.PrefetchScalarGridSpec(
        num_scalar_prefetch=3,
        grid=(n_tiles,),
        in_specs=[
            pl.BlockSpec((mt, width), lambda i, lo, hi, nu: (jnp.minimum(i, nu[0] - 1), 0)),
            expert(0, (d, de)), expert(0, (d, de)), expert(0, (de, d)),
            expert(1, (d, de)), expert(1, (d, de)), expert(1, (de, d)),
        ],
        out_specs=pl.BlockSpec((mt, d), lambda i, *_: (i, 0)),
    )
    f_sorted = pl.pallas_call(
        _moe_kernel,
        grid_spec=grid_spec,
        out_shape=jax.ShapeDtypeStruct((n_tiles * mt, d), F32),
        compiler_params=_cparams(1),
        name="moe",
    )(elo, ehi, n_used, h_sorted, wg, wu, wd, wg, wu, wd)
    return f_sorted, dest


def _ln2_kernel(cur_ref, nxt_ref, x_ref, f_hbm, mod_ref, lng_ref, lnb_ref, o_ref, fbuf, sem):
    nj = pl.num_programs(1)
    step = pl.program_id(0) * nj + pl.program_id(1)
    n_steps = pl.num_programs(0) * nj
    slot = step % 2
    rows = fbuf.shape[1]

    def gather_start(idx_ref, s):
        def body(r, carry):
            pltpu.make_async_copy(f_hbm.at[pl.ds(idx_ref[0, r], 1)], fbuf.at[s, pl.ds(r, 1)],
                                  sem.at[s]).start()
            return carry
        lax.fori_loop(0, rows, body, 0, unroll=8)

    @pl.when(step == 0)
    def _():
        gather_start(cur_ref, 0)

    @pl.when(step + 1 < n_steps)
    def _():
        gather_start(nxt_ref, 1 - slot)

    pltpu.make_async_copy(f_hbm.at[pl.ds(0, rows)], fbuf.at[slot], sem.at[slot]).wait()
    mod = mod_ref[...]
    v = DEEPNORM_ALPHA * x_ref[...] + mod[5:6] * fbuf[slot]
    o_ref[...] = _layer_norm(v, lng_ref[...], lnb_ref[...])


def _ln2(x, f_sorted, dest, modarr, ln_g, ln_b, kind0):
    bsz, n, d = x.shape
    nj = n // TM
    n_steps = bsz * nj

    def idx_rows(offset):
        return pl.BlockSpec((None, 1, TM), lambda b, j: (jnp.minimum(b * nj + j + offset, n_steps - 1), 0, 0),
                            memory_space=pltpu.SMEM)

    dest3 = dest.reshape(n_steps, 1, TM)
    return pl.pallas_call(
        _ln2_kernel,
        grid=(bsz, nj),
        in_specs=[
            idx_rows(0), idx_rows(1),
            pl.BlockSpec((None, TM, d), lambda b, j: (b, j, 0)),
            pl.BlockSpec(memory_space=pl.ANY),
            pl.BlockSpec((None, None, 6, d), lambda b, j: (b, jnp.minimum(j + kind0, 1), 0, 0)),
            pl.BlockSpec((1, d), lambda b, j: (0, 0)),
            pl.BlockSpec((1, d), lambda b, j: (0, 0)),
        ],
        out_specs=pl.BlockSpec((None, TM, d), lambda b, j: (b, j, 0)),
        out_shape=jax.ShapeDtypeStruct((bsz, n, d), F32),
        scratch_shapes=[pltpu.VMEM((2, TM, d), F32), pltpu.SemaphoreType.DMA((2,))],
        compiler_params=_cparams(2),
        name="ln2",
    )(dest3, dest3, x, f_sorted, modarr, ln_g, ln_b)


def _rope(x, c, s1, s2, shift):
    w = x.shape[1]
    return x * c + pltpu.roll(x, w - shift, 1) * s1 + pltpu.roll(x, shift, 1) * s2


def _rms(x, g):
    return x * lax.rsqrt(jnp.mean(x * x, axis=-1, keepdims=True) + RMS_EPS) * g


def _inproj_odd_kernel(x_ref, mod_ref, w_ref, qn_ref, kvn_ref, wuq_ref, wk_ref, we_ref, wv_ref,
                       tw_ref, tq_ref, tk_ref, qw_ref, kw_ref, vw_ref, qm_ref, km_ref, vm_ref):
    j = pl.program_id(1)
    is_ctx = j == 0
    mod = mod_ref[...]
    h = (x_ref[...] * (1.0 + mod[1:2]) + mod[0:1]).astype(BF16)
    p = jnp.dot(h, w_ref[...], preferred_element_type=F32)

    def tables(t_ref):
        c = jnp.where(is_ctx, 1.0, t_ref[0])
        s1 = jnp.where(is_ctx, 0.0, t_ref[1])
        s2 = jnp.where(is_ctx, 0.0, t_ref[2])
        return c, s1, s2

    cw, s1w, s2w = tables(tw_ref)
    nq = C_Q_HEADS * C_HEAD_DIM
    for r in range(nq // LANES):
        blk = _rope(p[:, r * LANES:(r + 1) * LANES], cw, s1w, s2w, C_HEAD_DIM // 2)
        qw_ref[:, r * LANES:(r + 1) * LANES] = (blk * (C_HEAD_DIM ** -0.5)).astype(BF16)
    kw_ref[...] = _rope(p[:, nq:nq + LANES], cw, s1w, s2w, C_HEAD_DIM // 2).astype(BF16)
    vw_ref[...] = p[:, nq + LANES:nq + 2 * LANES].astype(BF16)

    o = nq + 2 * LANES
    dq = _rms(p[:, o:o + D_Q_RANK], qn_ref[...]).astype(BF16)
    o += D_Q_RANK
    dkv = _rms(p[:, o:o + D_KV_RANK], kvn_ref[...]).astype(BF16)
    o += D_KV_RANK
    cq, s1q, s2q = tables(tq_ref)
    ck, s1k, s2k = tables(tk_ref)
    krope = _rope(p[:, o:o + LANES], ck, s1k, s2k, D_ROPE // 2).astype(BF16)
    scale = (D_NOPE + D_ROPE) ** -0.5
    q_all = jnp.dot(dq, wuq_ref[...], preferred_element_type=F32)
    for hh in range(D_HEADS):
        sl = slice(hh * LANES, (hh + 1) * LANES)
        qm_ref[:, sl] = (_rope(q_all[:, sl], cq, s1q, s2q, D_ROPE // 2) * scale).astype(BF16)
    km_ref[...] = (jnp.dot(dkv, wk_ref[...], preferred_element_type=F32)
                   + jnp.dot(krope, we_ref[...], preferred_element_type=F32)).astype(BF16)
    vm_ref[...] = jnp.dot(dkv, wv_ref[...], preferred_element_type=F32).astype(BF16)


def _inproj_odd(xin, modarr, w1, qnorm, kvnorm, wuq, wk, we, wv, tab_w, tab_q, tab_k):
    bsz, t, d = xin.shape
    n1 = w1.shape[1]
    hw = D_HEADS * LANES

    def tab_spec():
        return pl.BlockSpec((3, TM, LANES), lambda b, j: (0, jnp.maximum(j - 1, 0), 0))

    def full(a):
        return pl.BlockSpec(a.shape, lambda b, j: (0,) * a.ndim)

    def out(width):
        return (pl.BlockSpec((None, TM, width), lambda b, j: (b, j, 0)),
                jax.ShapeDtypeStruct((bsz, t, width), BF16))

    def out_latent(width):
        return (pl.BlockSpec((None, TM, width), lambda b, j: (b, jnp.maximum(j - 1, 0), 0)),
                jax.ShapeDtypeStruct((bsz, t - TM, width), BF16))

    outs = (out_latent(C_Q_HEADS * C_HEAD_DIM), out(LANES), out(LANES), out_latent(hw), out(hw), out(hw))
    return pl.pallas_call(
        _inproj_odd_kernel,
        grid=(bsz, t // TM),
        in_specs=[
            pl.BlockSpec((None, TM, d), lambda b, j: (b, j, 0)),
            pl.BlockSpec((None, None, 6, d), lambda b, j: (b, jnp.minimum(j, 1), 0, 0)),
            full(w1), full(qnorm), full(kvnorm), full(wuq), full(wk), full(we), full(wv),
            tab_spec(), tab_spec(), tab_spec(),
        ],
        out_specs=[o[0] for o in outs],
        out_shape=[o[1] for o in outs],
        compiler_params=_cparams(2),
        name="inproj_odd",
    )(xin, modarr, w1, qnorm, kvnorm, wuq, wk, we, wv, tab_w, tab_q, tab_k)


WIN_TQ = 256


def _win_kernel(sink_ref, q_ref, k_ref, v_ref, o_ref, klo_scr, khi_scr, *, n_ctx):
    i = pl.program_id(1)
    wdw = C_WINDOW
    t = k_ref.shape[0]
    lane = lax.broadcasted_iota(jnp.int32, (t, LANES), 1)

    @pl.when(i == 0)
    def _():
        kk = k_ref[...]
        klo_scr[...] = jnp.where(lane < C_HEAD_DIM, kk, jnp.zeros_like(kk))
        khi_scr[...] = jnp.where(lane >= C_HEAD_DIM, kk, jnp.zeros_like(kk))

    tq = q_ref.shape[0]
    span = tq + 2 * wdw
    n_lat_blk = (t - n_ctx) // wdw
    blk0 = jnp.clip(i * (tq // wdw) - 1, 0, n_lat_blk - span // wdw)
    r0 = pl.multiple_of(n_ctx + blk0 * wdw, wdw)
    kpos = blk0 * wdw + lax.broadcasted_iota(jnp.int32, (tq, span), 1)
    qpos = i * tq + lax.broadcasted_iota(jnp.int32, (tq, span), 0)
    near = jnp.abs(kpos - qpos) <= wdw
    v_loc = v_ref[pl.ds(r0, span), :]
    v_ctx = v_ref[0:n_ctx, :]
    olane = lax.broadcasted_iota(jnp.int32, (tq, LANES), 1)
    n_rep = C_Q_HEADS // C_KV_HEADS
    for r in range(n_rep):
        q = q_ref[:, r * LANES:(r + 1) * LANES]
        outs = []
        for g, k_scr in enumerate((klo_scr, khi_scr)):
            k_loc = k_scr[pl.ds(r0, span), :]
            k_ctx = k_scr[0:n_ctx, :]
            s_loc = lax.dot_general(q, k_loc, (((1,), (1,)), ((), ())), preferred_element_type=F32)
            s_loc = jnp.where(near, s_loc, -jnp.inf)
            s_ctx = lax.dot_general(q, k_ctx, (((1,), (1,)), ((), ())), preferred_element_type=F32)
            sink = sink_ref[g * n_rep + r]
            m = jnp.maximum(jnp.maximum(jnp.max(s_loc, axis=-1, keepdims=True),
                                        jnp.max(s_ctx, axis=-1, keepdims=True)), sink)
            p_loc = jnp.exp(s_loc - m)
            p_ctx = jnp.exp(s_ctx - m)
            den = (jnp.sum(p_loc, axis=-1, keepdims=True) + jnp.sum(p_ctx, axis=-1, keepdims=True)
                   + jnp.exp(sink - m))
            pv = (jnp.dot(p_loc.astype(BF16), v_loc, preferred_element_type=F32)
                  + jnp.dot(p_ctx.astype(BF16), v_ctx, preferred_element_type=F32))
            outs.append(pv / den)
        o_ref[:, r * LANES:(r + 1) * LANES] = jnp.where(olane < C_HEAD_DIM, outs[0], outs[1]).astype(BF16)


def _win_attention(sink, qw, kw, vw, n_ctx):
    bsz, n_lat, nq = qw.shape
    t = n_ctx + n_lat
    grid_spec = pltpu.PrefetchScalarGridSpec(
        num_scalar_prefetch=1,
        grid=(bsz, n_lat // WIN_TQ),
        in_specs=[
            pl.BlockSpec((None, WIN_TQ, nq), lambda b, i, s: (b, i, 0)),
            pl.BlockSpec((None, t, LANES), lambda b, i, s: (b, 0, 0)),
            pl.BlockSpec((None, t, LANES), lambda b, i, s: (b, 0, 0)),
        ],
        out_specs=pl.BlockSpec((None, WIN_TQ, nq), lambda b, i, s: (b, i, 0)),
        scratch_shapes=[pltpu.VMEM((t, LANES), BF16), pltpu.VMEM((t, LANES), BF16)],
    )
    return pl.pallas_call(
        functools.partial(_win_kernel, n_ctx=n_ctx),
        grid_spec=grid_spec,
        out_shape=jax.ShapeDtypeStruct((bsz, n_lat, nq), BF16),
        compiler_params=_cparams(2),
        name="win_attention",
    )(sink, qw, kw, vw)


MLA_TQ = 512


def _mla_kernel(q_ref, k_ref, v_ref, o_ref):
    for hp in range(D_HEADS // 2):
        acc = None
        for hh in (2 * hp, 2 * hp + 1):
            sl = slice(hh * LANES, (hh + 1) * LANES)
            s = lax.dot_general(q_ref[:, sl], k_ref[:, sl], (((1,), (1,)), ((), ())),
                                preferred_element_type=F32)
            m = jnp.max(s, axis=-1, keepdims=True)
            p = jnp.exp(s - m)
            den = jnp.sum(p, axis=-1, keepdims=True)
            pv = jnp.dot(p.astype(BF16), v_ref[:, sl], preferred_element_type=F32) / den
            acc = pv if acc is None else acc + pv
        o_ref[:, hp * LANES:(hp + 1) * LANES] = acc.astype(BF16)


def _mla_attention(qm, km, vm, n_ctx):
    bsz, n_lat, hw = qm.shape
    t = n_ctx + n_lat
    ow = D_HEADS * D_V
    return pl.pallas_call(
        _mla_kernel,
        grid=(bsz, n_lat // MLA_TQ),
        in_specs=[
            pl.BlockSpec((None, MLA_TQ, hw), lambda b, i: (b, i, 0)),
            pl.BlockSpec((None, t, hw), lambda b, i: (b, 0, 0)),
            pl.BlockSpec((None, t, hw), lambda b, i: (b, 0, 0)),
        ],
        out_specs=pl.BlockSpec((None, MLA_TQ, ow), lambda b, i: (b, i, 0)),
        out_shape=jax.ShapeDtypeStruct((bsz, n_lat, ow), BF16),
        compiler_params=_cparams(2),
        name="mla_attention",
    )(qm, km, vm)


def _rope_tables(n_tokens, rot_dim, group, offset):
    t = jnp.arange(n_tokens)
    rows = (t // GRID_W).astype(F32)
    cols = (t % GRID_W).astype(F32)
    n_freq = rot_dim // 4
    inv_freq = ROPE_BASE ** (-jnp.arange(n_freq, dtype=F32) / n_freq)
    ang = jnp.concatenate([rows[:, None] * inv_freq, cols[:, None] * inv_freq], -1)
    cos, sin = jnp.cos(ang), jnp.sin(ang)
    half = rot_dim // 2
    c = jnp.ones((n_tokens, LANES), F32)
    s1 = jnp.zeros((n_tokens, LANES), F32)
    s2 = jnp.zeros((n_tokens, LANES), F32)
    for start in range(offset, LANES, group):
        c = c.at[:, start:start + half].set(cos).at[:, start + half:start + rot_dim].set(cos)
        s1 = s1.at[:, start:start + half].set(-sin)
        s2 = s2.at[:, start + half:start + rot_dim].set(sin)
    return jnp.stack([c, s1, s2])


def _odd_weights(w_in, wuq, wukv, w_out):
    d = w_in.shape[0]
    nq = C_Q_HEADS * C_HEAD_DIM
    nkv = C_KV_HEADS * C_HEAD_DIM
    n_rep = C_Q_HEADS // C_KV_HEADS
    order = [g * n_rep + r for r in range(n_rep) for g in range(C_KV_HEADS)]
    cq = w_in[:, :nq].reshape(d, C_Q_HEADS, C_HEAD_DIM)[:, order].reshape(d, nq)
    rest = w_in[:, nq:nq + 2 * nkv + D_Q_RANK + D_KV_RANK]
    krope = jnp.pad(w_in[:, nq + 2 * nkv + D_Q_RANK + D_KV_RANK:], ((0, 0), (0, LANES - D_ROPE)))
    w1 = jnp.concatenate([cq, rest, krope], axis=1).astype(BF16)
    qh = wuq.reshape(D_Q_RANK, D_HEADS, D_NOPE + D_ROPE)
    wuq_p = jnp.pad(qh, ((0, 0), (0, 0), (0, LANES - D_NOPE - D_ROPE))).reshape(D_Q_RANK, D_HEADS * LANES)
    kvh = wukv.reshape(D_KV_RANK, D_HEADS, D_NOPE + D_V)
    wk_p = jnp.pad(kvh[:, :, :D_NOPE], ((0, 0), (0, 0), (0, LANES - D_NOPE))).reshape(D_KV_RANK, D_HEADS * LANES)
    e_blk = jnp.zeros((LANES, LANES), F32).at[jnp.arange(D_ROPE), D_NOPE + jnp.arange(D_ROPE)].set(1.0)
    we = jnp.tile(e_blk, (1, D_HEADS))
    vh = kvh[:, :, D_NOPE:]
    even = (jnp.arange(D_HEADS) % 2 == 0)[None, :, None]
    wv_p = jnp.where(even, jnp.pad(vh, ((0, 0), (0, 0), (0, D_V))),
                     jnp.pad(vh, ((0, 0), (0, 0), (D_V, 0)))).reshape(D_KV_RANK, D_HEADS * LANES)
    wo_win = w_out[:nq].reshape(C_Q_HEADS, C_HEAD_DIM, -1)[jnp.array(order)].reshape(nq, -1)
    wo = jnp.concatenate([wo_win, w_out[nq:]], axis=0).astype(BF16)
    return w1, wuq_p.astype(BF16), wk_p.astype(BF16), we.astype(BF16), wv_p.astype(BF16), wo


def kernel(x, c, ctx, c_ctx, ada_w, ada_b, ln_g, ln_b, ev_w_in, ev_a_conv, ev_b_conv, ev_b_alog, ev_b_dtbias, ev_b_norm, ev_w_out, od_w_in, od_c_sink, od_d_qnorm, od_d_kvnorm, od_d_wuq, od_d_wukv, od_w_out, router_w, router_bias, moe_w_gate, moe_w_up, moe_w_down):
    bsz, n_lat, d = x.shape
    n_ctx = ctx.shape[1]
    assert n_ctx == TM and n_lat % TM == 0 and n_lat % GRID_W == 0
    assert ada_w.shape[0] == DEPTH and bsz + 1 <= 40
    t = n_ctx + n_lat

    cs = jnp.zeros((40, d), F32).at[:bsz].set(c).at[bsz].set(c_ctx)
    mods = _ada_mod(cs, ada_w, ada_b)

    def modarr(layer):
        m = mods[layer].reshape(40, 6, d)
        return jnp.stack([jnp.broadcast_to(m[bsz], (bsz, 6, d)), m[:bsz]], axis=1)

    rwt = router_w.T
    rbias = router_bias.reshape(N_EXPERTS, 1)

    mod0 = modarr(0)
    n_main = 3 * A_WIDTH + 4 * B_WIDTH
    w_main = ev_w_in[0][:, :n_main].astype(BF16)
    w_small = jnp.pad(ev_w_in[0][:, n_main:], ((0, 0), (0, LANES - 4 * B_HEADS))).astype(BF16)
    p, small = _inproj_even(ctx, x, mod0, w_main, w_small)
    alog_pad = jnp.zeros((1, LANES), F32).at[0, 8:16].set(ev_b_alog[0].reshape(-1))
    dtb_pad = jnp.zeros((1, LANES), F32).at[0, 8:16].set(ev_b_dtbias[0].reshape(-1))
    ya, u, w, qe, ket, att, dec = _even_prep(p, small, ev_a_conv[0], ev_b_conv[0], alog_pad, dtb_pad,
                                             GDN_CHUNK)
    o_fwd, o_rev = _gdn_rec(u, w, qe, ket, att, dec, GDN_CHUNK)
    x1, h1, info0, cnt0 = _outproj(ya, None, (ctx, x), mod0, ev_w_out[0].astype(BF16),
                             ln_g[0, 0].reshape(1, d), ln_b[0, 0].reshape(1, d), rwt, rbias, 0,
                             gdn=(o_fwd, o_rev, p, ev_b_norm[0].reshape(1, B_HEAD_DIM)))
    f, dest = _moe(h1.reshape(bsz * t, d + LANES), info0, cnt0,
                   moe_w_gate[0].astype(BF16), moe_w_up[0].astype(BF16), moe_w_down[0].astype(BF16))
    x2 = _ln2(x1, f, dest, mod0, ln_g[0, 1].reshape(1, d), ln_b[0, 1].reshape(1, d), 0)

    mod1 = modarr(1)
    w1, wuq_p, wk_p, we, wv_p, wo = _odd_weights(od_w_in[0], od_d_wuq[0], od_d_wukv[0], od_w_out[0])
    tab_w = _rope_tables(n_lat, C_HEAD_DIM, C_HEAD_DIM, 0)
    tab_q = _rope_tables(n_lat, D_ROPE, LANES, D_NOPE)
    tab_k = _rope_tables(n_lat, D_ROPE, LANES, 0)
    qw, kw, vw, qm, km, vm = _inproj_odd(
        x2, mod1, w1, od_d_qnorm[0].reshape(1, -1), od_d_kvnorm[0].reshape(1, -1),
        wuq_p, wk_p, we, wv_p, tab_w, tab_q, tab_k)
    y_win = _win_attention(od_c_sink[0], qw, kw, vw, n_ctx)
    y_mla = _mla_attention(qm, km, vm, n_ctx)
    x3, h3, info1, cnt1 = _outproj(y_win, y_mla, x2, mod1, wo, ln_g[1, 0].reshape(1, d),
                                   ln_b[1, 0].reshape(1, d), rwt, rbias, n_ctx // TM)
    f1, dest1 = _moe(h3.reshape(bsz * n_lat, d + LANES), info1, cnt1,
                     moe_w_gate[1].astype(BF16), moe_w_up[1].astype(BF16), moe_w_down[1].astype(BF16))
    return _ln2(x3, f1, dest1, mod1, ln_g[1, 1].reshape(1, d), ln_b[1, 1].reshape(1, d), 1)
```

```python
import functools
import math

import numpy as np
import jax
import jax.numpy as jnp
from jax import lax
from jax.experimental import pallas as pl
from jax.experimental.pallas import tpu as pltpu

F32 = jnp.float32
BF16 = jnp.bfloat16
HIGHEST = lax.Precision.HIGHEST

DEPTH = 2
GRID_W = 64
DEEPNORM_ALPHA = (2.0 * DEPTH) ** 0.25
LN_EPS = 1e-5
RMS_EPS = 1e-6
ROPE_BASE = 10000.0
B_HEADS = 4
B_HEAD_DIM = 128
B_WIDTH = 512
A_WIDTH = 512
C_Q_HEADS = 8
C_KV_HEADS = 2
C_HEAD_DIM = 64
C_WINDOW = 128
D_HEADS = 8
D_NOPE = 64
D_ROPE = 32
D_V = 64
D_Q_RANK = 384
D_KV_RANK = 256
N_EXPERTS = 16
N_GROUPS = 4
PER_GROUP = N_EXPERTS // N_GROUPS
D_EXPERT = 512
N_PAIRS = PER_GROUP * (PER_GROUP - 1) // 2
N_CLASSES = N_GROUPS * N_PAIRS
CLS_PAD = 32

LANES = 128
TM = 256
GDN_CHUNK = 64
MOE_TM = 256
VMEM_LIMIT = 56 * 1024 * 1024


def _cparams(n_axes, vmem=VMEM_LIMIT):
    return pltpu.CompilerParams(dimension_semantics=("arbitrary",) * n_axes, vmem_limit_bytes=vmem)


def _sigmoid(x):
    return 1.0 / (1.0 + jnp.exp(-x))


def _silu(x):
    return x * _sigmoid(x)


def _softplus(x):
    return jnp.maximum(x, 0.0) + jnp.log(1.0 + jnp.exp(-jnp.abs(x)))


def _layer_norm(v, g, b):
    mu = jnp.mean(v, axis=-1, keepdims=True)
    d = v - mu
    var = jnp.mean(d * d, axis=-1, keepdims=True)
    return d * lax.rsqrt(var + LN_EPS) * g + b


def _ada_kernel(c_ref, w_ref, b_ref, o_ref):
    s = _silu(c_ref[...])
    o_ref[...] = jnp.dot(s, w_ref[...], precision=HIGHEST, preferred_element_type=F32) + b_ref[...]


def _ada_mod(cs, ada_w, ada_b):
    depth, d, n6 = ada_w.shape
    rows = cs.shape[0]
    tn = 1536
    return pl.pallas_call(
        _ada_kernel,
        grid=(depth, n6 // tn),
        in_specs=[
            pl.BlockSpec((rows, d), lambda l, n: (0, 0)),
            pl.BlockSpec((None, d, tn), lambda l, n: (l, 0, n)),
            pl.BlockSpec((None, 1, tn), lambda l, n: (l, 0, n)),
        ],
        out_specs=pl.BlockSpec((None, rows, tn), lambda l, n: (l, 0, n)),
        out_shape=jax.ShapeDtypeStruct((depth, rows, n6), F32),
        compiler_params=_cparams(2),
        name="ada_mod",
    )(cs, ada_w, ada_b.reshape(depth, 1, n6))


def _inproj_even_kernel(c_ref, x_ref, mod_ref, wm_ref, ws_ref, p_ref, s_ref):
    mod = mod_ref[...]
    xv = jnp.where(pl.program_id(1) == 0, c_ref[...], x_ref[...])
    h = (xv * (1.0 + mod[1:2]) + mod[0:1]).astype(BF16)
    p_ref[...] = jnp.dot(h, wm_ref[...], preferred_element_type=F32).astype(BF16)
    s_ref[...] = jnp.dot(h, ws_ref[...], preferred_element_type=F32)


def _inproj_even(ctx, x, modarr, w_main, w_small):
    bsz, n_lat, d = x.shape
    t = ctx.shape[1] + n_lat
    nm = w_main.shape[1]
    return pl.pallas_call(
        _inproj_even_kernel,
        grid=(bsz, t // TM),
        in_specs=[
            pl.BlockSpec((None, TM, d), lambda b, j: (b, 0, 0)),
            pl.BlockSpec((None, TM, d), lambda b, j: (b, jnp.maximum(j - 1, 0), 0)),
            pl.BlockSpec((None, None, 6, d), lambda b, j: (b, jnp.minimum(j, 1), 0, 0)),
            pl.BlockSpec((d, nm), lambda b, j: (0, 0)),
            pl.BlockSpec((d, LANES), lambda b, j: (0, 0)),
        ],
        out_specs=[
            pl.BlockSpec((None, TM, nm), lambda b, j: (b, j, 0)),
            pl.BlockSpec((None, TM, LANES), lambda b, j: (b, j, 0)),
        ],
        out_shape=[
            jax.ShapeDtypeStruct((bsz, t, nm), BF16),
            jax.ShapeDtypeStruct((bsz, t, LANES), F32),
        ],
        compiler_params=_cparams(2),
        name="inproj_even",
    )(ctx, x, modarr, w_main, w_small)


HALO = 16


def _conv3(z, zp, zn, w):
    n = z.shape[0]
    rows = lax.broadcasted_iota(jnp.int32, z.shape, 0)
    zprev = jnp.where(rows == 0, zp, pltpu.roll(z, 1, 0))
    znext = jnp.where(rows == n - 1, zn, pltpu.roll(z, n - 1, 0))
    return w[0:1] * zprev + w[1:2] * z + w[2:3] * znext


def _even_prep_kernel(p_ref, pp_ref, pn_ref, s_ref, aw_ref, bw_ref, alog_ref, dtb_ref,
                      ya_ref, u_ref, w_ref, qe_ref, ket_ref, att_ref, dec_ref,
                      q_scr, k_scr, v_scr, *, chunk):
    j = pl.program_id(1)
    nj = pl.num_programs(1)
    prev_on = jnp.where(jnp.logical_and(j != 0, j != 1), 1.0, 0.0)
    next_on = jnp.where(jnp.logical_and(j != 0, j != nj - 1), 1.0, 0.0)
    prow = pp_ref[...].astype(F32)[HALO - 1:HALO] * prev_on
    nrow = pn_ref[...].astype(F32)[0:1] * next_on

    def seg(lo, hi):
        return p_ref[:, lo:hi].astype(F32), prow[:, lo:hi], nrow[:, lo:hi]

    a0, _, _ = seg(0, A_WIDTH)
    a1, a1p, a1n = seg(A_WIDTH, 2 * A_WIDTH)
    a2, a2p, a2n = seg(2 * A_WIDTH, 3 * A_WIDTH)
    ya_ref[...] = (a0 * _conv3(a1 * a2, a1p * a2p, a1n * a2n, aw_ref[...])).astype(BF16)

    base = 3 * A_WIDTH
    for which in range(3):
        lo = base + which * B_WIDTH
        z, zp, zn = seg(lo, lo + B_WIDTH)
        c = _silu(_conv3(z, zp, zn, bw_ref[:, which * B_WIDTH:(which + 1) * B_WIDTH]))
        for h in range(B_HEADS):
            ch = c[:, h * B_HEAD_DIM:(h + 1) * B_HEAD_DIM]
            if which < 2:
                ss = jnp.sum(ch * ch, axis=-1, keepdims=True)
                ch = ch * lax.rsqrt(ss + 1e-6)
                if which == 0:
                    ch = ch * (B_HEAD_DIM ** -0.5)
            sl = slice(h * B_HEAD_DIM, (h + 1) * B_HEAD_DIM)
            if which == 0:
                q_scr[:, sl] = ch.astype(BF16)
            elif which == 1:
                k_scr[:, sl] = ch.astype(BF16)
            else:
                v_scr[:, sl] = ch

    s = s_ref[...]
    beta = _sigmoid(s)
    g = -jnp.exp(alog_ref[...]) * _softplus(s + dtb_ref[...])
    n = s.shape[0]
    nck = n // chunk
    ri = lax.broadcasted_iota(jnp.int32, (n, n), 0)
    ci = lax.broadcasted_iota(jnp.int32, (n, n), 1)
    same = (ri // chunk) == (ci // chunk)
    m_fwd = jnp.where(jnp.logical_and(same, ci <= ri), 1.0, 0.0).astype(BF16)
    m_rev = jnp.where(jnp.logical_and(same, ci >= ri), 1.0, 0.0).astype(BF16)
    g1 = g.astype(BF16)
    r1 = g - g1.astype(F32)
    g2 = r1.astype(BF16)
    g3 = (r1 - g2.astype(F32)).astype(BF16)
    g_split = jnp.concatenate([g1, g2, g3], axis=1)

    def cumulate(m):
        parts = jnp.dot(m, g_split, preferred_element_type=F32)
        return parts[:, 0:LANES] + parts[:, LANES:2 * LANES] + parts[:, 2 * LANES:]

    gc_f = cumulate(m_fwd)
    gc_r = cumulate(m_rev)
    tot = gc_f + gc_r - g
    lane = lax.broadcasted_iota(jnp.int32, s.shape, 1)
    gc = jnp.where(lane >= 8 + B_HEADS, gc_r, gc_f)
    e_gc = jnp.exp(gc)
    e_rest = jnp.exp(tot - gc)
    gct = gc.T
    e_tot = jnp.exp(tot)

    ri = lax.broadcasted_iota(jnp.int32, (1, chunk, chunk), 1)
    ci = lax.broadcasted_iota(jnp.int32, (1, chunk, chunk), 2)
    eye = jnp.where(ri == ci, 1.0, 0.0)
    n_sq = int(np.log2(chunk))
    nt_batched = (((2,), (2,)), ((0,), (0,)))
    nn_batched = (((2,), (1,)), ((0,), (0,)))
    kk, qk, kf, qf = [], [], [], []
    for h in range(B_HEADS):
        hs = slice(h * B_HEAD_DIM, (h + 1) * B_HEAD_DIM)
        k3 = k_scr[:, hs].reshape(nck, chunk, B_HEAD_DIM)
        q3 = q_scr[:, hs].reshape(nck, chunk, B_HEAD_DIM)
        kk.append(lax.dot_general(k3, k3, nt_batched, preferred_element_type=F32))
        qk.append(lax.dot_general(q3, k3, nt_batched, preferred_element_type=F32))
        kf.append(k_scr[:, hs].astype(F32))
        qf.append(q_scr[:, hs].astype(F32))
    a_all, rhs_all = [], []
    for d in range(2):
        incl = (ci <= ri) if d == 0 else (ci >= ri)
        strict = (ci < ri) if d == 0 else (ci > ri)
        for h in range(B_HEADS):
            chain = d * B_HEADS + h
            hs = slice(h * B_HEAD_DIM, (h + 1) * B_HEAD_DIM)
            bcol = beta[:, chain:chain + 1]
            e1 = e_gc[:, 8 + chain:9 + chain]
            e2 = e_rest[:, 8 + chain:9 + chain]
            gcol = gc[:, 8 + chain:9 + chain].reshape(nck, chunk, 1)
            grow = jnp.stack([gct[8 + chain:9 + chain, cc * chunk:(cc + 1) * chunk]
                              for cc in range(nck)], axis=0)
            decay = jnp.exp(jnp.where(incl, gcol - grow, -jnp.inf))
            a_all.append(jnp.where(strict, bcol.reshape(nck, chunk, 1) * kk[h] * decay, 0.0))
            att_ref[d, h] = (qk[h] * decay).reshape(n, chunk).astype(BF16)
            qe_ref[d, h] = (qf[h] * e1).astype(BF16)
            ket = (kf[h] * e2).T
            for cc in range(nck):
                ket_ref[d, h, cc] = ket[:, cc * chunk:(cc + 1) * chunk].astype(BF16)
                dec_ref[cc, chain] = jnp.broadcast_to(
                    e_tot[cc * chunk:cc * chunk + 1, 8 + chain:9 + chain], (1, LANES))
            rhs = jnp.concatenate([(v_scr[:, hs] * bcol).astype(BF16),
                                   (kf[h] * (bcol * e1)).astype(BF16)], axis=1)
            rhs_all.append(rhs.reshape(nck, chunk, 2 * B_HEAD_DIM))
    npow = -jnp.concatenate(a_all, axis=0)
    tinv = eye + npow
    for _ in range(n_sq - 1):
        nb = npow.astype(BF16)
        npow = lax.dot_general(nb, nb, nn_batched, preferred_element_type=F32)
        tinv = tinv + lax.dot_general(tinv.astype(BF16), npow.astype(BF16), nn_batched,
                                      preferred_element_type=F32)
    uw = lax.dot_general(tinv.astype(BF16), jnp.concatenate(rhs_all, axis=0), nn_batched,
                         preferred_element_type=F32)
    for d in range(2):
        for h in range(B_HEADS):
            blk = uw[(d * B_HEADS + h) * nck:(d * B_HEADS + h + 1) * nck]
            u_ref[d, h] = blk[:, :, :B_HEAD_DIM].reshape(n, B_HEAD_DIM).astype(BF16)
            w_ref[d, h] = blk[:, :, B_HEAD_DIM:].reshape(n, B_HEAD_DIM).astype(BF16)


def _even_prep(p, small, a_conv, b_conv, alog_pad, dtb_pad, chunk):
    bsz, t, nm = p.shape
    nhb = TM // HALO
    last_hb = t // HALO - 1
    nck = TM // chunk
    nc = t // chunk

    def per_dir():
        return (pl.BlockSpec((None, 2, B_HEADS, TM, B_HEAD_DIM), lambda b, j: (b, 0, 0, j, 0)),
                jax.ShapeDtypeStruct((bsz, 2, B_HEADS, t, B_HEAD_DIM), BF16))

    outs = [
        (pl.BlockSpec((None, TM, A_WIDTH), lambda b, j: (b, j, 0)),
         jax.ShapeDtypeStruct((bsz, t, A_WIDTH), BF16)),
        per_dir(), per_dir(), per_dir(),
        (pl.BlockSpec((None, 2, B_HEADS, nck, B_HEAD_DIM, chunk), lambda b, j: (b, 0, 0, j, 0, 0)),
         jax.ShapeDtypeStruct((bsz, 2, B_HEADS, nc, B_HEAD_DIM, chunk), BF16)),
        (pl.BlockSpec((None, 2, B_HEADS, TM, chunk), lambda b, j: (b, 0, 0, j, 0)),
         jax.ShapeDtypeStruct((bsz, 2, B_HEADS, t, chunk), BF16)),
        (pl.BlockSpec((None, nck, 2 * B_HEADS, 1, LANES), lambda b, j: (b, j, 0, 0, 0)),
         jax.ShapeDtypeStruct((bsz, nc, 2 * B_HEADS, 1, LANES), F32)),
    ]
    return pl.pallas_call(
        functools.partial(_even_prep_kernel, chunk=chunk),
        grid=(bsz, t // TM),
        in_specs=[
            pl.BlockSpec((None, TM, nm), lambda b, j: (b, j, 0)),
            pl.BlockSpec((None, HALO, nm), lambda b, j: (b, jnp.maximum(j * nhb - 1, 0), 0)),
            pl.BlockSpec((None, HALO, nm), lambda b, j: (b, jnp.minimum((j + 1) * nhb, last_hb), 0)),
            pl.BlockSpec((None, TM, LANES), lambda b, j: (b, j, 0)),
            pl.BlockSpec((3, A_WIDTH), lambda b, j: (0, 0)),
            pl.BlockSpec((3, 3 * B_WIDTH), lambda b, j: (0, 0)),
            pl.BlockSpec((1, LANES), lambda b, j: (0, 0)),
            pl.BlockSpec((1, LANES), lambda b, j: (0, 0)),
        ],
        out_specs=[o[0] for o in outs],
        out_shape=[o[1] for o in outs],
        scratch_shapes=[pltpu.VMEM((TM, B_WIDTH), BF16), pltpu.VMEM((TM, B_WIDTH), BF16),
                        pltpu.VMEM((TM, B_WIDTH), F32)],
        compiler_params=_cparams(2),
        name="even_prep",
    )(p, p, p, small, a_conv, b_conv, alog_pad, dtb_pad)


def _gdn_rec_kernel(uf, wf, qf, kf, af, df, ur, wr, qr, kr, ar, dr, of_ref, or_ref, s_scr, *, chunk):
    @pl.when(pl.program_id(1) == 0)
    def _():
        s_scr[...] = jnp.zeros_like(s_scr)

    nck = uf.shape[1] // chunk
    nn_batched = (((2,), (1,)), ((0,), (0,)))

    def both(fwd, rev):
        return jnp.concatenate([fwd, rev], axis=0)

    for cc in range(nck):
        cr = nck - 1 - cc
        rf = slice(cc * chunk, (cc + 1) * chunk)
        rr = slice(cr * chunk, (cr + 1) * chunk)
        s = s_scr[...]
        sb = s.astype(BF16)
        u = both(uf[:, rf, :], ur[:, rr, :]).astype(F32)
        v_new = u - lax.dot_general(both(wf[:, rf, :], wr[:, rr, :]), sb, nn_batched,
                                    preferred_element_type=F32)
        vb = v_new.astype(BF16)
        o = (lax.dot_general(both(qf[:, rf, :], qr[:, rr, :]), sb, nn_batched,
                             preferred_element_type=F32)
             + lax.dot_general(both(af[:, rf, :], ar[:, rr, :]), vb, nn_batched,
                               preferred_element_type=F32))
        dec = both(df[cc, 0:B_HEADS], dr[cr, B_HEADS:2 * B_HEADS])
        s_scr[...] = s * dec + lax.dot_general(both(kf[:, cc], kr[:, cr]), vb, nn_batched,
                                               preferred_element_type=F32)
        for h in range(B_HEADS):
            hs = slice(h * B_HEAD_DIM, (h + 1) * B_HEAD_DIM)
            of_ref[rf, hs] = o[h].astype(BF16)
            or_ref[rr, hs] = o[B_HEADS + h].astype(BF16)


def _gdn_rec(u, w, qe, ket, att, dec, chunk):
    bsz, _, _, t, _ = u.shape
    nt = t // TM
    nck = TM // chunk

    def tile(d, s):
        return s if d == 0 else jnp.where(s == 0, 0, nt - s)

    in_specs = []
    for d in range(2):
        for _ in range(3):
            in_specs.append(pl.BlockSpec((None, None, B_HEADS, TM, B_HEAD_DIM),
                                         lambda b, s, d=d: (b, d, 0, tile(d, s), 0)))
        in_specs.append(pl.BlockSpec((None, None, B_HEADS, nck, B_HEAD_DIM, chunk),
                                     lambda b, s, d=d: (b, d, 0, tile(d, s), 0, 0)))
        in_specs.append(pl.BlockSpec((None, None, B_HEADS, TM, chunk),
                                     lambda b, s, d=d: (b, d, 0, tile(d, s), 0)))
        in_specs.append(pl.BlockSpec((None, nck, 2 * B_HEADS, 1, LANES),
                                     lambda b, s, d=d: (b, tile(d, s), 0, 0, 0)))
    return pl.pallas_call(
        functools.partial(_gdn_rec_kernel, chunk=chunk),
        grid=(bsz, nt),
        in_specs=in_specs,
        out_specs=[pl.BlockSpec((None, TM, B_WIDTH), lambda b, s, d=d: (b, tile(d, s), 0))
                   for d in range(2)],
        out_shape=[jax.ShapeDtypeStruct((bsz, t, B_WIDTH), BF16)] * 2,
        scratch_shapes=[pltpu.VMEM((2 * B_HEADS, B_HEAD_DIM, B_HEAD_DIM), F32)],
        compiler_params=_cparams(2),
        name="gdn_rec",
    )(u, w, qe, ket, att, dec, u, w, qe, ket, att, dec)


def _route(logits_t, bias):
    aff = _sigmoid(logits_t)
    sel = aff + bias
    e_idx = lax.broadcasted_iota(jnp.int32, sel.shape, 0)
    pos = e_idx % PER_GROUP
    grp = e_idx // PER_GROUP

    def group_rot(v, k):
        return jnp.where(pos + k < PER_GROUP, pltpu.roll(v, N_EXPERTS - k, 0),
                         pltpu.roll(v, PER_GROUP - k, 0))

    rank = jnp.zeros(sel.shape, F32)
    for k in range(1, PER_GROUP):
        other = group_rot(sel, k)
        rank = rank + jnp.where(pos + k >= PER_GROUP, jnp.where(other >= sel, 1.0, 0.0),
                                jnp.where(other > sel, 1.0, 0.0))
    top = rank < 2.0
    gsum = jnp.where(top, sel, 0.0)
    gs = gsum
    for k in range(1, PER_GROUP):
        gs = gs + group_rot(gsum, k)
    beaten = jnp.zeros(sel.shape, F32)
    for m in range(1, N_GROUPS):
        other = pltpu.roll(gs, PER_GROUP * m, 0)
        beaten = beaten + jnp.where(grp >= m, jnp.where(gs > other, 0.0, 1.0),
                                    jnp.where(gs >= other, 0.0, 1.0))
    chosen = jnp.where(top, beaten, 1.0) < 0.5
    denom = jnp.sum(jnp.where(chosen, aff, 0.0), axis=0, keepdims=True)
    gate = jnp.where(chosen, aff / denom, 0.0)
    posf = pos.astype(F32)
    lo = jnp.min(jnp.where(chosen, posf, float(PER_GROUP)), axis=0, keepdims=True)
    hi = jnp.max(jnp.where(chosen, posf, -1.0), axis=0, keepdims=True)
    w_lo = jnp.sum(jnp.where(posf == lo, gate, 0.0), axis=0, keepdims=True)
    w_hi = jnp.sum(jnp.where(posf == hi, gate, 0.0), axis=0, keepdims=True)
    group = jnp.max(jnp.where(chosen, grp.astype(F32), 0.0), axis=0, keepdims=True)
    pair = lo * (2 * PER_GROUP - 1 - lo) * 0.5 + hi - lo - 1.0
    return group * float(N_PAIRS) + pair, w_lo, w_hi


def _outproj_even_kernel(ya_ref, of_ref, or_ref, gate_ref, bn_ref, c_ref, x_ref, *rest):
    xres = jnp.where(pl.program_id(1) == 0, c_ref[...], x_ref[...])
    parts = []
    for h in range(B_HEADS):
        hs = slice(h * B_HEAD_DIM, (h + 1) * B_HEAD_DIM)
        o = of_ref[:, hs].astype(F32) + or_ref[:, hs].astype(F32)
        y = o * lax.rsqrt(jnp.mean(o * o, axis=-1, keepdims=True) + RMS_EPS) * bn_ref[...]
        parts.append((y * _silu(gate_ref[:, hs].astype(F32))).astype(BF16))
    _outproj_body(ya_ref[...], jnp.concatenate(parts, axis=1), xres, *rest)


def _outproj_kernel(ya_ref, yb_ref, x_ref, *rest):
    _outproj_body(ya_ref[...], yb_ref[...], x_ref[...], *rest)


def _outproj_body(ya, yb, xres, mod_ref, w_ref, lng_ref, lnb_ref, rwt_ref, rb_ref,
                  xo_ref, h_ref, info_ref, cnt_ref, run_scr, wt_scr):
    mod = mod_ref[...]
    wa = w_ref[0:ya.shape[1], :]
    wb = w_ref[ya.shape[1]:, :]
    y = (jnp.dot(ya, wa, preferred_element_type=F32)
         + jnp.dot(yb, wb, preferred_element_type=F32))
    xn = _layer_norm(DEEPNORM_ALPHA * xres + mod[2:3] * y, lng_ref[...], lnb_ref[...])
    xo_ref[...] = xn
    h = xn * (1.0 + mod[4:5]) + mod[3:4]
    d = h.shape[1]
    h_ref[:, 0:d] = h
    rw = rwt_ref[...]
    rw_hi = rw.astype(BF16)
    rw_lo = (rw - rw_hi.astype(F32)).astype(BF16)
    h_hi = h.astype(BF16)
    h_lo = (h - h_hi.astype(F32)).astype(BF16)
    nt = (((1,), (1,)), ((), ()))
    logits_t = (lax.dot_general(rw_hi, h_hi, nt, preferred_element_type=F32)
                + lax.dot_general(rw_hi, h_lo, nt, preferred_element_type=F32)
                + lax.dot_general(rw_lo, h_hi, nt, preferred_element_type=F32))
    cls, w_lo, w_hi = _route(logits_t, rb_ref[...])

    @pl.when(jnp.logical_and(pl.program_id(0) == 0, pl.program_id(1) == 0))
    def _():
        run_scr[...] = jnp.zeros_like(run_scr)

    n = cls.shape[1]
    crow = lax.broadcasted_iota(jnp.int32, (CLS_PAD, n), 0).astype(F32)
    onehot = jnp.where(crow == cls, 1.0, 0.0)
    si = lax.broadcasted_iota(jnp.int32, (n, n), 0)
    ti = lax.broadcasted_iota(jnp.int32, (n, n), 1)
    before = jnp.where(si < ti, 1.0, 0.0).astype(BF16)
    cum = jnp.dot(onehot.astype(BF16), before, preferred_element_type=F32)
    run = run_scr[...]
    rank = jnp.sum(onehot * (cum + run[:, 0:1]), axis=0, keepdims=True)
    run = run + jnp.sum(onehot, axis=1, keepdims=True)
    run_scr[...] = run
    cnt_ref[...] = run
    info_ref[...] = jnp.zeros_like(info_ref)
    info_ref[0:1, :] = cls
    info_ref[1:2, :] = rank
    wt_scr[...] = jnp.zeros_like(wt_scr)
    wt_scr[0:1, :] = w_lo
    wt_scr[1:2, :] = w_hi
    h_ref[:, d:] = wt_scr[...].T


def _outproj(ya, yb, xres, modarr, w_out, ln_g, ln_b, rwt, rbias, row_blk0, gdn=None):
    bsz, n, wa = ya.shape

    def tok(width, col_blk=0):
        return pl.BlockSpec((None, TM, width), lambda b, j: (b, j, col_blk))

    if gdn is None:
        d = xres.shape[2]
        body, mix_args = _outproj_kernel, (ya, yb, xres)
        mix_specs = [tok(wa), tok(yb.shape[2]),
                     pl.BlockSpec((None, TM, d), lambda b, j: (b, j + row_blk0, 0))]
    else:
        o_fwd, o_rev, p, b_norm = gdn
        ctx, x = xres
        d = x.shape[2]
        gate_blk = (3 * A_WIDTH + 3 * B_WIDTH) // B_WIDTH
        body, mix_args = _outproj_even_kernel, (ya, o_fwd, o_rev, p, b_norm, ctx, x)
        mix_specs = [tok(wa), tok(B_WIDTH), tok(B_WIDTH), tok(B_WIDTH, gate_blk),
                     pl.BlockSpec((1, B_HEAD_DIM), lambda b, j: (0, 0)),
                     pl.BlockSpec((None, TM, d), lambda b, j: (b, 0, 0)),
                     pl.BlockSpec((None, TM, d), lambda b, j: (b, jnp.maximum(j - 1, 0), 0))]
    return pl.pallas_call(
        body,
        grid=(bsz, n // TM),
        in_specs=mix_specs + [
            pl.BlockSpec((None, None, 6, d), lambda b, j: (b, jnp.minimum(j + row_blk0, 1), 0, 0)),
            pl.BlockSpec(w_out.shape, lambda b, j: (0, 0)),
            pl.BlockSpec((1, d), lambda b, j: (0, 0)),
            pl.BlockSpec((1, d), lambda b, j: (0, 0)),
            pl.BlockSpec((N_EXPERTS, d), lambda b, j: (0, 0)),
            pl.BlockSpec((N_EXPERTS, 1), lambda b, j: (0, 0)),
        ],
        out_specs=[
            pl.BlockSpec((None, TM, d), lambda b, j: (b, j, 0)),
            pl.BlockSpec((None, TM, d + LANES), lambda b, j: (b, j, 0)),
            pl.BlockSpec((None, None, 8, TM), lambda b, j: (b, j, 0, 0)),
            pl.BlockSpec((CLS_PAD, LANES), lambda b, j: (0, 0)),
        ],
        out_shape=[
            jax.ShapeDtypeStruct((bsz, n, d), F32),
            jax.ShapeDtypeStruct((bsz, n, d + LANES), F32),
            jax.ShapeDtypeStruct((bsz, n // TM, 8, TM), F32),
            jax.ShapeDtypeStruct((CLS_PAD, LANES), F32),
        ],
        scratch_shapes=[pltpu.VMEM((CLS_PAD, LANES), F32),
                        pltpu.VMEM((LANES, TM), F32)],
        compiler_params=_cparams(2),
        name="outproj",
    )(*mix_args, modarr, w_out, ln_g, ln_b, rwt, rbias)


SCATTER_ROWS = 512


def _row_scatter_kernel(dst_ref, src_ref, init_hbm, out_hbm, sem):
    del init_hbm
    rows = src_ref.shape[0]

    def body(g, carry):
        r0 = pl.multiple_of(g * 8, 8)
        for u in range(8):
            pltpu.make_async_copy(src_ref.at[pl.ds(r0 + u, 1)],
                                  out_hbm.at[pl.ds(dst_ref[0, r0 + u], 1)], sem).start()
        return carry

    lax.fori_loop(0, rows // 8, body, 0)
    pltpu.make_async_copy(src_ref, out_hbm.at[pl.ds(0, rows)], sem).wait()


def _row_scatter(src, dest, n_out):
    n, width = src.shape
    rows = math.gcd(n, SCATTER_ROWS)
    return pl.pallas_call(
        _row_scatter_kernel,
        grid=(n // rows,),
        in_specs=[
            pl.BlockSpec((None, 1, rows), lambda j: (j, 0, 0), memory_space=pltpu.SMEM),
            pl.BlockSpec((rows, width), lambda j: (j, 0)),
            pl.BlockSpec(memory_space=pl.ANY),
        ],
        out_specs=pl.BlockSpec(memory_space=pl.ANY),
        out_shape=jax.ShapeDtypeStruct((n_out, width), src.dtype),
        scratch_shapes=[pltpu.SemaphoreType.DMA(())],
        input_output_aliases={2: 0},
        compiler_params=_cparams(1),
        name="row_scatter",
    )(dest.reshape(n // rows, 1, rows), src, jnp.zeros((n_out, width), src.dtype))


def _moe_kernel(elo_ref, ehi_ref, nused_ref, x_ref, wg_lo, wu_lo, wd_lo, wg_hi, wu_hi, wd_hi, o_ref):
    used = pl.program_id(0) < nused_ref[0]
    d = o_ref.shape[1]

    @pl.when(used)
    def _():
        x = x_ref[:, 0:d].astype(BF16)
        acc = None
        for col, (wg, wu, wd) in enumerate(((wg_lo, wu_lo, wd_lo), (wg_hi, wu_hi, wd_hi))):
            gate = jnp.dot(x, wg[...], preferred_element_type=F32)
            up = jnp.dot(x, wu[...], preferred_element_type=F32)
            act = (_silu(gate) * up).astype(BF16)
            y = x_ref[:, d + col:d + col + 1] * jnp.dot(act, wd[...], preferred_element_type=F32)
            acc = y if acc is None else acc + y
        o_ref[...] = acc

    @pl.when(jnp.logical_not(used))
    def _():
        o_ref[...] = jnp.zeros_like(o_ref)


def _moe_plan(info, counts, n_tok):
    mt = MOE_TM
    n_tiles = n_tok // mt + N_CLASSES
    cls = info[:, :, 0, :].reshape(-1).astype(jnp.int32)
    rank = info[:, :, 1, :].reshape(-1).astype(jnp.int32)
    cnt = counts[:N_CLASSES, 0].astype(jnp.int32)
    padded = ((cnt + mt - 1) // mt) * mt
    ends = jnp.cumsum(padded)
    starts = ends - padded
    classes = jnp.arange(N_CLASSES, dtype=jnp.int32)
    dest = jnp.sum(jnp.where(cls[:, None] == classes[None, :], starts[None, :], 0), axis=1) + rank
    n_used = ends[-1] // mt
    tidx = jnp.arange(n_tiles, dtype=jnp.int32)
    tidx = jnp.minimum(tidx, n_used - 1)
    tcls = jnp.sum((ends[None, :] <= (tidx * mt)[:, None]).astype(jnp.int32), axis=1)
    tcls = jnp.minimum(tcls, N_CLASSES - 1)
    pairs = [(a, b) for a in range(PER_GROUP) for b in range(a + 1, PER_GROUP)]
    pair = tcls % N_PAIRS
    lo = sum(jnp.where(pair == k, a, 0) for k, (a, _) in enumerate(pairs))
    hi = sum(jnp.where(pair == k, b, 0) for k, (_, b) in enumerate(pairs))
    group = tcls // N_PAIRS
    return (group * PER_GROUP + lo, group * PER_GROUP + hi, n_used.reshape(1).astype(jnp.int32),
            dest.astype(jnp.int32))


def _moe(h_ext, info, counts, wg, wu, wd):
    n, width = h_ext.shape
    d = width - LANES
    ne, _, de = wg.shape
    mt = MOE_TM
    elo, ehi, n_used, dest = _moe_plan(info, counts, n)
    n_tiles = n // mt + N_CLASSES
    h_sorted = _row_scatter(h_ext, dest, n_tiles * mt)

    def expert(which, shape):
        if which == 0:
            return pl.BlockSpec((None,) + shape, lambda i, lo, hi, nu: (lo[i], 0, 0))
        return pl.BlockSpec((None,) + shape, lambda i, lo, hi, nu: (hi[i], 0, 0))

    grid_spec = pltpu.PrefetchScalarGridSpec(
        num_scalar_prefetch=3,
        grid=(n_tiles,),
        in_specs=[
            pl.BlockSpec((mt, width), lambda i, lo, hi, nu: (jnp.minimum(i, nu[0] - 1), 0)),
            expert(0, (d, de)), expert(0, (d, de)), expert(0, (de, d)),
            expert(1, (d, de)), expert(1, (d, de)), expert(1, (de, d)),
        ],
        out_specs=pl.BlockSpec((mt, d), lambda i, *_: (i, 0)),
    )
    f_sorted = pl.pallas_call(
        _moe_kernel,
        grid_spec=grid_spec,
        out_shape=jax.ShapeDtypeStruct((n_tiles * mt, d), F32),
        compiler_params=_cparams(1),
        name="moe",
    )(elo, ehi, n_used, h_sorted, wg, wu, wd, wg, wu, wd)
    return f_sorted, dest


def _ln2_kernel(cur_ref, nxt_ref, x_ref, f_hbm, mod_ref, lng_ref, lnb_ref, o_ref, fbuf, sem):
    nj = pl.num_programs(1)
    step = pl.program_id(0) * nj + pl.program_id(1)
    n_steps = pl.num_programs(0) * nj
    slot = step % 2
    rows = fbuf.shape[1]

    def gather_start(idx_ref, s):
        def body(g, carry):
            r0 = pl.multiple_of(g * 8, 8)
            for u in range(8):
                pltpu.make_async_copy(f_hbm.at[pl.ds(idx_ref[0, r0 + u], 1)],
                                      fbuf.at[s, pl.ds(r0 + u, 1)], sem.at[s]).start()
            return carry
        lax.fori_loop(0, rows // 8, body, 0)

    @pl.when(step == 0)
    def _():
        gather_start(cur_ref, 0)

    def slot_wait(s):
        pltpu.make_async_copy(f_hbm.at[pl.ds(0, rows)], fbuf.at[s], sem.at[s]).wait()

    slot_wait(slot)
    for r in range(rows):
        pltpu.make_async_copy(f_hbm.at[pl.ds(nxt_ref[0, r], 1)], fbuf.at[1 - slot, pl.ds(r, 1)],
                              sem.at[1 - slot]).start()
    mod = mod_ref[...]
    v = DEEPNORM_ALPHA * x_ref[...] + mod[5:6] * fbuf[slot]
    o_ref[...] = _layer_norm(v, lng_ref[...], lnb_ref[...])

    @pl.when(step == n_steps - 1)
    def _():
        slot_wait(1 - slot)


def _ln2(x, f_sorted, dest, modarr, ln_g, ln_b, kind0):
    bsz, n, d = x.shape
    nj = n // TM
    n_steps = bsz * nj

    def idx_rows(offset):
        return pl.BlockSpec((None, 1, TM), lambda b, j: (jnp.minimum(b * nj + j + offset, n_steps - 1), 0, 0),
                            memory_space=pltpu.SMEM)

    dest3 = dest.reshape(n_steps, 1, TM)
    return pl.pallas_call(
        _ln2_kernel,
        grid=(bsz, nj),
        in_specs=[
            idx_rows(0), idx_rows(1),
            pl.BlockSpec((None, TM, d), lambda b, j: (b, j, 0)),
            pl.BlockSpec(memory_space=pl.ANY),
            pl.BlockSpec((None, None, 6, d), lambda b, j: (b, jnp.minimum(j + kind0, 1), 0, 0)),
            pl.BlockSpec((1, d), lambda b, j: (0, 0)),
            pl.BlockSpec((1, d), lambda b, j: (0, 0)),
        ],
        out_specs=pl.BlockSpec((None, TM, d), lambda b, j: (b, j, 0)),
        out_shape=jax.ShapeDtypeStruct((bsz, n, d), F32),
        scratch_shapes=[pltpu.VMEM((2, TM, d), F32), pltpu.SemaphoreType.DMA((2,))],
        compiler_params=_cparams(2),
        name="ln2",
    )(dest3, dest3, x, f_sorted, modarr, ln_g, ln_b)


def _rope(x, c, s1, s2, shift):
    w = x.shape[1]
    return x * c + pltpu.roll(x, w - shift, 1) * s1 + pltpu.roll(x, shift, 1) * s2


def _rms(x, g):
    return x * lax.rsqrt(jnp.mean(x * x, axis=-1, keepdims=True) + RMS_EPS) * g


def _inproj_odd_kernel(x_ref, mod_ref, w_ref, qn_ref, kvn_ref, wuq_ref, wk_ref, we_ref, wv_ref,
                       tw_ref, tq_ref, tk_ref, qw_ref, kw_ref, vw_ref, qm_ref, km_ref, vm_ref):
    j = pl.program_id(1)
    is_ctx = j == 0
    mod = mod_ref[...]
    h = (x_ref[...] * (1.0 + mod[1:2]) + mod[0:1]).astype(BF16)
    p = jnp.dot(h, w_ref[...], preferred_element_type=F32)

    def tables(t_ref):
        c = jnp.where(is_ctx, 1.0, t_ref[0])
        s1 = jnp.where(is_ctx, 0.0, t_ref[1])
        s2 = jnp.where(is_ctx, 0.0, t_ref[2])
        return c, s1, s2

    cw, s1w, s2w = tables(tw_ref)
    nq = C_Q_HEADS * C_HEAD_DIM
    for r in range(nq // LANES):
        blk = _rope(p[:, r * LANES:(r + 1) * LANES], cw, s1w, s2w, C_HEAD_DIM // 2)
        qw_ref[:, r * LANES:(r + 1) * LANES] = (blk * (C_HEAD_DIM ** -0.5)).astype(BF16)
    kw_ref[...] = _rope(p[:, nq:nq + LANES], cw, s1w, s2w, C_HEAD_DIM // 2).astype(BF16)
    vw_ref[...] = p[:, nq + LANES:nq + 2 * LANES].astype(BF16)

    o = nq + 2 * LANES
    dq = _rms(p[:, o:o + D_Q_RANK], qn_ref[...]).astype(BF16)
    o += D_Q_RANK
    dkv = _rms(p[:, o:o + D_KV_RANK], kvn_ref[...]).astype(BF16)
    o += D_KV_RANK
    cq, s1q, s2q = tables(tq_ref)
    ck, s1k, s2k = tables(tk_ref)
    krope = _rope(p[:, o:o + LANES], ck, s1k, s2k, D_ROPE // 2).astype(BF16)
    scale = (D_NOPE + D_ROPE) ** -0.5
    q_all = jnp.dot(dq, wuq_ref[...], preferred_element_type=F32)
    for hh in range(D_HEADS):
        sl = slice(hh * LANES, (hh + 1) * LANES)
        qm_ref[:, sl] = (_rope(q_all[:, sl], cq, s1q, s2q, D_ROPE // 2) * scale).astype(BF16)
    km_ref[...] = (jnp.dot(dkv, wk_ref[...], preferred_element_type=F32)
                   + jnp.dot(krope, we_ref[...], preferred_element_type=F32)).astype(BF16)
    vm_ref[...] = jnp.dot(dkv, wv_ref[...], preferred_element_type=F32).astype(BF16)


def _inproj_odd(xin, modarr, w1, qnorm, kvnorm, wuq, wk, we, wv, tab_w, tab_q, tab_k):
    bsz, t, d = xin.shape
    n1 = w1.shape[1]
    hw = D_HEADS * LANES

    def tab_spec():
        return pl.BlockSpec((3, TM, LANES), lambda b, j: (0, jnp.maximum(j - 1, 0), 0))

    def full(a):
        return pl.BlockSpec(a.shape, lambda b, j: (0,) * a.ndim)

    def out(width):
        return (pl.BlockSpec((None, TM, width), lambda b, j: (b, j, 0)),
                jax.ShapeDtypeStruct((bsz, t, width), BF16))

    def out_latent(width):
        return (pl.BlockSpec((None, TM, width), lambda b, j: (b, jnp.maximum(j - 1, 0), 0)),
                jax.ShapeDtypeStruct((bsz, t - TM, width), BF16))

    outs = (out_latent(C_Q_HEADS * C_HEAD_DIM), out(LANES), out(LANES), out_latent(hw), out(hw), out(hw))
    return pl.pallas_call(
        _inproj_odd_kernel,
        grid=(bsz, t // TM),
        in_specs=[
            pl.BlockSpec((None, TM, d), lambda b, j: (b, j, 0)),
            pl.BlockSpec((None, None, 6, d), lambda b, j: (b, jnp.minimum(j, 1), 0, 0)),
            full(w1), full(qnorm), full(kvnorm), full(wuq), full(wk), full(we), full(wv),
            tab_spec(), tab_spec(), tab_spec(),
        ],
        out_specs=[o[0] for o in outs],
        out_shape=[o[1] for o in outs],
        compiler_params=_cparams(2),
        name="inproj_odd",
    )(xin, modarr, w1, qnorm, kvnorm, wuq, wk, we, wv, tab_w, tab_q, tab_k)


WIN_TQ = 256


def _win_kernel(sink_ref, q_ref, k_ref, v_ref, o_ref, klo_scr, khi_scr, *, n_ctx):
    i = pl.program_id(1)
    wdw = C_WINDOW
    t = k_ref.shape[0]
    lane = lax.broadcasted_iota(jnp.int32, (t, LANES), 1)

    @pl.when(i == 0)
    def _():
        kk = k_ref[...]
        klo_scr[...] = jnp.where(lane < C_HEAD_DIM, kk, jnp.zeros_like(kk))
        khi_scr[...] = jnp.where(lane >= C_HEAD_DIM, kk, jnp.zeros_like(kk))

    tq = q_ref.shape[0]
    span = tq + 2 * wdw
    n_lat_blk = (t - n_ctx) // wdw
    blk0 = jnp.clip(i * (tq // wdw) - 1, 0, n_lat_blk - span // wdw)
    r0 = pl.multiple_of(n_ctx + blk0 * wdw, wdw)
    kpos = blk0 * wdw + lax.broadcasted_iota(jnp.int32, (tq, span), 1)
    qpos = i * tq + lax.broadcasted_iota(jnp.int32, (tq, span), 0)
    near = jnp.abs(kpos - qpos) <= wdw
    v_loc = v_ref[pl.ds(r0, span), :]
    v_ctx = v_ref[0:n_ctx, :]
    olane = lax.broadcasted_iota(jnp.int32, (tq, LANES), 1)
    n_rep = C_Q_HEADS // C_KV_HEADS
    for r in range(n_rep):
        q = q_ref[:, r * LANES:(r + 1) * LANES]
        outs = []
        for g, k_scr in enumerate((klo_scr, khi_scr)):
            k_loc = k_scr[pl.ds(r0, span), :]
            k_ctx = k_scr[0:n_ctx, :]
            s_loc = lax.dot_general(q, k_loc, (((1,), (1,)), ((), ())), preferred_element_type=F32)
            s_loc = jnp.where(near, s_loc, -jnp.inf)
            s_ctx = lax.dot_general(q, k_ctx, (((1,), (1,)), ((), ())), preferred_element_type=F32)
            sink = sink_ref[g * n_rep + r]
            m = jnp.maximum(jnp.maximum(jnp.max(s_loc, axis=-1, keepdims=True),
                                        jnp.max(s_ctx, axis=-1, keepdims=True)), sink)
            p_loc = jnp.exp(s_loc - m)
            p_ctx = jnp.exp(s_ctx - m)
            den = (jnp.sum(p_loc, axis=-1, keepdims=True) + jnp.sum(p_ctx, axis=-1, keepdims=True)
                   + jnp.exp(sink - m))
            pv = (jnp.dot(p_loc.astype(BF16), v_loc, preferred_element_type=F32)
                  + jnp.dot(p_ctx.astype(BF16), v_ctx, preferred_element_type=F32))
            outs.append(pv / den)
        o_ref[:, r * LANES:(r + 1) * LANES] = jnp.where(olane < C_HEAD_DIM, outs[0], outs[1]).astype(BF16)


def _win_attention(sink, qw, kw, vw, n_ctx):
    bsz, n_lat, nq = qw.shape
    t = n_ctx + n_lat
    grid_spec = pltpu.PrefetchScalarGridSpec(
        num_scalar_prefetch=1,
        grid=(bsz, n_lat // WIN_TQ),
        in_specs=[
            pl.BlockSpec((None, WIN_TQ, nq), lambda b, i, s: (b, i, 0)),
            pl.BlockSpec((None, t, LANES), lambda b, i, s: (b, 0, 0)),
            pl.BlockSpec((None, t, LANES), lambda b, i, s: (b, 0, 0)),
        ],
        out_specs=pl.BlockSpec((None, WIN_TQ, nq), lambda b, i, s: (b, i, 0)),
        scratch_shapes=[pltpu.VMEM((t, LANES), BF16), pltpu.VMEM((t, LANES), BF16)],
    )
    return pl.pallas_call(
        functools.partial(_win_kernel, n_ctx=n_ctx),
        grid_spec=grid_spec,
        out_shape=jax.ShapeDtypeStruct((bsz, n_lat, nq), BF16),
        compiler_params=_cparams(2),
        name="win_attention",
    )(sink, qw, kw, vw)


MLA_TQ = 512


def _mla_kernel(q_ref, k_ref, v_ref, o_ref):
    for hp in range(D_HEADS // 2):
        acc = None
        for hh in (2 * hp, 2 * hp + 1):
            sl = slice(hh * LANES, (hh + 1) * LANES)
            s = lax.dot_general(q_ref[:, sl], k_ref[:, sl], (((1,), (1,)), ((), ())),
                                preferred_element_type=F32)
            m = jnp.max(s, axis=-1, keepdims=True)
            p = jnp.exp(s - m)
            den = jnp.sum(p, axis=-1, keepdims=True)
            pv = jnp.dot(p.astype(BF16), v_ref[:, sl], preferred_element_type=F32) / den
            acc = pv if acc is None else acc + pv
        o_ref[:, hp * LANES:(hp + 1) * LANES] = acc.astype(BF16)


def _mla_attention(qm, km, vm, n_ctx):
    bsz, n_lat, hw = qm.shape
    t = n_ctx + n_lat
    ow = D_HEADS * D_V
    return pl.pallas_call(
        _mla_kernel,
        grid=(bsz, n_lat // MLA_TQ),
        in_specs=[
            pl.BlockSpec((None, MLA_TQ, hw), lambda b, i: (b, i, 0)),
            pl.BlockSpec((None, t, hw), lambda b, i: (b, 0, 0)),
            pl.BlockSpec((None, t, hw), lambda b, i: (b, 0, 0)),
        ],
        out_specs=pl.BlockSpec((None, MLA_TQ, ow), lambda b, i: (b, i, 0)),
        out_shape=jax.ShapeDtypeStruct((bsz, n_lat, ow), BF16),
        compiler_params=_cparams(2),
        name="mla_attention",
    )(qm, km, vm)


def _rope_tables(n_tokens, rot_dim, group, offset):
    t = jnp.arange(n_tokens)
    rows = (t // GRID_W).astype(F32)
    cols = (t % GRID_W).astype(F32)
    n_freq = rot_dim // 4
    inv_freq = ROPE_BASE ** (-jnp.arange(n_freq, dtype=F32) / n_freq)
    ang = jnp.concatenate([rows[:, None] * inv_freq, cols[:, None] * inv_freq], -1)
    cos, sin = jnp.cos(ang), jnp.sin(ang)
    half = rot_dim // 2
    c = jnp.ones((n_tokens, LANES), F32)
    s1 = jnp.zeros((n_tokens, LANES), F32)
    s2 = jnp.zeros((n_tokens, LANES), F32)
    for start in range(offset, LANES, group):
        c = c.at[:, start:start + half].set(cos).at[:, start + half:start + rot_dim].set(cos)
        s1 = s1.at[:, start:start + half].set(-sin)
        s2 = s2.at[:, start + half:start + rot_dim].set(sin)
    return jnp.stack([c, s1, s2])


def _odd_weights(w_in, wuq, wukv, w_out):
    d = w_in.shape[0]
    nq = C_Q_HEADS * C_HEAD_DIM
    nkv = C_KV_HEADS * C_HEAD_DIM
    n_rep = C_Q_HEADS // C_KV_HEADS
    order = [g * n_rep + r for r in range(n_rep) for g in range(C_KV_HEADS)]
    cq = w_in[:, :nq].reshape(d, C_Q_HEADS, C_HEAD_DIM)[:, order].reshape(d, nq)
    rest = w_in[:, nq:nq + 2 * nkv + D_Q_RANK + D_KV_RANK]
    krope = jnp.pad(w_in[:, nq + 2 * nkv + D_Q_RANK + D_KV_RANK:], ((0, 0), (0, LANES - D_ROPE)))
    w1 = jnp.concatenate([cq, rest, krope], axis=1).astype(BF16)
    qh = wuq.reshape(D_Q_RANK, D_HEADS, D_NOPE + D_ROPE)
    wuq_p = jnp.pad(qh, ((0, 0), (0, 0), (0, LANES - D_NOPE - D_ROPE))).reshape(D_Q_RANK, D_HEADS * LANES)
    kvh = wukv.reshape(D_KV_RANK, D_HEADS, D_NOPE + D_V)
    wk_p = jnp.pad(kvh[:, :, :D_NOPE], ((0, 0), (0, 0), (0, LANES - D_NOPE))).reshape(D_KV_RANK, D_HEADS * LANES)
    e_blk = jnp.zeros((LANES, LANES), F32).at[jnp.arange(D_ROPE), D_NOPE + jnp.arange(D_ROPE)].set(1.0)
    we = jnp.tile(e_blk, (1, D_HEADS))
    vh = kvh[:, :, D_NOPE:]
    even = (jnp.arange(D_HEADS) % 2 == 0)[None, :, None]
    wv_p = jnp.where(even, jnp.pad(vh, ((0, 0), (0, 0), (0, D_V))),
                     jnp.pad(vh, ((0, 0), (0, 0), (D_V, 0)))).reshape(D_KV_RANK, D_HEADS * LANES)
    wo_win = w_out[:nq].reshape(C_Q_HEADS, C_HEAD_DIM, -1)[jnp.array(order)].reshape(nq, -1)
    wo = jnp.concatenate([wo_win, w_out[nq:]], axis=0).astype(BF16)
    return w1, wuq_p.astype(BF16), wk_p.astype(BF16), we.astype(BF16), wv_p.astype(BF16), wo


def kernel(x, c, ctx, c_ctx, ada_w, ada_b, ln_g, ln_b, ev_w_in, ev_a_conv, ev_b_conv, ev_b_alog, ev_b_dtbias, ev_b_norm, ev_w_out, od_w_in, od_c_sink, od_d_qnorm, od_d_kvnorm, od_d_wuq, od_d_wukv, od_w_out, router_w, router_bias, moe_w_gate, moe_w_up, moe_w_down):
    bsz, n_lat, d = x.shape
    n_ctx = ctx.shape[1]
    assert n_ctx == TM and n_lat % TM == 0 and n_lat % GRID_W == 0
    assert ada_w.shape[0] == DEPTH and bsz + 1 <= 40
    t = n_ctx + n_lat

    cs = jnp.zeros((40, d), F32).at[:bsz].set(c).at[bsz].set(c_ctx)
    mods = _ada_mod(cs, ada_w, ada_b)

    def modarr(layer):
        m = mods[layer].reshape(40, 6, d)
        return jnp.stack([jnp.broadcast_to(m[bsz], (bsz, 6, d)), m[:bsz]], axis=1)

    rwt = router_w.T
    rbias = router_bias.reshape(N_EXPERTS, 1)

    mod0 = modarr(0)
    n_main = 3 * A_WIDTH + 4 * B_WIDTH
    w_main = ev_w_in[0][:, :n_main].astype(BF16)
    w_small = jnp.pad(ev_w_in[0][:, n_main:], ((0, 0), (0, LANES - 4 * B_HEADS))).astype(BF16)
    p, small = _inproj_even(ctx, x, mod0, w_main, w_small)
    alog_pad = jnp.zeros((1, LANES), F32).at[0, 8:16].set(ev_b_alog[0].reshape(-1))
    dtb_pad = jnp.zeros((1, LANES), F32).at[0, 8:16].set(ev_b_dtbias[0].reshape(-1))
    ya, u, w, qe, ket, att, dec = _even_prep(p, small, ev_a_conv[0], ev_b_conv[0], alog_pad, dtb_pad,
                                             GDN_CHUNK)
    o_fwd, o_rev = _gdn_rec(u, w, qe, ket, att, dec, GDN_CHUNK)
    x1, h1, info0, cnt0 = _outproj(ya, None, (ctx, x), mod0, ev_w_out[0].astype(BF16),
                             ln_g[0, 0].reshape(1, d), ln_b[0, 0].reshape(1, d), rwt, rbias, 0,
                             gdn=(o_fwd, o_rev, p, ev_b_norm[0].reshape(1, B_HEAD_DIM)))
    f, dest = _moe(h1.reshape(bsz * t, d + LANES), info0, cnt0,
                   moe_w_gate[0].astype(BF16), moe_w_up[0].astype(BF16), moe_w_down[0].astype(BF16))
    x2 = _ln2(x1, f, dest, mod0, ln_g[0, 1].reshape(1, d), ln_b[0, 1].reshape(1, d), 0)

    mod1 = modarr(1)
    w1, wuq_p, wk_p, we, wv_p, wo = _odd_weights(od_w_in[0], od_d_wuq[0], od_d_wukv[0], od_w_out[0])
    tab_w = _rope_tables(n_lat, C_HEAD_DIM, C_HEAD_DIM, 0)
    tab_q = _rope_tables(n_lat, D_ROPE, LANES, D_NOPE)
    tab_k = _rope_tables(n_lat, D_ROPE, LANES, 0)
    qw, kw, vw, qm, km, vm = _inproj_odd(
        x2, mod1, w1, od_d_qnorm[0].reshape(1, -1), od_d_kvnorm[0].reshape(1, -1),
        wuq_p, wk_p, we, wv_p, tab_w, tab_q, tab_k)
    y_win = _win_attention(od_c_sink[0], qw, kw, vw, n_ctx)
    y_mla = _mla_attention(qm, km, vm, n_ctx)
    x3, h3, info1, cnt1 = _outproj(y_win, y_mla, x2, mod1, wo, ln_g[1, 0].reshape(1, d),
                                   ln_b[1, 0].reshape(1, d), rwt, rbias, n_ctx // TM)
    f1, dest1 = _moe(h3.reshape(bsz * n_lat, d + LANES), info1, cnt1,
                     moe_w_gate[1].astype(BF16), moe_w_up[1].astype(BF16), moe_w_down[1].astype(BF16))
    return _ln2(x3, f1, dest1, mod1, ln_g[1, 1].reshape(1, d), ln_b[1, 1].reshape(1, d), 1)
```

```python
import functools
import math

import numpy as np
import jax
import jax.numpy as jnp
from jax import lax
from jax.experimental import pallas as pl
from jax.experimental.pallas import tpu as pltpu

F32 = jnp.float32
BF16 = jnp.bfloat16
HIGHEST = lax.Precision.HIGHEST

DEPTH = 2
GRID_W = 64
DEEPNORM_ALPHA = (2.0 * DEPTH) ** 0.25
LN_EPS = 1e-5
RMS_EPS = 1e-6
ROPE_BASE = 10000.0
B_HEADS = 4
B_HEAD_DIM = 128
B_WIDTH = 512
A_WIDTH = 512
C_Q_HEADS = 8
C_KV_HEADS = 2
C_HEAD_DIM = 64
C_WINDOW = 128
D_HEADS = 8
D_NOPE = 64
D_ROPE = 32
D_V = 64
D_Q_RANK = 384
D_KV_RANK = 256
N_EXPERTS = 16
N_GROUPS = 4
PER_GROUP = N_EXPERTS // N_GROUPS
D_EXPERT = 512
N_PAIRS = PER_GROUP * (PER_GROUP - 1) // 2
N_CLASSES = N_GROUPS * N_PAIRS
CLS_PAD = 32

LANES = 128
TM = 256
GDN_CHUNK = 64
MOE_TM = 256
VMEM_LIMIT = 56 * 1024 * 1024


def _cparams(n_axes, vmem=VMEM_LIMIT):
    return pltpu.CompilerParams(dimension_semantics=("arbitrary",) * n_axes, vmem_limit_bytes=vmem)


def _sigmoid(x):
    return 1.0 / (1.0 + jnp.exp(-x))


def _silu(x):
    return x * _sigmoid(x)


def _softplus(x):
    return jnp.maximum(x, 0.0) + jnp.log(1.0 + jnp.exp(-jnp.abs(x)))


def _layer_norm(v, g, b):
    mu = jnp.mean(v, axis=-1, keepdims=True)
    d = v - mu
    var = jnp.mean(d * d, axis=-1, keepdims=True)
    return d * lax.rsqrt(var + LN_EPS) * g + b


def _ada_kernel(c_ref, w_ref, b_ref, o_ref):
    s = _silu(c_ref[...])
    o_ref[...] = jnp.dot(s, w_ref[...], precision=HIGHEST, preferred_element_type=F32) + b_ref[...]


def _ada_mod(cs, ada_w, ada_b):
    depth, d, n6 = ada_w.shape
    rows = cs.shape[0]
    tn = 1536
    return pl.pallas_call(
        _ada_kernel,
        grid=(depth, n6 // tn),
        in_specs=[
            pl.BlockSpec((rows, d), lambda l, n: (0, 0)),
            pl.BlockSpec((None, d, tn), lambda l, n: (l, 0, n)),
            pl.BlockSpec((None, 1, tn), lambda l, n: (l, 0, n)),
        ],
        out_specs=pl.BlockSpec((None, rows, tn), lambda l, n: (l, 0, n)),
        out_shape=jax.ShapeDtypeStruct((depth, rows, n6), F32),
        compiler_params=_cparams(2),
        name="ada_mod",
    )(cs, ada_w, ada_b.reshape(depth, 1, n6))


def _inproj_even_kernel(c_ref, x_ref, mod_ref, wm_ref, ws_ref, p_ref, s_ref):
    mod = mod_ref[...]
    xv = jnp.where(pl.program_id(1) == 0, c_ref[...], x_ref[...])
    h = (xv * (1.0 + mod[1:2]) + mod[0:1]).astype(BF16)
    p_ref[...] = jnp.dot(h, wm_ref[...], preferred_element_type=F32).astype(BF16)
    s_ref[...] = jnp.dot(h, ws_ref[...], preferred_element_type=F32)


def _inproj_even(ctx, x, modarr, w_main, w_small):
    bsz, n_lat, d = x.shape
    t = ctx.shape[1] + n_lat
    nm = w_main.shape[1]
    return pl.pallas_call(
        _inproj_even_kernel,
        grid=(bsz, t // TM),
        in_specs=[
            pl.BlockSpec((None, TM, d), lambda b, j: (b, 0, 0)),
            pl.BlockSpec((None, TM, d), lambda b, j: (b, jnp.maximum(j - 1, 0), 0)),
            pl.BlockSpec((None, None, 6, d), lambda b, j: (b, jnp.minimum(j, 1), 0, 0)),
            pl.BlockSpec((d, nm), lambda b, j: (0, 0)),
            pl.BlockSpec((d, LANES), lambda b, j: (0, 0)),
        ],
        out_specs=[
            pl.BlockSpec((None, TM, nm), lambda b, j: (b, j, 0)),
            pl.BlockSpec((None, TM, LANES), lambda b, j: (b, j, 0)),
        ],
        out_shape=[
            jax.ShapeDtypeStruct((bsz, t, nm), BF16),
            jax.ShapeDtypeStruct((bsz, t, LANES), F32),
        ],
        compiler_params=_cparams(2),
        name="inproj_even",
    )(ctx, x, modarr, w_main, w_small)


HALO = 16


def _conv3(z, zp, zn, w):
    n = z.shape[0]
    rows = lax.broadcasted_iota(jnp.int32, z.shape, 0)
    zprev = jnp.where(rows == 0, zp, pltpu.roll(z, 1, 0))
    znext = jnp.where(rows == n - 1, zn, pltpu.roll(z, n - 1, 0))
    return w[0:1] * zprev + w[1:2] * z + w[2:3] * znext


def _even_prep_kernel(p_ref, pp_ref, pn_ref, s_ref, aw_ref, bw_ref, alog_ref, dtb_ref,
                      ya_ref, u_ref, w_ref, qe_ref, ket_ref, att_ref, dec_ref,
                      q_scr, k_scr, v_scr, *, chunk):
    j = pl.program_id(1)
    nj = pl.num_programs(1)
    prev_on = jnp.where(jnp.logical_and(j != 0, j != 1), 1.0, 0.0)
    next_on = jnp.where(jnp.logical_and(j != 0, j != nj - 1), 1.0, 0.0)
    prow = pp_ref[...].astype(F32)[HALO - 1:HALO] * prev_on
    nrow = pn_ref[...].astype(F32)[0:1] * next_on

    def seg(lo, hi):
        return p_ref[:, lo:hi].astype(F32), prow[:, lo:hi], nrow[:, lo:hi]

    a0, _, _ = seg(0, A_WIDTH)
    a1, a1p, a1n = seg(A_WIDTH, 2 * A_WIDTH)
    a2, a2p, a2n = seg(2 * A_WIDTH, 3 * A_WIDTH)
    ya_ref[...] = (a0 * _conv3(a1 * a2, a1p * a2p, a1n * a2n, aw_ref[...])).astype(BF16)

    base = 3 * A_WIDTH
    for which in range(3):
        lo = base + which * B_WIDTH
        z, zp, zn = seg(lo, lo + B_WIDTH)
        c = _silu(_conv3(z, zp, zn, bw_ref[:, which * B_WIDTH:(which + 1) * B_WIDTH]))
        for h in range(B_HEADS):
            ch = c[:, h * B_HEAD_DIM:(h + 1) * B_HEAD_DIM]
            if which < 2:
                ss = jnp.sum(ch * ch, axis=-1, keepdims=True)
                ch = ch * lax.rsqrt(ss + 1e-6)
                if which == 0:
                    ch = ch * (B_HEAD_DIM ** -0.5)
            sl = slice(h * B_HEAD_DIM, (h + 1) * B_HEAD_DIM)
            if which == 0:
                q_scr[:, sl] = ch.astype(BF16)
            elif which == 1:
                k_scr[:, sl] = ch.astype(BF16)
            else:
                v_scr[:, sl] = ch

    s = s_ref[...]
    beta = _sigmoid(s)
    g = -jnp.exp(alog_ref[...]) * _softplus(s + dtb_ref[...])
    n = s.shape[0]
    nck = n // chunk
    ri = lax.broadcasted_iota(jnp.int32, (n, n), 0)
    ci = lax.broadcasted_iota(jnp.int32, (n, n), 1)
    same = (ri // chunk) == (ci // chunk)
    m_fwd = jnp.where(jnp.logical_and(same, ci <= ri), 1.0, 0.0).astype(BF16)
    m_rev = jnp.where(jnp.logical_and(same, ci >= ri), 1.0, 0.0).astype(BF16)
    g1 = g.astype(BF16)
    r1 = g - g1.astype(F32)
    g2 = r1.astype(BF16)
    g3 = (r1 - g2.astype(F32)).astype(BF16)
    g_split = jnp.concatenate([g1, g2, g3], axis=1)

    def cumulate(m):
        parts = jnp.dot(m, g_split, preferred_element_type=F32)
        return parts[:, 0:LANES] + parts[:, LANES:2 * LANES] + parts[:, 2 * LANES:]

    gc_f = cumulate(m_fwd)
    gc_r = cumulate(m_rev)
    tot = gc_f + gc_r - g
    lane = lax.broadcasted_iota(jnp.int32, s.shape, 1)
    gc = jnp.where(lane >= 8 + B_HEADS, gc_r, gc_f)
    e_gc = jnp.exp(gc)
    e_rest = jnp.exp(tot - gc)
    gct = gc.T
    e_tot = jnp.exp(tot)

    ri = lax.broadcasted_iota(jnp.int32, (1, chunk, chunk), 1)
    ci = lax.broadcasted_iota(jnp.int32, (1, chunk, chunk), 2)
    eye = jnp.where(ri == ci, 1.0, 0.0)
    n_sq = int(np.log2(chunk))
    nt_batched = (((2,), (2,)), ((0,), (0,)))
    nn_batched = (((2,), (1,)), ((0,), (0,)))
    kk, qk, kf, qf = [], [], [], []
    for h in range(B_HEADS):
        hs = slice(h * B_HEAD_DIM, (h + 1) * B_HEAD_DIM)
        k3 = k_scr[:, hs].reshape(nck, chunk, B_HEAD_DIM)
        q3 = q_scr[:, hs].reshape(nck, chunk, B_HEAD_DIM)
        kk.append(lax.dot_general(k3, k3, nt_batched, preferred_element_type=F32))
        qk.append(lax.dot_general(q3, k3, nt_batched, preferred_element_type=F32))
        kf.append(k_scr[:, hs].astype(F32))
        qf.append(q_scr[:, hs].astype(F32))
    a_all, rhs_all = [], []
    for d in range(2):
        incl = (ci <= ri) if d == 0 else (ci >= ri)
        strict = (ci < ri) if d == 0 else (ci > ri)
        for h in range(B_HEADS):
            chain = d * B_HEADS + h
            hs = slice(h * B_HEAD_DIM, (h + 1) * B_HEAD_DIM)
            bcol = beta[:, chain:chain + 1]
            e1 = e_gc[:, 8 + chain:9 + chain]
            e2 = e_rest[:, 8 + chain:9 + chain]
            gcol = gc[:, 8 + chain:9 + chain].reshape(nck, chunk, 1)
            grow = jnp.stack([gct[8 + chain:9 + chain, cc * chunk:(cc + 1) * chunk]
                              for cc in range(nck)], axis=0)
            decay = jnp.exp(jnp.where(incl, gcol - grow, -jnp.inf))
            a_all.append(jnp.where(strict, bcol.reshape(nck, chunk, 1) * kk[h] * decay, 0.0))
            att_ref[d, h] = (qk[h] * decay).reshape(n, chunk).astype(BF16)
            qe_ref[d, h] = (qf[h] * e1).astype(BF16)
            ket = (kf[h] * e2).T
            for cc in range(nck):
                ket_ref[d, h, cc] = ket[:, cc * chunk:(cc + 1) * chunk].astype(BF16)
                dec_ref[cc, chain] = jnp.broadcast_to(
                    e_tot[cc * chunk:cc * chunk + 1, 8 + chain:9 + chain], (1, LANES))
            rhs = jnp.concatenate([(v_scr[:, hs] * bcol).astype(BF16),
                                   (kf[h] * (bcol * e1)).astype(BF16)], axis=1)
            rhs_all.append(rhs.reshape(nck, chunk, 2 * B_HEAD_DIM))
    npow = -jnp.concatenate(a_all, axis=0)
    tinv = eye + npow
    for _ in range(n_sq - 1):
        nb = npow.astype(BF16)
        npow = lax.dot_general(nb, nb, nn_batched, preferred_element_type=F32)
        tinv = tinv + lax.dot_general(tinv.astype(BF16), npow.astype(BF16), nn_batched,
                                      preferred_element_type=F32)
    uw = lax.dot_general(tinv.astype(BF16), jnp.concatenate(rhs_all, axis=0), nn_batched,
                         preferred_element_type=F32)
    for d in range(2):
        for h in range(B_HEADS):
            blk = uw[(d * B_HEADS + h) * nck:(d * B_HEADS + h + 1) * nck]
            u_ref[d, h] = blk[:, :, :B_HEAD_DIM].reshape(n, B_HEAD_DIM).astype(BF16)
            w_ref[d, h] = blk[:, :, B_HEAD_DIM:].reshape(n, B_HEAD_DIM).astype(BF16)


def _even_prep(p, small, a_conv, b_conv, alog_pad, dtb_pad, chunk):
    bsz, t, nm = p.shape
    nhb = TM // HALO
    last_hb = t // HALO - 1
    nck = TM // chunk
    nc = t // chunk

    def per_dir():
        return (pl.BlockSpec((None, 2, B_HEADS, TM, B_HEAD_DIM), lambda b, j: (b, 0, 0, j, 0)),
                jax.ShapeDtypeStruct((bsz, 2, B_HEADS, t, B_HEAD_DIM), BF16))

    outs = [
        (pl.BlockSpec((None, TM, A_WIDTH), lambda b, j: (b, j, 0)),
         jax.ShapeDtypeStruct((bsz, t, A_WIDTH), BF16)),
        per_dir(), per_dir(), per_dir(),
        (pl.BlockSpec((None, 2, B_HEADS, nck, B_HEAD_DIM, chunk), lambda b, j: (b, 0, 0, j, 0, 0)),
         jax.ShapeDtypeStruct((bsz, 2, B_HEADS, nc, B_HEAD_DIM, chunk), BF16)),
        (pl.BlockSpec((None, 2, B_HEADS, TM, chunk), lambda b, j: (b, 0, 0, j, 0)),
         jax.ShapeDtypeStruct((bsz, 2, B_HEADS, t, chunk), BF16)),
        (pl.BlockSpec((None, nck, 2 * B_HEADS, 1, LANES), lambda b, j: (b, j, 0, 0, 0)),
         jax.ShapeDtypeStruct((bsz, nc, 2 * B_HEADS, 1, LANES), F32)),
    ]
    return pl.pallas_call(
        functools.partial(_even_prep_kernel, chunk=chunk),
        grid=(bsz, t // TM),
        in_specs=[
            pl.BlockSpec((None, TM, nm), lambda b, j: (b, j, 0)),
            pl.BlockSpec((None, HALO, nm), lambda b, j: (b, jnp.maximum(j * nhb - 1, 0), 0)),
            pl.BlockSpec((None, HALO, nm), lambda b, j: (b, jnp.minimum((j + 1) * nhb, last_hb), 0)),
            pl.BlockSpec((None, TM, LANES), lambda b, j: (b, j, 0)),
            pl.BlockSpec((3, A_WIDTH), lambda b, j: (0, 0)),
            pl.BlockSpec((3, 3 * B_WIDTH), lambda b, j: (0, 0)),
            pl.BlockSpec((1, LANES), lambda b, j: (0, 0)),
            pl.BlockSpec((1, LANES), lambda b, j: (0, 0)),
        ],
        out_specs=[o[0] for o in outs],
        out_shape=[o[1] for o in outs],
        scratch_shapes=[pltpu.VMEM((TM, B_WIDTH), BF16), pltpu.VMEM((TM, B_WIDTH), BF16),
                        pltpu.VMEM((TM, B_WIDTH), F32)],
        compiler_params=_cparams(2),
        name="even_prep",
    )(p, p, p, small, a_conv, b_conv, alog_pad, dtb_pad)


def _gdn_rec_kernel(uf, wf, qf, kf, af, df, ur, wr, qr, kr, ar, dr, of_ref, or_ref, s_scr, *, chunk):
    @pl.when(pl.program_id(1) == 0)
    def _():
        s_scr[...] = jnp.zeros_like(s_scr)

    nck = uf.shape[1] // chunk
    nn_batched = (((2,), (1,)), ((0,), (0,)))

    def both(fwd, rev):
        return jnp.concatenate([fwd, rev], axis=0)

    for cc in range(nck):
        cr = nck - 1 - cc
        rf = slice(cc * chunk, (cc + 1) * chunk)
        rr = slice(cr * chunk, (cr + 1) * chunk)
        s = s_scr[...]
        sb = s.astype(BF16)
        u = both(uf[:, rf, :], ur[:, rr, :]).astype(F32)
        v_new = u - lax.dot_general(both(wf[:, rf, :], wr[:, rr, :]), sb, nn_batched,
                                    preferred_element_type=F32)
        vb = v_new.astype(BF16)
        o = (lax.dot_general(both(qf[:, rf, :], qr[:, rr, :]), sb, nn_batched,
                             preferred_element_type=F32)
             + lax.dot_general(both(af[:, rf, :], ar[:, rr, :]), vb, nn_batched,
                               preferred_element_type=F32))
        dec = both(df[cc, 0:B_HEADS], dr[cr, B_HEADS:2 * B_HEADS])
        s_scr[...] = s * dec + lax.dot_general(both(kf[:, cc], kr[:, cr]), vb, nn_batched,
                                               preferred_element_type=F32)
        for h in range(B_HEADS):
            hs = slice(h * B_HEAD_DIM, (h + 1) * B_HEAD_DIM)
            of_ref[rf, hs] = o[h].astype(BF16)
            or_ref[rr, hs] = o[B_HEADS + h].astype(BF16)


def _gdn_rec(u, w, qe, ket, att, dec, chunk):
    bsz, _, _, t, _ = u.shape
    nt = t // TM
    nck = TM // chunk

    def tile(d, s):
        return s if d == 0 else jnp.where(s == 0, 0, nt - s)

    in_specs = []
    for d in range(2):
        for _ in range(3):
            in_specs.append(pl.BlockSpec((None, None, B_HEADS, TM, B_HEAD_DIM),
                                         lambda b, s, d=d: (b, d, 0, tile(d, s), 0)))
        in_specs.append(pl.BlockSpec((None, None, B_HEADS, nck, B_HEAD_DIM, chunk),
                                     lambda b, s, d=d: (b, d, 0, tile(d, s), 0, 0)))
        in_specs.append(pl.BlockSpec((None, None, B_HEADS, TM, chunk),
                                     lambda b, s, d=d: (b, d, 0, tile(d, s), 0)))
        in_specs.append(pl.BlockSpec((None, nck, 2 * B_HEADS, 1, LANES),
                                     lambda b, s, d=d: (b, tile(d, s), 0, 0, 0)))
    return pl.pallas_call(
        functools.partial(_gdn_rec_kernel, chunk=chunk),
        grid=(bsz, nt),
        in_specs=in_specs,
        out_specs=[pl.BlockSpec((None, TM, B_WIDTH), lambda b, s, d=d: (b, tile(d, s), 0))
                   for d in range(2)],
        out_shape=[jax.ShapeDtypeStruct((bsz, t, B_WIDTH), BF16)] * 2,
        scratch_shapes=[pltpu.VMEM((2 * B_HEADS, B_HEAD_DIM, B_HEAD_DIM), F32)],
        compiler_params=_cparams(2),
        name="gdn_rec",
    )(u, w, qe, ket, att, dec, u, w, qe, ket, att, dec)


def _route(logits_t, bias):
    aff = _sigmoid(logits_t)
    sel = aff + bias
    e_idx = lax.broadcasted_iota(jnp.int32, sel.shape, 0)
    pos = e_idx % PER_GROUP
    grp = e_idx // PER_GROUP

    def group_rot(v, k):
        return jnp.where(pos + k < PER_GROUP, pltpu.roll(v, N_EXPERTS - k, 0),
                         pltpu.roll(v, PER_GROUP - k, 0))

    rank = jnp.zeros(sel.shape, F32)
    for k in range(1, PER_GROUP):
        other = group_rot(sel, k)
        rank = rank + jnp.where(pos + k >= PER_GROUP, jnp.where(other >= sel, 1.0, 0.0),
                                jnp.where(other > sel, 1.0, 0.0))
    top = rank < 2.0
    gsum = jnp.where(top, sel, 0.0)
    gs = gsum
    for k in range(1, PER_GROUP):
        gs = gs + group_rot(gsum, k)
    beaten = jnp.zeros(sel.shape, F32)
    for m in range(1, N_GROUPS):
        other = pltpu.roll(gs, PER_GROUP * m, 0)
        beaten = beaten + jnp.where(grp >= m, jnp.where(gs > other, 0.0, 1.0),
                                    jnp.where(gs >= other, 0.0, 1.0))
    chosen = jnp.where(top, beaten, 1.0) < 0.5
    denom = jnp.sum(jnp.where(chosen, aff, 0.0), axis=0, keepdims=True)
    gate = jnp.where(chosen, aff / denom, 0.0)
    posf = pos.astype(F32)
    lo = jnp.min(jnp.where(chosen, posf, float(PER_GROUP)), axis=0, keepdims=True)
    hi = jnp.max(jnp.where(chosen, posf, -1.0), axis=0, keepdims=True)
    w_lo = jnp.sum(jnp.where(posf == lo, gate, 0.0), axis=0, keepdims=True)
    w_hi = jnp.sum(jnp.where(posf == hi, gate, 0.0), axis=0, keepdims=True)
    group = jnp.max(jnp.where(chosen, grp.astype(F32), 0.0), axis=0, keepdims=True)
    pair = lo * (2 * PER_GROUP - 1 - lo) * 0.5 + hi - lo - 1.0
    return group * float(N_PAIRS) + pair, w_lo, w_hi


def _outproj_even_kernel(ya_ref, of_ref, or_ref, gate_ref, bn_ref, c_ref, x_ref, *rest):
    xres = jnp.where(pl.program_id(1) == 0, c_ref[...], x_ref[...])
    parts = []
    for h in range(B_HEADS):
        hs = slice(h * B_HEAD_DIM, (h + 1) * B_HEAD_DIM)
        o = of_ref[:, hs].astype(F32) + or_ref[:, hs].astype(F32)
        y = o * lax.rsqrt(jnp.mean(o * o, axis=-1, keepdims=True) + RMS_EPS) * bn_ref[...]
        parts.append((y * _silu(gate_ref[:, hs].astype(F32))).astype(BF16))
    _outproj_body(ya_ref[...], jnp.concatenate(parts, axis=1), xres, *rest)


def _outproj_kernel(ya_ref, yb_ref, x_ref, *rest):
    _outproj_body(ya_ref[...], yb_ref[...], x_ref[...], *rest)


def _outproj_body(ya, yb, xres, mod_ref, w_ref, lng_ref, lnb_ref, rwt_ref, rb_ref,
                  xo_ref, h_ref, info_ref, cnt_ref, run_scr, wt_scr):
    mod = mod_ref[...]
    wa = w_ref[0:ya.shape[1], :]
    wb = w_ref[ya.shape[1]:, :]
    y = (jnp.dot(ya, wa, preferred_element_type=F32)
         + jnp.dot(yb, wb, preferred_element_type=F32))
    xn = _layer_norm(DEEPNORM_ALPHA * xres + mod[2:3] * y, lng_ref[...], lnb_ref[...])
    xo_ref[...] = xn
    h = xn * (1.0 + mod[4:5]) + mod[3:4]
    d = h.shape[1]
    h_ref[:, 0:d] = h
    rw = rwt_ref[...]
    rw_hi = rw.astype(BF16)
    rw_lo = (rw - rw_hi.astype(F32)).astype(BF16)
    h_hi = h.astype(BF16)
    h_lo = (h - h_hi.astype(F32)).astype(BF16)
    nt = (((1,), (1,)), ((), ()))
    logits_t = (lax.dot_general(rw_hi, h_hi, nt, preferred_element_type=F32)
                + lax.dot_general(rw_hi, h_lo, nt, preferred_element_type=F32)
                + lax.dot_general(rw_lo, h_hi, nt, preferred_element_type=F32))
    cls, w_lo, w_hi = _route(logits_t, rb_ref[...])

    @pl.when(jnp.logical_and(pl.program_id(0) == 0, pl.program_id(1) == 0))
    def _():
        run_scr[...] = jnp.zeros_like(run_scr)

    n = cls.shape[1]
    crow = lax.broadcasted_iota(jnp.int32, (CLS_PAD, n), 0).astype(F32)
    onehot = jnp.where(crow == cls, 1.0, 0.0)
    si = lax.broadcasted_iota(jnp.int32, (n, n), 0)
    ti = lax.broadcasted_iota(jnp.int32, (n, n), 1)
    before = jnp.where(si < ti, 1.0, 0.0).astype(BF16)
    cum = jnp.dot(onehot.astype(BF16), before, preferred_element_type=F32)
    run = run_scr[...]
    rank = jnp.sum(onehot * (cum + run[:, 0:1]), axis=0, keepdims=True)
    run = run + jnp.sum(onehot, axis=1, keepdims=True)
    run_scr[...] = run
    cnt_ref[...] = run
    info_ref[...] = jnp.zeros_like(info_ref)
    info_ref[0:1, :] = cls
    info_ref[1:2, :] = rank
    wt_scr[...] = jnp.zeros_like(wt_scr)
    wt_scr[0:1, :] = w_lo
    wt_scr[1:2, :] = w_hi
    h_ref[:, d:] = wt_scr[...].T


def _outproj(ya, yb, xres, modarr, w_out, ln_g, ln_b, rwt, rbias, row_blk0, gdn=None):
    bsz, n, wa = ya.shape

    def tok(width, col_blk=0):
        return pl.BlockSpec((None, TM, width), lambda b, j: (b, j, col_blk))

    if gdn is None:
        d = xres.shape[2]
        body, mix_args = _outproj_kernel, (ya, yb, xres)
        mix_specs = [tok(wa), tok(yb.shape[2]),
                     pl.BlockSpec((None, TM, d), lambda b, j: (b, j + row_blk0, 0))]
    else:
        o_fwd, o_rev, p, b_norm = gdn
        ctx, x = xres
        d = x.shape[2]
        gate_blk = (3 * A_WIDTH + 3 * B_WIDTH) // B_WIDTH
        body, mix_args = _outproj_even_kernel, (ya, o_fwd, o_rev, p, b_norm, ctx, x)
        mix_specs = [tok(wa), tok(B_WIDTH), tok(B_WIDTH), tok(B_WIDTH, gate_blk),
                     pl.BlockSpec((1, B_HEAD_DIM), lambda b, j: (0, 0)),
                     pl.BlockSpec((None, TM, d), lambda b, j: (b, 0, 0)),
                     pl.BlockSpec((None, TM, d), lambda b, j: (b, jnp.maximum(j - 1, 0), 0))]
    return pl.pallas_call(
        body,
        grid=(bsz, n // TM),
        in_specs=mix_specs + [
            pl.BlockSpec((None, None, 6, d), lambda b, j: (b, jnp.minimum(j + row_blk0, 1), 0, 0)),
            pl.BlockSpec(w_out.shape, lambda b, j: (0, 0)),
            pl.BlockSpec((1, d), lambda b, j: (0, 0)),
            pl.BlockSpec((1, d), lambda b, j: (0, 0)),
            pl.BlockSpec((N_EXPERTS, d), lambda b, j: (0, 0)),
            pl.BlockSpec((N_EXPERTS, 1), lambda b, j: (0, 0)),
        ],
        out_specs=[
            pl.BlockSpec((None, TM, d), lambda b, j: (b, j, 0)),
            pl.BlockSpec((None, TM, d + LANES), lambda b, j: (b, j, 0)),
            pl.BlockSpec((None, None, 8, TM), lambda b, j: (b, j, 0, 0)),
            pl.BlockSpec((CLS_PAD, LANES), lambda b, j: (0, 0)),
        ],
        out_shape=[
            jax.ShapeDtypeStruct((bsz, n, d), F32),
            jax.ShapeDtypeStruct((bsz, n, d + LANES), F32),
            jax.ShapeDtypeStruct((bsz, n // TM, 8, TM), F32),
            jax.ShapeDtypeStruct((CLS_PAD, LANES), F32),
        ],
        scratch_shapes=[pltpu.VMEM((CLS_PAD, LANES), F32),
                        pltpu.VMEM((LANES, TM), F32)],
        compiler_params=_cparams(2),
        name="outproj",
    )(*mix_args, modarr, w_out, ln_g, ln_b, rwt, rbias)


SCATTER_ROWS = 512


def _row_scatter_kernel(dst_ref, src_ref, init_hbm, out_hbm, sem):
    del init_hbm
    rows = src_ref.shape[0]

    def body(g, carry):
        r0 = pl.multiple_of(g * 8, 8)
        for u in range(8):
            pltpu.make_async_copy(src_ref.at[pl.ds(r0 + u, 1)],
                                  out_hbm.at[pl.ds(dst_ref[0, r0 + u], 1)], sem).start(priority=u % 2)
        return carry

    lax.fori_loop(0, rows // 8, body, 0)
    pltpu.make_async_copy(src_ref, out_hbm.at[pl.ds(0, rows)], sem).wait()


def _row_scatter(src, dest, n_out):
    n, width = src.shape
    rows = math.gcd(n, SCATTER_ROWS)
    return pl.pallas_call(
        _row_scatter_kernel,
        grid=(n // rows,),
        in_specs=[
            pl.BlockSpec((None, 1, rows), lambda j: (j, 0, 0), memory_space=pltpu.SMEM),
            pl.BlockSpec((rows, width), lambda j: (j, 0)),
            pl.BlockSpec(memory_space=pl.ANY),
        ],
        out_specs=pl.BlockSpec(memory_space=pl.ANY),
        out_shape=jax.ShapeDtypeStruct((n_out, width), src.dtype),
        scratch_shapes=[pltpu.SemaphoreType.DMA(())],
        input_output_aliases={2: 0},
        compiler_params=_cparams(1),
        name="row_scatter",
    )(dest.reshape(n // rows, 1, rows), src, jnp.zeros((n_out, width), src.dtype))


def _moe_kernel(elo_ref, ehi_ref, nused_ref, x_ref, wg_lo, wu_lo, wd_lo, wg_hi, wu_hi, wd_hi, o_ref):
    used = pl.program_id(0) < nused_ref[0]
    d = o_ref.shape[1]

    @pl.when(used)
    def _():
        x = x_ref[:, 0:d].astype(BF16)
        acc = None
        for col, (wg, wu, wd) in enumerate(((wg_lo, wu_lo, wd_lo), (wg_hi, wu_hi, wd_hi))):
            gate = jnp.dot(x, wg[...], preferred_element_type=F32)
            up = jnp.dot(x, wu[...], preferred_element_type=F32)
            act = (_silu(gate) * up).astype(BF16)
            y = x_ref[:, d + col:d + col + 1] * jnp.dot(act, wd[...], preferred_element_type=F32)
            acc = y if acc is None else acc + y
        o_ref[...] = acc

    @pl.when(jnp.logical_not(used))
    def _():
        o_ref[...] = jnp.zeros_like(o_ref)


def _moe_plan(info, counts, n_tok):
    mt = MOE_TM
    n_tiles = n_tok // mt + N_CLASSES
    cls = info[:, :, 0, :].reshape(-1).astype(jnp.int32)
    rank = info[:, :, 1, :].reshape(-1).astype(jnp.int32)
    cnt = counts[:N_CLASSES, 0].astype(jnp.int32)
    padded = ((cnt + mt - 1) // mt) * mt
    ends = jnp.cumsum(padded)
    starts = ends - padded
    classes = jnp.arange(N_CLASSES, dtype=jnp.int32)
    dest = jnp.sum(jnp.where(cls[:, None] == classes[None, :], starts[None, :], 0), axis=1) + rank
    n_used = ends[-1] // mt
    tidx = jnp.arange(n_tiles, dtype=jnp.int32)
    tidx = jnp.minimum(tidx, n_used - 1)
    tcls = jnp.sum((ends[None, :] <= (tidx * mt)[:, None]).astype(jnp.int32), axis=1)
    tcls = jnp.minimum(tcls, N_CLASSES - 1)
    pairs = [(a, b) for a in range(PER_GROUP) for b in range(a + 1, PER_GROUP)]
    pair = tcls % N_PAIRS
    lo = sum(jnp.where(pair == k, a, 0) for k, (a, _) in enumerate(pairs))
    hi = sum(jnp.where(pair == k, b, 0) for k, (_, b) in enumerate(pairs))
    group = tcls // N_PAIRS
    return (group * PER_GROUP + lo, group * PER_GROUP + hi, n_used.reshape(1).astype(jnp.int32),
            dest.astype(jnp.int32))


def _moe(h_ext, info, counts, wg, wu, wd):
    n, width = h_ext.shape
    d = width - LANES
    ne, _, de = wg.shape
    mt = MOE_TM
    elo, ehi, n_used, dest = _moe_plan(info, counts, n)
    n_tiles = n // mt + N_CLASSES
    h_sorted = _row_scatter(h_ext, dest, n_tiles * mt)

    def expert(which, shape):
        if which == 0:
            return pl.BlockSpec((None,) + shape, lambda i, lo, hi, nu: (lo[i], 0, 0))
        return pl.BlockSpec((None,) + shape, lambda i, lo, hi, nu: (hi[i], 0, 0))

    grid_spec = pltpu.PrefetchScalarGridSpec(
        num_scalar_prefetch=3,
        grid=(n_tiles,),
        in_specs=[
            pl.BlockSpec((mt, width), lambda i, lo, hi, nu: (jnp.minimum(i, nu[0] - 1), 0)),
            expert(0, (d, de)), expert(0, (d, de)), expert(0, (de, d)),
            expert(1, (d, de)), expert(1, (d, de)), expert(1, (de, d)),
        ],
        out_specs=pl.BlockSpec((mt, d), lambda i, *_: (i, 0)),
    )
    f_sorted = pl.pallas_call(
        _moe_kernel,
        grid_spec=grid_spec,
        out_shape=jax.ShapeDtypeStruct((n_tiles * mt, d), F32),
        compiler_params=_cparams(1),
        name="moe",
    )(elo, ehi, n_used, h_sorted, wg, wu, wd, wg, wu, wd)
    return f_sorted, dest


def _ln2_kernel(cur_ref, nxt_ref, x_ref, f_hbm, mod_ref, lng_ref, lnb_ref, o_ref, fbuf, sem):
    o_ref[...] = _gathered_ln2(cur_ref, nxt_ref, x_ref, f_hbm, mod_ref, lng_ref, lnb_ref, fbuf, sem)


def _gathered_ln2(cur_ref, nxt_ref, x_ref, f_hbm, mod_ref, lng_ref, lnb_ref, fbuf, sem):
    nj = pl.num_programs(1)
    step = pl.program_id(0) * nj + pl.program_id(1)
    n_steps = pl.num_programs(0) * nj
    slot = step % 2
    rows = fbuf.shape[1]

    def gather_start(idx_ref, s):
        def body(g, carry):
            r0 = pl.multiple_of(g * 8, 8)
            for u in range(8):
                pltpu.make_async_copy(f_hbm.at[pl.ds(idx_ref[0, r0 + u], 1)],
                                      fbuf.at[s, pl.ds(r0 + u, 1)], sem.at[s]).start(priority=u % 2)
            return carry
        lax.fori_loop(0, rows // 8, body, 0)

    @pl.when(step == 0)
    def _():
        gather_start(cur_ref, 0)

    def slot_wait(s):
        pltpu.make_async_copy(f_hbm.at[pl.ds(0, rows)], fbuf.at[s], sem.at[s]).wait()

    slot_wait(slot)
    for r in range(rows):
        pltpu.make_async_copy(f_hbm.at[pl.ds(nxt_ref[0, r], 1)], fbuf.at[1 - slot, pl.ds(r, 1)],
                              sem.at[1 - slot]).start(priority=r % 2)
    mod = mod_ref[...]
    v = DEEPNORM_ALPHA * x_ref[...] + mod[5:6] * fbuf[slot]
    out = _layer_norm(v, lng_ref[...], lnb_ref[...])

    @pl.when(step == n_steps - 1)
    def _():
        slot_wait(1 - slot)

    return out


def _ln2(x, f_sorted, dest, modarr, ln_g, ln_b, kind0):
    bsz, n, d = x.shape
    nj = n // TM
    n_steps = bsz * nj

    def idx_rows(offset):
        return pl.BlockSpec((None, 1, TM), lambda b, j: (jnp.minimum(b * nj + j + offset, n_steps - 1), 0, 0),
                            memory_space=pltpu.SMEM)

    dest3 = dest.reshape(n_steps, 1, TM)
    return pl.pallas_call(
        _ln2_kernel,
        grid=(bsz, nj),
        in_specs=[
            idx_rows(0), idx_rows(1),
            pl.BlockSpec((None, TM, d), lambda b, j: (b, j, 0)),
            pl.BlockSpec(memory_space=pl.ANY),
            pl.BlockSpec((None, None, 6, d), lambda b, j: (b, jnp.minimum(j + kind0, 1), 0, 0)),
            pl.BlockSpec((1, d), lambda b, j: (0, 0)),
            pl.BlockSpec((1, d), lambda b, j: (0, 0)),
        ],
        out_specs=pl.BlockSpec((None, TM, d), lambda b, j: (b, j, 0)),
        out_shape=jax.ShapeDtypeStruct((bsz, n, d), F32),
        scratch_shapes=[pltpu.VMEM((2, TM, d), F32), pltpu.SemaphoreType.DMA((2,))],
        compiler_params=_cparams(2),
        name="ln2",
    )(dest3, dest3, x, f_sorted, modarr, ln_g, ln_b)


def _rope(x, c, s1, s2, shift):
    w = x.shape[1]
    return x * c + pltpu.roll(x, w - shift, 1) * s1 + pltpu.roll(x, shift, 1) * s2


def _rms(x, g):
    return x * lax.rsqrt(jnp.mean(x * x, axis=-1, keepdims=True) + RMS_EPS) * g


def _ln2_inproj_odd_kernel(cur_ref, nxt_ref, x1_ref, f_hbm, mod0_ref, lng_ref, lnb_ref,
                           mod_ref, w_ref, qn_ref, kvn_ref, wuq_ref, wk_ref, we_ref, wv_ref,
                           tw_ref, tq_ref, tk_ref,
                           x2_ref, qw_ref, kw_ref, vw_ref, qm_ref, km_ref, vm_ref, fbuf, sem):
    x2 = _gathered_ln2(cur_ref, nxt_ref, x1_ref, f_hbm, mod0_ref, lng_ref, lnb_ref, fbuf, sem)
    x2_ref[...] = x2
    j = pl.program_id(1)
    is_ctx = j == 0
    mod = mod_ref[...]
    h = (x2 * (1.0 + mod[1:2]) + mod[0:1]).astype(BF16)
    p = jnp.dot(h, w_ref[...], preferred_element_type=F32)

    def tables(t_ref):
        c = jnp.where(is_ctx, 1.0, t_ref[0])
        s1 = jnp.where(is_ctx, 0.0, t_ref[1])
        s2 = jnp.where(is_ctx, 0.0, t_ref[2])
        return c, s1, s2

    cw, s1w, s2w = tables(tw_ref)
    nq = C_Q_HEADS * C_HEAD_DIM
    for r in range(nq // LANES):
        blk = _rope(p[:, r * LANES:(r + 1) * LANES], cw, s1w, s2w, C_HEAD_DIM // 2)
        qw_ref[:, r * LANES:(r + 1) * LANES] = (blk * (C_HEAD_DIM ** -0.5)).astype(BF16)
    kw_ref[...] = _rope(p[:, nq:nq + LANES], cw, s1w, s2w, C_HEAD_DIM // 2).astype(BF16)
    vw_ref[...] = p[:, nq + LANES:nq + 2 * LANES].astype(BF16)

    o = nq + 2 * LANES
    dq = _rms(p[:, o:o + D_Q_RANK], qn_ref[...]).astype(BF16)
    o += D_Q_RANK
    dkv = _rms(p[:, o:o + D_KV_RANK], kvn_ref[...]).astype(BF16)
    o += D_KV_RANK
    cq, s1q, s2q = tables(tq_ref)
    ck, s1k, s2k = tables(tk_ref)
    krope = _rope(p[:, o:o + LANES], ck, s1k, s2k, D_ROPE // 2).astype(BF16)
    scale = (D_NOPE + D_ROPE) ** -0.5
    q_all = jnp.dot(dq, wuq_ref[...], preferred_element_type=F32)
    for hh in range(D_HEADS):
        sl = slice(hh * LANES, (hh + 1) * LANES)
        qm_ref[:, sl] = (_rope(q_all[:, sl], cq, s1q, s2q, D_ROPE // 2) * scale).astype(BF16)
    km_ref[...] = (jnp.dot(dkv, wk_ref[...], preferred_element_type=F32)
                   + jnp.dot(krope, we_ref[...], preferred_element_type=F32)).astype(BF16)
    vm_ref[...] = jnp.dot(dkv, wv_ref[...], preferred_element_type=F32).astype(BF16)


def _ln2_inproj_odd(x1, f_sorted, dest, mod0, ln_g, ln_b, modarr, w1, qnorm, kvnorm, wuq, wk, we, wv,
                    tab_w, tab_q, tab_k):
    bsz, t, d = x1.shape
    hw = D_HEADS * LANES
    nj = t // TM
    n_steps = bsz * nj

    def idx_rows(offset):
        return pl.BlockSpec((None, 1, TM), lambda b, j: (jnp.minimum(b * nj + j + offset, n_steps - 1), 0, 0),
                            memory_space=pltpu.SMEM)

    dest3 = dest.reshape(n_steps, 1, TM)

    def tab_spec():
        return pl.BlockSpec((3, TM, LANES), lambda b, j: (0, jnp.maximum(j - 1, 0), 0))

    def full(a):
        return pl.BlockSpec(a.shape, lambda b, j: (0,) * a.ndim)

    def out(width):
        return (pl.BlockSpec((None, TM, width), lambda b, j: (b, j, 0)),
                jax.ShapeDtypeStruct((bsz, t, width), BF16))

    def out_latent(width):
        return (pl.BlockSpec((None, TM, width), lambda b, j: (b, jnp.maximum(j - 1, 0), 0)),
                jax.ShapeDtypeStruct((bsz, t - TM, width), BF16))

    x2_out = (pl.BlockSpec((None, TM, d), lambda b, j: (b, j, 0)), jax.ShapeDtypeStruct((bsz, t, d), F32))
    outs = (x2_out, out_latent(C_Q_HEADS * C_HEAD_DIM), out(LANES), out(LANES), out_latent(hw),
            out(hw), out(hw))

    def mod_spec():
        return pl.BlockSpec((None, None, 6, d), lambda b, j: (b, jnp.minimum(j, 1), 0, 0))

    return pl.pallas_call(
        _ln2_inproj_odd_kernel,
        grid=(bsz, nj),
        in_specs=[
            idx_rows(0), idx_rows(1),
            pl.BlockSpec((None, TM, d), lambda b, j: (b, j, 0)),
            pl.BlockSpec(memory_space=pl.ANY),
            mod_spec(), full(ln_g), full(ln_b), mod_spec(),
            full(w1), full(qnorm), full(kvnorm), full(wuq), full(wk), full(we), full(wv),
            tab_spec(), tab_spec(), tab_spec(),
        ],
        out_specs=[o[0] for o in outs],
        out_shape=[o[1] for o in outs],
        scratch_shapes=[pltpu.VMEM((2, TM, d), F32), pltpu.SemaphoreType.DMA((2,))],
        compiler_params=_cparams(2),
        name="ln2_inproj_odd",
    )(dest3, dest3, x1, f_sorted, mod0, ln_g, ln_b, modarr, w1, qnorm, kvnorm, wuq, wk, we, wv,
      tab_w, tab_q, tab_k)


WIN_TQ = 256


def _win_kernel(sink_ref, q_ref, k_ref, v_ref, o_ref, klo_scr, khi_scr, *, n_ctx):
    i = pl.program_id(1)
    wdw = C_WINDOW
    t = k_ref.shape[0]
    lane = lax.broadcasted_iota(jnp.int32, (t, LANES), 1)

    @pl.when(i == 0)
    def _():
        kk = k_ref[...]
        klo_scr[...] = jnp.where(lane < C_HEAD_DIM, kk, jnp.zeros_like(kk))
        khi_scr[...] = jnp.where(lane >= C_HEAD_DIM, kk, jnp.zeros_like(kk))

    tq = q_ref.shape[0]
    span = tq + 2 * wdw
    n_lat_blk = (t - n_ctx) // wdw
    blk0 = jnp.clip(i * (tq // wdw) - 1, 0, n_lat_blk - span // wdw)
    r0 = pl.multiple_of(n_ctx + blk0 * wdw, wdw)
    kpos = blk0 * wdw + lax.broadcasted_iota(jnp.int32, (tq, span), 1)
    qpos = i * tq + lax.broadcasted_iota(jnp.int32, (tq, span), 0)
    near = jnp.abs(kpos - qpos) <= wdw
    v_loc = v_ref[pl.ds(r0, span), :]
    v_ctx = v_ref[0:n_ctx, :]
    olane = lax.broadcasted_iota(jnp.int32, (tq, LANES), 1)
    n_rep = C_Q_HEADS // C_KV_HEADS
    for r in range(n_rep):
        q = q_ref[:, r * LANES:(r + 1) * LANES]
        outs = []
        for g, k_scr in enumerate((klo_scr, khi_scr)):
            k_loc = k_scr[pl.ds(r0, span), :]
            k_ctx = k_scr[0:n_ctx, :]
            s_loc = lax.dot_general(q, k_loc, (((1,), (1,)), ((), ())), preferred_element_type=F32)
            s_loc = jnp.where(near, s_loc, -jnp.inf)
            s_ctx = lax.dot_general(q, k_ctx, (((1,), (1,)), ((), ())), preferred_element_type=F32)
            sink = sink_ref[g * n_rep + r]
            m = jnp.maximum(jnp.maximum(jnp.max(s_loc, axis=-1, keepdims=True),
                                        jnp.max(s_ctx, axis=-1, keepdims=True)), sink)
            p_loc = jnp.exp(s_loc - m)
            p_ctx = jnp.exp(s_ctx - m)
            den = (jnp.sum(p_loc, axis=-1, keepdims=True) + jnp.sum(p_ctx, axis=-1, keepdims=True)
                   + jnp.exp(sink - m))
            pv = (jnp.dot(p_loc.astype(BF16), v_loc, preferred_element_type=F32)
                  + jnp.dot(p_ctx.astype(BF16), v_ctx, preferred_element_type=F32))
            outs.append(pv / den)
        o_ref[:, r * LANES:(r + 1) * LANES] = jnp.where(olane < C_HEAD_DIM, outs[0], outs[1]).astype(BF16)


def _win_attention(sink, qw, kw, vw, n_ctx):
    bsz, n_lat, nq = qw.shape
    t = n_ctx + n_lat
    grid_spec = pltpu.PrefetchScalarGridSpec(
        num_scalar_prefetch=1,
        grid=(bsz, n_lat // WIN_TQ),
        in_specs=[
            pl.BlockSpec((None, WIN_TQ, nq), lambda b, i, s: (b, i, 0)),
            pl.BlockSpec((None, t, LANES), lambda b, i, s: (b, 0, 0)),
            pl.BlockSpec((None, t, LANES), lambda b, i, s: (b, 0, 0)),
        ],
        out_specs=pl.BlockSpec((None, WIN_TQ, nq), lambda b, i, s: (b, i, 0)),
        scratch_shapes=[pltpu.VMEM((t, LANES), BF16), pltpu.VMEM((t, LANES), BF16)],
    )
    return pl.pallas_call(
        functools.partial(_win_kernel, n_ctx=n_ctx),
        grid_spec=grid_spec,
        out_shape=jax.ShapeDtypeStruct((bsz, n_lat, nq), BF16),
        compiler_params=_cparams(2),
        name="win_attention",
    )(sink, qw, kw, vw)


MLA_TQ = 512


def _mla_kernel(q_ref, k_ref, v_ref, o_ref):
    for hp in range(D_HEADS // 2):
        acc = None
        for hh in (2 * hp, 2 * hp + 1):
            sl = slice(hh * LANES, (hh + 1) * LANES)
            s = lax.dot_general(q_ref[:, sl], k_ref[:, sl], (((1,), (1,)), ((), ())),
                                preferred_element_type=F32)
            m = jnp.max(s, axis=-1, keepdims=True)
            p = jnp.exp(s - m)
            den = jnp.sum(p, axis=-1, keepdims=True)
            pv = jnp.dot(p.astype(BF16), v_ref[:, sl], preferred_element_type=F32) / den
            acc = pv if acc is None else acc + pv
        o_ref[:, hp * LANES:(hp + 1) * LANES] = acc.astype(BF16)


def _mla_attention(qm, km, vm, n_ctx):
    bsz, n_lat, hw = qm.shape
    t = n_ctx + n_lat
    ow = D_HEADS * D_V
    return pl.pallas_call(
        _mla_kernel,
        grid=(bsz, n_lat // MLA_TQ),
        in_specs=[
            pl.BlockSpec((None, MLA_TQ, hw), lambda b, i: (b, i, 0)),
            pl.BlockSpec((None, t, hw), lambda b, i: (b, 0, 0)),
            pl.BlockSpec((None, t, hw), lambda b, i: (b, 0, 0)),
        ],
        out_specs=pl.BlockSpec((None, MLA_TQ, ow), lambda b, i: (b, i, 0)),
        out_shape=jax.ShapeDtypeStruct((bsz, n_lat, ow), BF16),
        compiler_params=_cparams(2),
        name="mla_attention",
    )(qm, km, vm)


def _rope_tables(n_tokens, rot_dim, group, offset):
    t = jnp.arange(n_tokens)
    rows = (t // GRID_W).astype(F32)
    cols = (t % GRID_W).astype(F32)
    n_freq = rot_dim // 4
    inv_freq = ROPE_BASE ** (-jnp.arange(n_freq, dtype=F32) / n_freq)
    ang = jnp.concatenate([rows[:, None] * inv_freq, cols[:, None] * inv_freq], -1)
    cos, sin = jnp.cos(ang), jnp.sin(ang)
    half = rot_dim // 2
    c = jnp.ones((n_tokens, LANES), F32)
    s1 = jnp.zeros((n_tokens, LANES), F32)
    s2 = jnp.zeros((n_tokens, LANES), F32)
    for start in range(offset, LANES, group):
        c = c.at[:, start:start + half].set(cos).at[:, start + half:start + rot_dim].set(cos)
        s1 = s1.at[:, start:start + half].set(-sin)
        s2 = s2.at[:, start + half:start + rot_dim].set(sin)
    return jnp.stack([c, s1, s2])


def _odd_weights(w_in, wuq, wukv, w_out):
    d = w_in.shape[0]
    nq = C_Q_HEADS * C_HEAD_DIM
    nkv = C_KV_HEADS * C_HEAD_DIM
    n_rep = C_Q_HEADS // C_KV_HEADS
    order = [g * n_rep + r for r in range(n_rep) for g in range(C_KV_HEADS)]
    cq = w_in[:, :nq].reshape(d, C_Q_HEADS, C_HEAD_DIM)[:, order].reshape(d, nq)
    rest = w_in[:, nq:nq + 2 * nkv + D_Q_RANK + D_KV_RANK]
    krope = jnp.pad(w_in[:, nq + 2 * nkv + D_Q_RANK + D_KV_RANK:], ((0, 0), (0, LANES - D_ROPE)))
    w1 = jnp.concatenate([cq, rest, krope], axis=1).astype(BF16)
    qh = wuq.reshape(D_Q_RANK, D_HEADS, D_NOPE + D_ROPE)
    wuq_p = jnp.pad(qh, ((0, 0), (0, 0), (0, LANES - D_NOPE - D_ROPE))).reshape(D_Q_RANK, D_HEADS * LANES)
    kvh = wukv.reshape(D_KV_RANK, D_HEADS, D_NOPE + D_V)
    wk_p = jnp.pad(kvh[:, :, :D_NOPE], ((0, 0), (0, 0), (0, LANES - D_NOPE))).reshape(D_KV_RANK, D_HEADS * LANES)
    e_blk = jnp.zeros((LANES, LANES), F32).at[jnp.arange(D_ROPE), D_NOPE + jnp.arange(D_ROPE)].set(1.0)
    we = jnp.tile(e_blk, (1, D_HEADS))
    vh = kvh[:, :, D_NOPE:]
    even = (jnp.arange(D_HEADS) % 2 == 0)[None, :, None]
    wv_p = jnp.where(even, jnp.pad(vh, ((0, 0), (0, 0), (0, D_V))),
                     jnp.pad(vh, ((0, 0), (0, 0), (D_V, 0)))).reshape(D_KV_RANK, D_HEADS * LANES)
    wo_win = w_out[:nq].reshape(C_Q_HEADS, C_HEAD_DIM, -1)[jnp.array(order)].reshape(nq, -1)
    wo = jnp.concatenate([wo_win, w_out[nq:]], axis=0).astype(BF16)
    return w1, wuq_p.astype(BF16), wk_p.astype(BF16), we.astype(BF16), wv_p.astype(BF16), wo


def kernel(x, c, ctx, c_ctx, ada_w, ada_b, ln_g, ln_b, ev_w_in, ev_a_conv, ev_b_conv, ev_b_alog, ev_b_dtbias, ev_b_norm, ev_w_out, od_w_in, od_c_sink, od_d_qnorm, od_d_kvnorm, od_d_wuq, od_d_wukv, od_w_out, router_w, router_bias, moe_w_gate, moe_w_up, moe_w_down):
    bsz, n_lat, d = x.shape
    n_ctx = ctx.shape[1]
    assert n_ctx == TM and n_lat % TM == 0 and n_lat % GRID_W == 0
    assert ada_w.shape[0] == DEPTH and bsz + 1 <= 40
    t = n_ctx + n_lat

    cs = jnp.zeros((40, d), F32).at[:bsz].set(c).at[bsz].set(c_ctx)
    mods = _ada_mod(cs, ada_w, ada_b)

    def modarr(layer):
        m = mods[layer].reshape(40, 6, d)
        return jnp.stack([jnp.broadcast_to(m[bsz], (bsz, 6, d)), m[:bsz]], axis=1)

    rwt = router_w.T
    rbias = router_bias.reshape(N_EXPERTS, 1)

    mod0 = modarr(0)
    n_main = 3 * A_WIDTH + 4 * B_WIDTH
    w_main = ev_w_in[0][:, :n_main].astype(BF16)
    w_small = jnp.pad(ev_w_in[0][:, n_main:], ((0, 0), (0, LANES - 4 * B_HEADS))).astype(BF16)
    p, small = _inproj_even(ctx, x, mod0, w_main, w_small)
    alog_pad = jnp.zeros((1, LANES), F32).at[0, 8:16].set(ev_b_alog[0].reshape(-1))
    dtb_pad = jnp.zeros((1, LANES), F32).at[0, 8:16].set(ev_b_dtbias[0].reshape(-1))
    ya, u, w, qe, ket, att, dec = _even_prep(p, small, ev_a_conv[0], ev_b_conv[0], alog_pad, dtb_pad,
                                             GDN_CHUNK)
    o_fwd, o_rev = _gdn_rec(u, w, qe, ket, att, dec, GDN_CHUNK)
    x1, h1, info0, cnt0 = _outproj(ya, None, (ctx, x), mod0, ev_w_out[0].astype(BF16),
                             ln_g[0, 0].reshape(1, d), ln_b[0, 0].reshape(1, d), rwt, rbias, 0,
                             gdn=(o_fwd, o_rev, p, ev_b_norm[0].reshape(1, B_HEAD_DIM)))
    f, dest = _moe(h1.reshape(bsz * t, d + LANES), info0, cnt0,
                   moe_w_gate[0].astype(BF16), moe_w_up[0].astype(BF16), moe_w_down[0].astype(BF16))
    mod1 = modarr(1)
    w1, wuq_p, wk_p, we, wv_p, wo = _odd_weights(od_w_in[0], od_d_wuq[0], od_d_wukv[0], od_w_out[0])
    tab_w = _rope_tables(n_lat, C_HEAD_DIM, C_HEAD_DIM, 0)
    tab_q = _rope_tables(n_lat, D_ROPE, LANES, D_NOPE)
    tab_k = _rope_tables(n_lat, D_ROPE, LANES, 0)
    x2, qw, kw, vw, qm, km, vm = _ln2_inproj_odd(
        x1, f, dest, mod0, ln_g[0, 1].reshape(1, d), ln_b[0, 1].reshape(1, d),
        mod1, w1, od_d_qnorm[0].reshape(1, -1), od_d_kvnorm[0].reshape(1, -1),
        wuq_p, wk_p, we, wv_p, tab_w, tab_q, tab_k)
    y_win = _win_attention(od_c_sink[0], qw, kw, vw, n_ctx)
    y_mla = _mla_attention(qm, km, vm, n_ctx)
    x3, h3, info1, cnt1 = _outproj(y_win, y_mla, x2, mod1, wo, ln_g[1, 0].reshape(1, d),
                                   ln_b[1, 0].reshape(1, d), rwt, rbias, n_ctx // TM)
    f1, dest1 = _moe(h3.reshape(bsz * n_lat, d + LANES), info1, cnt1,
                     moe_w_gate[1].astype(BF16), moe_w_up[1].astype(BF16), moe_w_down[1].astype(BF16))
    return _ln2(x3, f1, dest1, mod1, ln_g[1, 1].reshape(1, d), ln_b[1, 1].reshape(1, d), 1)
```

```python
import functools
import math

import numpy as np
import jax
import jax.numpy as jnp
from jax import lax
from jax.experimental import pallas as pl
from jax.experimental.pallas import tpu as pltpu

F32 = jnp.float32
BF16 = jnp.bfloat16
HIGHEST = lax.Precision.HIGHEST

DEPTH = 2
GRID_W = 64
DEEPNORM_ALPHA = (2.0 * DEPTH) ** 0.25
LN_EPS = 1e-5
RMS_EPS = 1e-6
ROPE_BASE = 10000.0
B_HEADS = 4
B_HEAD_DIM = 128
B_WIDTH = 512
A_WIDTH = 512
C_Q_HEADS = 8
C_KV_HEADS = 2
C_HEAD_DIM = 64
C_WINDOW = 128
D_HEADS = 8
D_NOPE = 64
D_ROPE = 32
D_V = 64
D_Q_RANK = 384
D_KV_RANK = 256
N_EXPERTS = 16
N_GROUPS = 4
PER_GROUP = N_EXPERTS // N_GROUPS
D_EXPERT = 512
N_PAIRS = PER_GROUP * (PER_GROUP - 1) // 2
N_CLASSES = N_GROUPS * N_PAIRS
CLS_PAD = 32

LANES = 128
TM = 256
GDN_CHUNK = 64
MOE_TM = 256
VMEM_LIMIT = 56 * 1024 * 1024


def _cparams(n_axes, vmem=VMEM_LIMIT):
    return pltpu.CompilerParams(dimension_semantics=("arbitrary",) * n_axes, vmem_limit_bytes=vmem)


def _sigmoid(x):
    return 1.0 / (1.0 + jnp.exp(-x))


def _silu(x):
    return x * _sigmoid(x)


def _softplus(x):
    return jnp.maximum(x, 0.0) + jnp.log(1.0 + jnp.exp(-jnp.abs(x)))


def _layer_norm(v, g, b):
    mu = jnp.mean(v, axis=-1, keepdims=True)
    d = v - mu
    var = jnp.mean(d * d, axis=-1, keepdims=True)
    return d * lax.rsqrt(var + LN_EPS) * g + b


def _ada_kernel(c_ref, w_ref, b_ref, o_ref):
    s = _silu(c_ref[...])
    o_ref[...] = jnp.dot(s, w_ref[...], precision=HIGHEST, preferred_element_type=F32) + b_ref[...]


def _ada_mod(cs, ada_w, ada_b):
    depth, d, n6 = ada_w.shape
    rows = cs.shape[0]
    tn = 1536
    return pl.pallas_call(
        _ada_kernel,
        grid=(depth, n6 // tn),
        in_specs=[
            pl.BlockSpec((rows, d), lambda l, n: (0, 0)),
            pl.BlockSpec((None, d, tn), lambda l, n: (l, 0, n)),
            pl.BlockSpec((None, 1, tn), lambda l, n: (l, 0, n)),
        ],
        out_specs=pl.BlockSpec((None, rows, tn), lambda l, n: (l, 0, n)),
        out_shape=jax.ShapeDtypeStruct((depth, rows, n6), F32),
        compiler_params=_cparams(2),
        name="ada_mod",
    )(cs, ada_w, ada_b.reshape(depth, 1, n6))


def _inproj_even_kernel(c_ref, x_ref, mod_ref, wm_ref, ws_ref, p_ref, s_ref):
    mod = mod_ref[...]
    xv = jnp.where(pl.program_id(1) == 0, c_ref[...], x_ref[...])
    h = (xv * (1.0 + mod[1:2]) + mod[0:1]).astype(BF16)
    p_ref[...] = jnp.dot(h, wm_ref[...], preferred_element_type=F32).astype(BF16)
    s_ref[...] = jnp.dot(h, ws_ref[...], preferred_element_type=F32)


def _inproj_even(ctx, x, modarr, w_main, w_small):
    bsz, n_lat, d = x.shape
    t = ctx.shape[1] + n_lat
    nm = w_main.shape[1]
    return pl.pallas_call(
        _inproj_even_kernel,
        grid=(bsz, t // TM),
        in_specs=[
            pl.BlockSpec((None, TM, d), lambda b, j: (b, 0, 0)),
            pl.BlockSpec((None, TM, d), lambda b, j: (b, jnp.maximum(j - 1, 0), 0)),
            pl.BlockSpec((None, None, 6, d), lambda b, j: (b, jnp.minimum(j, 1), 0, 0)),
            pl.BlockSpec((d, nm), lambda b, j: (0, 0)),
            pl.BlockSpec((d, LANES), lambda b, j: (0, 0)),
        ],
        out_specs=[
            pl.BlockSpec((None, TM, nm), lambda b, j: (b, j, 0)),
            pl.BlockSpec((None, TM, LANES), lambda b, j: (b, j, 0)),
        ],
        out_shape=[
            jax.ShapeDtypeStruct((bsz, t, nm), BF16),
            jax.ShapeDtypeStruct((bsz, t, LANES), F32),
        ],
        compiler_params=_cparams(2),
        name="inproj_even",
    )(ctx, x, modarr, w_main, w_small)


HALO = 16


def _conv3(z, zp, zn, w):
    n = z.shape[0]
    rows = lax.broadcasted_iota(jnp.int32, z.shape, 0)
    zprev = jnp.where(rows == 0, zp, pltpu.roll(z, 1, 0))
    znext = jnp.where(rows == n - 1, zn, pltpu.roll(z, n - 1, 0))
    return w[0:1] * zprev + w[1:2] * z + w[2:3] * znext


def _even_prep_kernel(p_ref, pp_ref, pn_ref, s_ref, aw_ref, bw_ref, alog_ref, dtb_ref,
                      ya_ref, u_ref, w_ref, qe_ref, ket_ref, att_ref, dec_ref,
                      q_scr, k_scr, v_scr, *, chunk):
    j = pl.program_id(1)
    nj = pl.num_programs(1)
    prev_on = jnp.where(jnp.logical_and(j != 0, j != 1), 1.0, 0.0)
    next_on = jnp.where(jnp.logical_and(j != 0, j != nj - 1), 1.0, 0.0)
    prow = pp_ref[...].astype(F32)[HALO - 1:HALO] * prev_on
    nrow = pn_ref[...].astype(F32)[0:1] * next_on

    def seg(lo, hi):
        return p_ref[:, lo:hi].astype(F32), prow[:, lo:hi], nrow[:, lo:hi]

    a0, _, _ = seg(0, A_WIDTH)
    a1, a1p, a1n = seg(A_WIDTH, 2 * A_WIDTH)
    a2, a2p, a2n = seg(2 * A_WIDTH, 3 * A_WIDTH)
    ya_ref[...] = (a0 * _conv3(a1 * a2, a1p * a2p, a1n * a2n, aw_ref[...])).astype(BF16)

    base = 3 * A_WIDTH
    for which in range(3):
        lo = base + which * B_WIDTH
        z, zp, zn = seg(lo, lo + B_WIDTH)
        c = _silu(_conv3(z, zp, zn, bw_ref[:, which * B_WIDTH:(which + 1) * B_WIDTH]))
        for h in range(B_HEADS):
            ch = c[:, h * B_HEAD_DIM:(h + 1) * B_HEAD_DIM]
            if which < 2:
                ss = jnp.sum(ch * ch, axis=-1, keepdims=True)
                ch = ch * lax.rsqrt(ss + 1e-6)
                if which == 0:
                    ch = ch * (B_HEAD_DIM ** -0.5)
            sl = slice(h * B_HEAD_DIM, (h + 1) * B_HEAD_DIM)
            if which == 0:
                q_scr[:, sl] = ch.astype(BF16)
            elif which == 1:
                k_scr[:, sl] = ch.astype(BF16)
            else:
                v_scr[:, sl] = ch

    s = s_ref[...]
    beta = _sigmoid(s)
    g = -jnp.exp(alog_ref[...]) * _softplus(s + dtb_ref[...])
    n = s.shape[0]
    nck = n // chunk
    ri = lax.broadcasted_iota(jnp.int32, (n, n), 0)
    ci = lax.broadcasted_iota(jnp.int32, (n, n), 1)
    same = (ri // chunk) == (ci // chunk)
    m_fwd = jnp.where(jnp.logical_and(same, ci <= ri), 1.0, 0.0).astype(BF16)
    m_rev = jnp.where(jnp.logical_and(same, ci >= ri), 1.0, 0.0).astype(BF16)
    g1 = g.astype(BF16)
    r1 = g - g1.astype(F32)
    g2 = r1.astype(BF16)
    g3 = (r1 - g2.astype(F32)).astype(BF16)
    g_split = jnp.concatenate([g1, g2, g3], axis=1)

    def cumulate(m):
        parts = jnp.dot(m, g_split, preferred_element_type=F32)
        return parts[:, 0:LANES] + parts[:, LANES:2 * LANES] + parts[:, 2 * LANES:]

    gc_f = cumulate(m_fwd)
    gc_r = cumulate(m_rev)
    tot = gc_f + gc_r - g
    lane = lax.broadcasted_iota(jnp.int32, s.shape, 1)
    gc = jnp.where(lane >= 8 + B_HEADS, gc_r, gc_f)
    e_gc = jnp.exp(gc)
    e_rest = jnp.exp(tot - gc)
    gct = gc.T
    e_tot = jnp.exp(tot)

    ri = lax.broadcasted_iota(jnp.int32, (1, chunk, chunk), 1)
    ci = lax.broadcasted_iota(jnp.int32, (1, chunk, chunk), 2)
    eye = jnp.where(ri == ci, 1.0, 0.0)
    n_sq = int(np.log2(chunk))
    nt_batched = (((2,), (2,)), ((0,), (0,)))
    nn_batched = (((2,), (1,)), ((0,), (0,)))
    kk, qk, kf, qf = [], [], [], []
    for h in range(B_HEADS):
        hs = slice(h * B_HEAD_DIM, (h + 1) * B_HEAD_DIM)
        k3 = k_scr[:, hs].reshape(nck, chunk, B_HEAD_DIM)
        q3 = q_scr[:, hs].reshape(nck, chunk, B_HEAD_DIM)
        kk.append(lax.dot_general(k3, k3, nt_batched, preferred_element_type=F32))
        qk.append(lax.dot_general(q3, k3, nt_batched, preferred_element_type=F32))
        kf.append(k_scr[:, hs].astype(F32))
        qf.append(q_scr[:, hs].astype(F32))
    a_all, rhs_all = [], []
    for d in range(2):
        incl = (ci <= ri) if d == 0 else (ci >= ri)
        strict = (ci < ri) if d == 0 else (ci > ri)
        for h in range(B_HEADS):
            chain = d * B_HEADS + h
            hs = slice(h * B_HEAD_DIM, (h + 1) * B_HEAD_DIM)
            bcol = beta[:, chain:chain + 1]
            e1 = e_gc[:, 8 + chain:9 + chain]
            e2 = e_rest[:, 8 + chain:9 + chain]
            gcol = gc[:, 8 + chain:9 + chain].reshape(nck, chunk, 1)
            grow = jnp.stack([gct[8 + chain:9 + chain, cc * chunk:(cc + 1) * chunk]
                              for cc in range(nck)], axis=0)
            decay = jnp.exp(jnp.where(incl, gcol - grow, -jnp.inf))
            a_all.append(jnp.where(strict, bcol.reshape(nck, chunk, 1) * kk[h] * decay, 0.0))
            att_ref[d, h] = (qk[h] * decay).reshape(n, chunk).astype(BF16)
            qe_ref[d, h] = (qf[h] * e1).astype(BF16)
            ket = (kf[h] * e2).T
            for cc in range(nck):
                ket_ref[d, h, cc] = ket[:, cc * chunk:(cc + 1) * chunk].astype(BF16)
                dec_ref[cc, chain] = jnp.broadcast_to(
                    e_tot[cc * chunk:cc * chunk + 1, 8 + chain:9 + chain], (1, LANES))
            rhs = jnp.concatenate([(v_scr[:, hs] * bcol).astype(BF16),
                                   (kf[h] * (bcol * e1)).astype(BF16)], axis=1)
            rhs_all.append(rhs.reshape(nck, chunk, 2 * B_HEAD_DIM))
    npow = -jnp.concatenate(a_all, axis=0)
    tinv = eye + npow
    for _ in range(n_sq - 1):
        nb = npow.astype(BF16)
        npow = lax.dot_general(nb, nb, nn_batched, preferred_element_type=F32)
        tinv = tinv + lax.dot_general(tinv.astype(BF16), npow.astype(BF16), nn_batched,
                                      preferred_element_type=F32)
    uw = lax.dot_general(tinv.astype(BF16), jnp.concatenate(rhs_all, axis=0), nn_batched,
                         preferred_element_type=F32)
    for d in range(2):
        for h in range(B_HEADS):
            blk = uw[(d * B_HEADS + h) * nck:(d * B_HEADS + h + 1) * nck]
            u_ref[d, h] = blk[:, :, :B_HEAD_DIM].reshape(n, B_HEAD_DIM).astype(BF16)
            w_ref[d, h] = blk[:, :, B_HEAD_DIM:].reshape(n, B_HEAD_DIM).astype(BF16)


def _even_prep(p, small, a_conv, b_conv, alog_pad, dtb_pad, chunk):
    bsz, t, nm = p.shape
    nhb = TM // HALO
    last_hb = t // HALO - 1
    nck = TM // chunk
    nc = t // chunk

    def per_dir():
        return (pl.BlockSpec((None, 2, B_HEADS, TM, B_HEAD_DIM), lambda b, j: (b, 0, 0, j, 0)),
                jax.ShapeDtypeStruct((bsz, 2, B_HEADS, t, B_HEAD_DIM), BF16))

    outs = [
        (pl.BlockSpec((None, TM, A_WIDTH), lambda b, j: (b, j, 0)),
         jax.ShapeDtypeStruct((bsz, t, A_WIDTH), BF16)),
        per_dir(), per_dir(), per_dir(),
        (pl.BlockSpec((None, 2, B_HEADS, nck, B_HEAD_DIM, chunk), lambda b, j: (b, 0, 0, j, 0, 0)),
         jax.ShapeDtypeStruct((bsz, 2, B_HEADS, nc, B_HEAD_DIM, chunk), BF16)),
        (pl.BlockSpec((None, 2, B_HEADS, TM, chunk), lambda b, j: (b, 0, 0, j, 0)),
         jax.ShapeDtypeStruct((bsz, 2, B_HEADS, t, chunk), BF16)),
        (pl.BlockSpec((None, nck, 2 * B_HEADS, 1, LANES), lambda b, j: (b, j, 0, 0, 0)),
         jax.ShapeDtypeStruct((bsz, nc, 2 * B_HEADS, 1, LANES), F32)),
    ]
    return pl.pallas_call(
        functools.partial(_even_prep_kernel, chunk=chunk),
        grid=(bsz, t // TM),
        in_specs=[
            pl.BlockSpec((None, TM, nm), lambda b, j: (b, j, 0)),
            pl.BlockSpec((None, HALO, nm), lambda b, j: (b, jnp.maximum(j * nhb - 1, 0), 0)),
            pl.BlockSpec((None, HALO, nm), lambda b, j: (b, jnp.minimum((j + 1) * nhb, last_hb), 0)),
            pl.BlockSpec((None, TM, LANES), lambda b, j: (b, j, 0)),
            pl.BlockSpec((3, A_WIDTH), lambda b, j: (0, 0)),
            pl.BlockSpec((3, 3 * B_WIDTH), lambda b, j: (0, 0)),
            pl.BlockSpec((1, LANES), lambda b, j: (0, 0)),
            pl.BlockSpec((1, LANES), lambda b, j: (0, 0)),
        ],
        out_specs=[o[0] for o in outs],
        out_shape=[o[1] for o in outs],
        scratch_shapes=[pltpu.VMEM((TM, B_WIDTH), BF16), pltpu.VMEM((TM, B_WIDTH), BF16),
                        pltpu.VMEM((TM, B_WIDTH), F32)],
        compiler_params=_cparams(2),
        name="even_prep",
    )(p, p, p, small, a_conv, b_conv, alog_pad, dtb_pad)


def _gdn_rec_kernel(uf, wf, qf, kf, af, df, ur, wr, qr, kr, ar, dr, of_ref, or_ref, s_scr, *, chunk):
    @pl.when(pl.program_id(1) == 0)
    def _():
        s_scr[...] = jnp.zeros_like(s_scr)

    nck = uf.shape[1] // chunk
    nn_batched = (((2,), (1,)), ((0,), (0,)))

    def both(fwd, rev):
        return jnp.concatenate([fwd, rev], axis=0)

    for cc in range(nck):
        cr = nck - 1 - cc
        rf = slice(cc * chunk, (cc + 1) * chunk)
        rr = slice(cr * chunk, (cr + 1) * chunk)
        s = s_scr[...]
        sb = s.astype(BF16)
        u = both(uf[:, rf, :], ur[:, rr, :]).astype(F32)
        v_new = u - lax.dot_general(both(wf[:, rf, :], wr[:, rr, :]), sb, nn_batched,
                                    preferred_element_type=F32)
        vb = v_new.astype(BF16)
        o = (lax.dot_general(both(qf[:, rf, :], qr[:, rr, :]), sb, nn_batched,
                             preferred_element_type=F32)
             + lax.dot_general(both(af[:, rf, :], ar[:, rr, :]), vb, nn_batched,
                               preferred_element_type=F32))
        dec = both(df[cc, 0:B_HEADS], dr[cr, B_HEADS:2 * B_HEADS])
        s_scr[...] = s * dec + lax.dot_general(both(kf[:, cc], kr[:, cr]), vb, nn_batched,
                                               preferred_element_type=F32)
        for h in range(B_HEADS):
            hs = slice(h * B_HEAD_DIM, (h + 1) * B_HEAD_DIM)
            of_ref[rf, hs] = o[h].astype(BF16)
            or_ref[rr, hs] = o[B_HEADS + h].astype(BF16)


def _gdn_rec(u, w, qe, ket, att, dec, chunk):
    bsz, _, _, t, _ = u.shape
    nt = t // TM
    nck = TM // chunk

    def tile(d, s):
        return s if d == 0 else jnp.where(s == 0, 0, nt - s)

    in_specs = []
    for d in range(2):
        for _ in range(3):
            in_specs.append(pl.BlockSpec((None, None, B_HEADS, TM, B_HEAD_DIM),
                                         lambda b, s, d=d: (b, d, 0, tile(d, s), 0)))
        in_specs.append(pl.BlockSpec((None, None, B_HEADS, nck, B_HEAD_DIM, chunk),
                                     lambda b, s, d=d: (b, d, 0, tile(d, s), 0, 0)))
        in_specs.append(pl.BlockSpec((None, None, B_HEADS, TM, chunk),
                                     lambda b, s, d=d: (b, d, 0, tile(d, s), 0)))
        in_specs.append(pl.BlockSpec((None, nck, 2 * B_HEADS, 1, LANES),
                                     lambda b, s, d=d: (b, tile(d, s), 0, 0, 0)))
    return pl.pallas_call(
        functools.partial(_gdn_rec_kernel, chunk=chunk),
        grid=(bsz, nt),
        in_specs=in_specs,
        out_specs=[pl.BlockSpec((None, TM, B_WIDTH), lambda b, s, d=d: (b, tile(d, s), 0))
                   for d in range(2)],
        out_shape=[jax.ShapeDtypeStruct((bsz, t, B_WIDTH), BF16)] * 2,
        scratch_shapes=[pltpu.VMEM((2 * B_HEADS, B_HEAD_DIM, B_HEAD_DIM), F32)],
        compiler_params=_cparams(2),
        name="gdn_rec",
    )(u, w, qe, ket, att, dec, u, w, qe, ket, att, dec)


def _route(logits_t, bias):
    aff = _sigmoid(logits_t)
    sel = aff + bias
    e_idx = lax.broadcasted_iota(jnp.int32, sel.shape, 0)
    pos = e_idx % PER_GROUP
    grp = e_idx // PER_GROUP

    def group_rot(v, k):
        return jnp.where(pos + k < PER_GROUP, pltpu.roll(v, N_EXPERTS - k, 0),
                         pltpu.roll(v, PER_GROUP - k, 0))

    rank = jnp.zeros(sel.shape, F32)
    for k in range(1, PER_GROUP):
        other = group_rot(sel, k)
        rank = rank + jnp.where(pos + k >= PER_GROUP, jnp.where(other >= sel, 1.0, 0.0),
                                jnp.where(other > sel, 1.0, 0.0))
    top = rank < 2.0
    gsum = jnp.where(top, sel, 0.0)
    gs = gsum
    for k in range(1, PER_GROUP):
        gs = gs + group_rot(gsum, k)
    beaten = jnp.zeros(sel.shape, F32)
    for m in range(1, N_GROUPS):
        other = pltpu.roll(gs, PER_GROUP * m, 0)
        beaten = beaten + jnp.where(grp >= m, jnp.where(gs > other, 0.0, 1.0),
                                    jnp.where(gs >= other, 0.0, 1.0))
    chosen = jnp.where(top, beaten, 1.0) < 0.5
    denom = jnp.sum(jnp.where(chosen, aff, 0.0), axis=0, keepdims=True)
    gate = jnp.where(chosen, aff / denom, 0.0)
    posf = pos.astype(F32)
    lo = jnp.min(jnp.where(chosen, posf, float(PER_GROUP)), axis=0, keepdims=True)
    hi = jnp.max(jnp.where(chosen, posf, -1.0), axis=0, keepdims=True)
    w_lo = jnp.sum(jnp.where(posf == lo, gate, 0.0), axis=0, keepdims=True)
    w_hi = jnp.sum(jnp.where(posf == hi, gate, 0.0), axis=0, keepdims=True)
    group = jnp.max(jnp.where(chosen, grp.astype(F32), 0.0), axis=0, keepdims=True)
    pair = lo * (2 * PER_GROUP - 1 - lo) * 0.5 + hi - lo - 1.0
    return group * float(N_PAIRS) + pair, w_lo, w_hi


def _outproj_even_kernel(ya_ref, of_ref, or_ref, gate_ref, bn_ref, c_ref, x_ref, *rest):
    xres = jnp.where(pl.program_id(1) == 0, c_ref[...], x_ref[...])
    parts = []
    for h in range(B_HEADS):
        hs = slice(h * B_HEAD_DIM, (h + 1) * B_HEAD_DIM)
        o = of_ref[:, hs].astype(F32) + or_ref[:, hs].astype(F32)
        y = o * lax.rsqrt(jnp.mean(o * o, axis=-1, keepdims=True) + RMS_EPS) * bn_ref[...]
        parts.append((y * _silu(gate_ref[:, hs].astype(F32))).astype(BF16))
    _outproj_body(ya_ref[...], jnp.concatenate(parts, axis=1), xres, *rest)


def _outproj_kernel(ya_ref, yb_ref, x_ref, *rest):
    _outproj_body(ya_ref[...], yb_ref[...], x_ref[...], *rest)


def _outproj_body(ya, yb, xres, mod_ref, w_ref, lng_ref, lnb_ref, rwt_ref, rb_ref,
                  xo_ref, h_ref, info_ref, cnt_ref, run_scr, wt_scr):
    mod = mod_ref[...]
    wa = w_ref[0:ya.shape[1], :]
    wb = w_ref[ya.shape[1]:, :]
    y = (jnp.dot(ya, wa, preferred_element_type=F32)
         + jnp.dot(yb, wb, preferred_element_type=F32))
    xn = _layer_norm(DEEPNORM_ALPHA * xres + mod[2:3] * y, lng_ref[...], lnb_ref[...])
    xo_ref[...] = xn
    h = xn * (1.0 + mod[4:5]) + mod[3:4]
    d = h.shape[1]
    h_ref[:, 0:d] = h
    rw = rwt_ref[...]
    rw_hi = rw.astype(BF16)
    rw_lo = (rw - rw_hi.astype(F32)).astype(BF16)
    h_hi = h.astype(BF16)
    h_lo = (h - h_hi.astype(F32)).astype(BF16)
    nt = (((1,), (1,)), ((), ()))
    logits_t = (lax.dot_general(rw_hi, h_hi, nt, preferred_element_type=F32)
                + lax.dot_general(rw_hi, h_lo, nt, preferred_element_type=F32)
                + lax.dot_general(rw_lo, h_hi, nt, preferred_element_type=F32))
    cls, w_lo, w_hi = _route(logits_t, rb_ref[...])

    @pl.when(jnp.logical_and(pl.program_id(0) == 0, pl.program_id(1) == 0))
    def _():
        run_scr[...] = jnp.zeros_like(run_scr)

    n = cls.shape[1]
    crow = lax.broadcasted_iota(jnp.int32, (CLS_PAD, n), 0).astype(F32)
    onehot = jnp.where(crow == cls, 1.0, 0.0)
    si = lax.broadcasted_iota(jnp.int32, (n, n), 0)
    ti = lax.broadcasted_iota(jnp.int32, (n, n), 1)
    before = jnp.where(si < ti, 1.0, 0.0).astype(BF16)
    cum = jnp.dot(onehot.astype(BF16), before, preferred_element_type=F32)
    run = run_scr[...]
    rank = jnp.sum(onehot * (cum + run[:, 0:1]), axis=0, keepdims=True)
    run = run + jnp.sum(onehot, axis=1, keepdims=True)
    run_scr[...] = run
    cnt_ref[...] = run
    info_ref[...] = jnp.zeros_like(info_ref)
    info_ref[0:1, :] = cls
    info_ref[1:2, :] = rank
    wt_scr[...] = jnp.zeros_like(wt_scr)
    wt_scr[0:1, :] = w_lo
    wt_scr[1:2, :] = w_hi
    h_ref[:, d:] = wt_scr[...].T


def _outproj(ya, yb, xres, modarr, w_out, ln_g, ln_b, rwt, rbias, row_blk0, gdn=None):
    bsz, n, wa = ya.shape

    def tok(width, col_blk=0):
        return pl.BlockSpec((None, TM, width), lambda b, j: (b, j, col_blk))

    if gdn is None:
        d = xres.shape[2]
        body, mix_args = _outproj_kernel, (ya, yb, xres)
        mix_specs = [tok(wa), tok(yb.shape[2]),
                     pl.BlockSpec((None, TM, d), lambda b, j: (b, j + row_blk0, 0))]
    else:
        o_fwd, o_rev, p, b_norm = gdn
        ctx, x = xres
        d = x.shape[2]
        gate_blk = (3 * A_WIDTH + 3 * B_WIDTH) // B_WIDTH
        body, mix_args = _outproj_even_kernel, (ya, o_fwd, o_rev, p, b_norm, ctx, x)
        mix_specs = [tok(wa), tok(B_WIDTH), tok(B_WIDTH), tok(B_WIDTH, gate_blk),
                     pl.BlockSpec((1, B_HEAD_DIM), lambda b, j: (0, 0)),
                     pl.BlockSpec((None, TM, d), lambda b, j: (b, 0, 0)),
                     pl.BlockSpec((None, TM, d), lambda b, j: (b, jnp.maximum(j - 1, 0), 0))]
    return pl.pallas_call(
        body,
        grid=(bsz, n // TM),
        in_specs=mix_specs + [
            pl.BlockSpec((None, None, 6, d), lambda b, j: (b, jnp.minimum(j + row_blk0, 1), 0, 0)),
            pl.BlockSpec(w_out.shape, lambda b, j: (0, 0)),
            pl.BlockSpec((1, d), lambda b, j: (0, 0)),
            pl.BlockSpec((1, d), lambda b, j: (0, 0)),
            pl.BlockSpec((N_EXPERTS, d), lambda b, j: (0, 0)),
            pl.BlockSpec((N_EXPERTS, 1), lambda b, j: (0, 0)),
        ],
        out_specs=[
            pl.BlockSpec((None, TM, d), lambda b, j: (b, j, 0)),
            pl.BlockSpec((None, TM, d + LANES), lambda b, j: (b, j, 0)),
            pl.BlockSpec((None, None, 8, TM), lambda b, j: (b, j, 0, 0)),
            pl.BlockSpec((CLS_PAD, LANES), lambda b, j: (0, 0)),
        ],
        out_shape=[
            jax.ShapeDtypeStruct((bsz, n, d), F32),
            jax.ShapeDtypeStruct((bsz, n, d + LANES), F32),
            jax.ShapeDtypeStruct((bsz, n // TM, 8, TM), F32),
            jax.ShapeDtypeStruct((CLS_PAD, LANES), F32),
        ],
        scratch_shapes=[pltpu.VMEM((CLS_PAD, LANES), F32),
                        pltpu.VMEM((LANES, TM), F32)],
        compiler_params=_cparams(2),
        name="outproj",
    )(*mix_args, modarr, w_out, ln_g, ln_b, rwt, rbias)


SCATTER_ROWS = 2048


def _row_scatter_kernel(dst_ref, src_ref, init_hbm, out_hbm, sem):
    del init_hbm
    rows = src_ref.shape[0]

    def body(g, carry):
        r0 = pl.multiple_of(g * 8, 8)
        for u in range(8):
            pltpu.make_async_copy(src_ref.at[pl.ds(r0 + u, 1)],
                                  out_hbm.at[pl.ds(dst_ref[0, r0 + u], 1)], sem).start(priority=u % 2)
        return carry

    lax.fori_loop(0, rows // 8, body, 0)
    pltpu.make_async_copy(src_ref, out_hbm.at[pl.ds(0, rows)], sem).wait()


def _row_scatter(src, dest, n_out):
    n, width = src.shape
    rows = math.gcd(n, SCATTER_ROWS)
    return pl.pallas_call(
        _row_scatter_kernel,
        grid=(n // rows,),
        in_specs=[
            pl.BlockSpec((None, 1, rows), lambda j: (j, 0, 0), memory_space=pltpu.SMEM),
            pl.BlockSpec((rows, width), lambda j: (j, 0)),
            pl.BlockSpec(memory_space=pl.ANY),
        ],
        out_specs=pl.BlockSpec(memory_space=pl.ANY),
        out_shape=jax.ShapeDtypeStruct((n_out, width), src.dtype),
        scratch_shapes=[pltpu.SemaphoreType.DMA(())],
        input_output_aliases={2: 0},
        compiler_params=_cparams(1),
        name="row_scatter",
    )(dest.reshape(n // rows, 1, rows), src, jnp.zeros((n_out, width), src.dtype))


def _moe_kernel(elo_ref, ehi_ref, nused_ref, x_ref, wg_lo, wu_lo, wd_lo, wg_hi, wu_hi, wd_hi, o_ref):
    used = pl.program_id(0) < nused_ref[0]
    d = o_ref.shape[1]

    @pl.when(used)
    def _():
        x = x_ref[:, 0:d].astype(BF16)
        acc = None
        for col, (wg, wu, wd) in enumerate(((wg_lo, wu_lo, wd_lo), (wg_hi, wu_hi, wd_hi))):
            gate = jnp.dot(x, wg[...], preferred_element_type=F32)
            up = jnp.dot(x, wu[...], preferred_element_type=F32)
            act = (_silu(gate) * up).astype(BF16)
            y = x_ref[:, d + col:d + col + 1] * jnp.dot(act, wd[...], preferred_element_type=F32)
            acc = y if acc is None else acc + y
        o_ref[...] = acc

    @pl.when(jnp.logical_not(used))
    def _():
        o_ref[...] = jnp.zeros_like(o_ref)


def _moe_plan(info, counts, n_tok):
    mt = MOE_TM
    n_tiles = n_tok // mt + N_CLASSES
    cls = info[:, :, 0, :].reshape(-1).astype(jnp.int32)
    rank = info[:, :, 1, :].reshape(-1).astype(jnp.int32)
    cnt = counts[:N_CLASSES, 0].astype(jnp.int32)
    padded = ((cnt + mt - 1) // mt) * mt
    ends = jnp.cumsum(padded)
    starts = ends - padded
    classes = jnp.arange(N_CLASSES, dtype=jnp.int32)
    dest = jnp.sum(jnp.where(cls[:, None] == classes[None, :], starts[None, :], 0), axis=1) + rank
    n_used = ends[-1] // mt
    tidx = jnp.arange(n_tiles, dtype=jnp.int32)
    tidx = jnp.minimum(tidx, n_used - 1)
    tcls = jnp.sum((ends[None, :] <= (tidx * mt)[:, None]).astype(jnp.int32), axis=1)
    tcls = jnp.minimum(tcls, N_CLASSES - 1)
    pairs = [(a, b) for a in range(PER_GROUP) for b in range(a + 1, PER_GROUP)]
    pair = tcls % N_PAIRS
    lo = sum(jnp.where(pair == k, a, 0) for k, (a, _) in enumerate(pairs))
    hi = sum(jnp.where(pair == k, b, 0) for k, (_, b) in enumerate(pairs))
    group = tcls // N_PAIRS
    return (group * PER_GROUP + lo, group * PER_GROUP + hi, n_used.reshape(1).astype(jnp.int32),
            dest.astype(jnp.int32))


def _moe(h_ext, info, counts, wg, wu, wd):
    n, width = h_ext.shape
    d = width - LANES
    ne, _, de = wg.shape
    mt = MOE_TM
    elo, ehi, n_used, dest = _moe_plan(info, counts, n)
    n_tiles = n // mt + N_CLASSES
    h_sorted = _row_scatter(h_ext, dest, n_tiles * mt)

    def expert(which, shape):
        if which == 0:
            return pl.BlockSpec((None,) + shape, lambda i, lo, hi, nu: (lo[i], 0, 0))
        return pl.BlockSpec((None,) + shape, lambda i, lo, hi, nu: (hi[i], 0, 0))

    grid_spec = pltpu.PrefetchScalarGridSpec(
        num_scalar_prefetch=3,
        grid=(n_tiles,),
        in_specs=[
            pl.BlockSpec((mt, width), lambda i, lo, hi, nu: (jnp.minimum(i, nu[0] - 1), 0)),
            expert(0, (d, de)), expert(0, (d, de)), expert(0, (de, d)),
            expert(1, (d, de)), expert(1, (d, de)), expert(1, (de, d)),
        ],
        out_specs=pl.BlockSpec((mt, d), lambda i, *_: (i, 0)),
    )
    f_sorted = pl.pallas_call(
        _moe_kernel,
        grid_spec=grid_spec,
        out_shape=jax.ShapeDtypeStruct((n_tiles * mt, d), F32),
        compiler_params=_cparams(1),
        name="moe",
    )(elo, ehi, n_used, h_sorted, wg, wu, wd, wg, wu, wd)
    return f_sorted, dest


def _ln2_kernel(cur_ref, nxt_ref, x_ref, f_hbm, mod_ref, lng_ref, lnb_ref, o_ref, fbuf, sem):
    o_ref[...] = _gathered_ln2(cur_ref, nxt_ref, x_ref, f_hbm, mod_ref, lng_ref, lnb_ref, fbuf, sem)


def _gathered_ln2(cur_ref, nxt_ref, x_ref, f_hbm, mod_ref, lng_ref, lnb_ref, fbuf, sem):
    nj = pl.num_programs(1)
    step = pl.program_id(0) * nj + pl.program_id(1)
    n_steps = pl.num_programs(0) * nj
    slot = step % 2
    rows = fbuf.shape[1]

    def gather_start(idx_ref, s):
        def body(g, carry):
            r0 = pl.multiple_of(g * 8, 8)
            for u in range(8):
                pltpu.make_async_copy(f_hbm.at[pl.ds(idx_ref[0, r0 + u], 1)],
                                      fbuf.at[s, pl.ds(r0 + u, 1)], sem.at[s]).start(priority=u % 2)
            return carry
        lax.fori_loop(0, rows // 8, body, 0)

    @pl.when(step == 0)
    def _():
        gather_start(cur_ref, 0)

    def slot_wait(s):
        pltpu.make_async_copy(f_hbm.at[pl.ds(0, rows)], fbuf.at[s], sem.at[s]).wait()

    slot_wait(slot)
    for r in range(rows):
        pltpu.make_async_copy(f_hbm.at[pl.ds(nxt_ref[0, r], 1)], fbuf.at[1 - slot, pl.ds(r, 1)],
                              sem.at[1 - slot]).start(priority=r % 2)
    mod = mod_ref[...]
    v = DEEPNORM_ALPHA * x_ref[...] + mod[5:6] * fbuf[slot]
    out = _layer_norm(v, lng_ref[...], lnb_ref[...])

    @pl.when(step == n_steps - 1)
    def _():
        slot_wait(1 - slot)

    return out


def _ln2(x, f_sorted, dest, modarr, ln_g, ln_b, kind0):
    bsz, n, d = x.shape
    nj = n // TM
    n_steps = bsz * nj

    def idx_rows(offset):
        return pl.BlockSpec((None, 1, TM), lambda b, j: (jnp.minimum(b * nj + j + offset, n_steps - 1), 0, 0),
                            memory_space=pltpu.SMEM)

    dest3 = dest.reshape(n_steps, 1, TM)
    return pl.pallas_call(
        _ln2_kernel,
        grid=(bsz, nj),
        in_specs=[
            idx_rows(0), idx_rows(1),
            pl.BlockSpec((None, TM, d), lambda b, j: (b, j, 0)),
            pl.BlockSpec(memory_space=pl.ANY),
            pl.BlockSpec((None, None, 6, d), lambda b, j: (b, jnp.minimum(j + kind0, 1), 0, 0)),
            pl.BlockSpec((1, d), lambda b, j: (0, 0)),
            pl.BlockSpec((1, d), lambda b, j: (0, 0)),
        ],
        out_specs=pl.BlockSpec((None, TM, d), lambda b, j: (b, j, 0)),
        out_shape=jax.ShapeDtypeStruct((bsz, n, d), F32),
        scratch_shapes=[pltpu.VMEM((2, TM, d), F32), pltpu.SemaphoreType.DMA((2,))],
        compiler_params=_cparams(2),
        name="ln2",
    )(dest3, dest3, x, f_sorted, modarr, ln_g, ln_b)


def _rope(x, c, s1, s2, shift):
    w = x.shape[1]
    return x * c + pltpu.roll(x, w - shift, 1) * s1 + pltpu.roll(x, shift, 1) * s2


def _rms(x, g):
    return x * lax.rsqrt(jnp.mean(x * x, axis=-1, keepdims=True) + RMS_EPS) * g


def _ln2_inproj_odd_kernel(cur_ref, nxt_ref, x1_ref, f_hbm, mod0_ref, lng_ref, lnb_ref,
                           mod_ref, w_ref, qn_ref, kvn_ref, wuq_ref, wk_ref, we_ref, wv_ref,
                           tw_ref, tq_ref, tk_ref,
                           x2_ref, qw_ref, kw_ref, vw_ref, qm_ref, km_ref, vm_ref, fbuf, sem):
    x2 = _gathered_ln2(cur_ref, nxt_ref, x1_ref, f_hbm, mod0_ref, lng_ref, lnb_ref, fbuf, sem)
    x2_ref[...] = x2
    j = pl.program_id(1)
    is_ctx = j == 0
    mod = mod_ref[...]
    h = (x2 * (1.0 + mod[1:2]) + mod[0:1]).astype(BF16)
    p = jnp.dot(h, w_ref[...], preferred_element_type=F32)

    def tables(t_ref):
        c = jnp.where(is_ctx, 1.0, t_ref[0])
        s1 = jnp.where(is_ctx, 0.0, t_ref[1])
        s2 = jnp.where(is_ctx, 0.0, t_ref[2])
        return c, s1, s2

    cw, s1w, s2w = tables(tw_ref)
    nq = C_Q_HEADS * C_HEAD_DIM
    for r in range(nq // LANES):
        blk = _rope(p[:, r * LANES:(r + 1) * LANES], cw, s1w, s2w, C_HEAD_DIM // 2)
        qw_ref[:, r * LANES:(r + 1) * LANES] = (blk * (C_HEAD_DIM ** -0.5)).astype(BF16)
    kw_ref[...] = _rope(p[:, nq:nq + LANES], cw, s1w, s2w, C_HEAD_DIM // 2).astype(BF16)
    vw_ref[...] = p[:, nq + LANES:nq + 2 * LANES].astype(BF16)

    o = nq + 2 * LANES
    dq = _rms(p[:, o:o + D_Q_RANK], qn_ref[...]).astype(BF16)
    o += D_Q_RANK
    dkv = _rms(p[:, o:o + D_KV_RANK], kvn_ref[...]).astype(BF16)
    o += D_KV_RANK
    cq, s1q, s2q = tables(tq_ref)
    ck, s1k, s2k = tables(tk_ref)
    krope = _rope(p[:, o:o + LANES], ck, s1k, s2k, D_ROPE // 2).astype(BF16)
    scale = (D_NOPE + D_ROPE) ** -0.5
    q_all = jnp.dot(dq, wuq_ref[...], preferred_element_type=F32)
    for hh in range(D_HEADS):
        sl = slice(hh * LANES, (hh + 1) * LANES)
        qm_ref[:, sl] = (_rope(q_all[:, sl], cq, s1q, s2q, D_ROPE // 2) * scale).astype(BF16)
    km_ref[...] = (jnp.dot(dkv, wk_ref[...], preferred_element_type=F32)
                   + jnp.dot(krope, we_ref[...], preferred_element_type=F32)).astype(BF16)
    vm_ref[...] = jnp.dot(dkv, wv_ref[...], preferred_element_type=F32).astype(BF16)


def _ln2_inproj_odd(x1, f_sorted, dest, mod0, ln_g, ln_b, modarr, w1, qnorm, kvnorm, wuq, wk, we, wv,
                    tab_w, tab_q, tab_k):
    bsz, t, d = x1.shape
    hw = D_HEADS * LANES
    nj = t // TM
    n_steps = bsz * nj

    def idx_rows(offset):
        return pl.BlockSpec((None, 1, TM), lambda b, j: (jnp.minimum(b * nj + j + offset, n_steps - 1), 0, 0),
                            memory_space=pltpu.SMEM)

    dest3 = dest.reshape(n_steps, 1, TM)

    def tab_spec():
        return pl.BlockSpec((3, TM, LANES), lambda b, j: (0, jnp.maximum(j - 1, 0), 0))

    def full(a):
        return pl.BlockSpec(a.shape, lambda b, j: (0,) * a.ndim)

    def out(width):
        return (pl.BlockSpec((None, TM, width), lambda b, j: (b, j, 0)),
                jax.ShapeDtypeStruct((bsz, t, width), BF16))

    def out_latent(width):
        return (pl.BlockSpec((None, TM, width), lambda b, j: (b, jnp.maximum(j - 1, 0), 0)),
                jax.ShapeDtypeStruct((bsz, t - TM, width), BF16))

    x2_out = (pl.BlockSpec((None, TM, d), lambda b, j: (b, j, 0)), jax.ShapeDtypeStruct((bsz, t, d), F32))
    outs = (x2_out, out_latent(C_Q_HEADS * C_HEAD_DIM), out(LANES), out(LANES), out_latent(hw),
            out(hw), out(hw))

    def mod_spec():
        return pl.BlockSpec((None, None, 6, d), lambda b, j: (b, jnp.minimum(j, 1), 0, 0))

    return pl.pallas_call(
        _ln2_inproj_odd_kernel,
        grid=(bsz, nj),
        in_specs=[
            idx_rows(0), idx_rows(1),
            pl.BlockSpec((None, TM, d), lambda b, j: (b, j, 0)),
            pl.BlockSpec(memory_space=pl.ANY),
            mod_spec(), full(ln_g), full(ln_b), mod_spec(),
            full(w1), full(qnorm), full(kvnorm), full(wuq), full(wk), full(we), full(wv),
            tab_spec(), tab_spec(), tab_spec(),
        ],
        out_specs=[o[0] for o in outs],
        out_shape=[o[1] for o in outs],
        scratch_shapes=[pltpu.VMEM((2, TM, d), F32), pltpu.SemaphoreType.DMA((2,))],
        compiler_params=_cparams(2),
        name="ln2_inproj_odd",
    )(dest3, dest3, x1, f_sorted, mod0, ln_g, ln_b, modarr, w1, qnorm, kvnorm, wuq, wk, we, wv,
      tab_w, tab_q, tab_k)


WIN_TQ = 256


def _win_kernel(sink_ref, q_ref, k_ref, v_ref, o_ref, klo_scr, khi_scr, *, n_ctx):
    i = pl.program_id(1)
    wdw = C_WINDOW
    t = k_ref.shape[0]
    lane = lax.broadcasted_iota(jnp.int32, (t, LANES), 1)

    @pl.when(i == 0)
    def _():
        kk = k_ref[...]
        klo_scr[...] = jnp.where(lane < C_HEAD_DIM, kk, jnp.zeros_like(kk))
        khi_scr[...] = jnp.where(lane >= C_HEAD_DIM, kk, jnp.zeros_like(kk))

    tq = q_ref.shape[0]
    span = tq + 2 * wdw
    n_lat_blk = (t - n_ctx) // wdw
    blk0 = jnp.clip(i * (tq // wdw) - 1, 0, n_lat_blk - span // wdw)
    r0 = pl.multiple_of(n_ctx + blk0 * wdw, wdw)
    kpos = blk0 * wdw + lax.broadcasted_iota(jnp.int32, (tq, span), 1)
    qpos = i * tq + lax.broadcasted_iota(jnp.int32, (tq, span), 0)
    near = jnp.abs(kpos - qpos) <= wdw
    v_loc = v_ref[pl.ds(r0, span), :]
    v_ctx = v_ref[0:n_ctx, :]
    olane = lax.broadcasted_iota(jnp.int32, (tq, LANES), 1)
    n_rep = C_Q_HEADS // C_KV_HEADS
    for r in range(n_rep):
        q = q_ref[:, r * LANES:(r + 1) * LANES]
        outs = []
        for g, k_scr in enumerate((klo_scr, khi_scr)):
            k_loc = k_scr[pl.ds(r0, span), :]
            k_ctx = k_scr[0:n_ctx, :]
            s_loc = lax.dot_general(q, k_loc, (((1,), (1,)), ((), ())), preferred_element_type=F32)
            s_loc = jnp.where(near, s_loc, -jnp.inf)
            s_ctx = lax.dot_general(q, k_ctx, (((1,), (1,)), ((), ())), preferred_element_type=F32)
            sink = sink_ref[g * n_rep + r]
            m = jnp.maximum(jnp.maximum(jnp.max(s_loc, axis=-1, keepdims=True),
                                        jnp.max(s_ctx, axis=-1, keepdims=True)), sink)
            p_loc = jnp.exp(s_loc - m)
            p_ctx = jnp.exp(s_ctx - m)
            den = (jnp.sum(p_loc, axis=-1, keepdims=True) + jnp.sum(p_ctx, axis=-1, keepdims=True)
                   + jnp.exp(sink - m))
            pv = (jnp.dot(p_loc.astype(BF16), v_loc, preferred_element_type=F32)
                  + jnp.dot(p_ctx.astype(BF16), v_ctx, preferred_element_type=F32))
            outs.append(pv / den)
        o_ref[:, r * LANES:(r + 1) * LANES] = jnp.where(olane < C_HEAD_DIM, outs[0], outs[1]).astype(BF16)


def _win_attention(sink, qw, kw, vw, n_ctx):
    bsz, n_lat, nq = qw.shape
    t = n_ctx + n_lat
    grid_spec = pltpu.PrefetchScalarGridSpec(
        num_scalar_prefetch=1,
        grid=(bsz, n_lat // WIN_TQ),
        in_specs=[
            pl.BlockSpec((None, WIN_TQ, nq), lambda b, i, s: (b, i, 0)),
            pl.BlockSpec((None, t, LANES), lambda b, i, s: (b, 0, 0)),
            pl.BlockSpec((None, t, LANES), lambda b, i, s: (b, 0, 0)),
        ],
        out_specs=pl.BlockSpec((None, WIN_TQ, nq), lambda b, i, s: (b, i, 0)),
        scratch_shapes=[pltpu.VMEM((t, LANES), BF16), pltpu.VMEM((t, LANES), BF16)],
    )
    return pl.pallas_call(
        functools.partial(_win_kernel, n_ctx=n_ctx),
        grid_spec=grid_spec,
        out_shape=jax.ShapeDtypeStruct((bsz, n_lat, nq), BF16),
        compiler_params=_cparams(2),
        name="win_attention",
    )(sink, qw, kw, vw)


MLA_TQ = 512


def _mla_kernel(q_ref, k_ref, v_ref, o_ref):
    for hp in range(D_HEADS // 2):
        acc = None
        for hh in (2 * hp, 2 * hp + 1):
            sl = slice(hh * LANES, (hh + 1) * LANES)
            s = lax.dot_general(q_ref[:, sl], k_ref[:, sl], (((1,), (1,)), ((), ())),
                                preferred_element_type=F32)
            m = jnp.max(s, axis=-1, keepdims=True)
            p = jnp.exp(s - m)
            den = jnp.sum(p, axis=-1, keepdims=True)
            pv = jnp.dot(p.astype(BF16), v_ref[:, sl], preferred_element_type=F32) / den
            acc = pv if acc is None else acc + pv
        o_ref[:, hp * LANES:(hp + 1) * LANES] = acc.astype(BF16)


def _mla_attention(qm, km, vm, n_ctx):
    bsz, n_lat, hw = qm.shape
    t = n_ctx + n_lat
    ow = D_HEADS * D_V
    return pl.pallas_call(
        _mla_kernel,
        grid=(bsz, n_lat // MLA_TQ),
        in_specs=[
            pl.BlockSpec((None, MLA_TQ, hw), lambda b, i: (b, i, 0)),
            pl.BlockSpec((None, t, hw), lambda b, i: (b, 0, 0)),
            pl.BlockSpec((None, t, hw), lambda b, i: (b, 0, 0)),
        ],
        out_specs=pl.BlockSpec((None, MLA_TQ, ow), lambda b, i: (b, i, 0)),
        out_shape=jax.ShapeDtypeStruct((bsz, n_lat, ow), BF16),
        compiler_params=_cparams(2),
        name="mla_attention",
    )(qm, km, vm)


def _rope_tables(n_tokens, rot_dim, group, offset):
    t = jnp.arange(n_tokens)
    rows = (t // GRID_W).astype(F32)
    cols = (t % GRID_W).astype(F32)
    n_freq = rot_dim // 4
    inv_freq = ROPE_BASE ** (-jnp.arange(n_freq, dtype=F32) / n_freq)
    ang = jnp.concatenate([rows[:, None] * inv_freq, cols[:, None] * inv_freq], -1)
    cos, sin = jnp.cos(ang), jnp.sin(ang)
    half = rot_dim // 2
    c = jnp.ones((n_tokens, LANES), F32)
    s1 = jnp.zeros((n_tokens, LANES), F32)
    s2 = jnp.zeros((n_tokens, LANES), F32)
    for start in range(offset, LANES, group):
        c = c.at[:, start:start + half].set(cos).at[:, start + half:start + rot_dim].set(cos)
        s1 = s1.at[:, start:start + half].set(-sin)
        s2 = s2.at[:, start + half:start + rot_dim].set(sin)
    return jnp.stack([c, s1, s2])


def _odd_weights(w_in, wuq, wukv, w_out):
    d = w_in.shape[0]
    nq = C_Q_HEADS * C_HEAD_DIM
    nkv = C_KV_HEADS * C_HEAD_DIM
    n_rep = C_Q_HEADS // C_KV_HEADS
    order = [g * n_rep + r for r in range(n_rep) for g in range(C_KV_HEADS)]
    cq = w_in[:, :nq].reshape(d, C_Q_HEADS, C_HEAD_DIM)[:, order].reshape(d, nq)
    rest = w_in[:, nq:nq + 2 * nkv + D_Q_RANK + D_KV_RANK]
    krope = jnp.pad(w_in[:, nq + 2 * nkv + D_Q_RANK + D_KV_RANK:], ((0, 0), (0, LANES - D_ROPE)))
    w1 = jnp.concatenate([cq, rest, krope], axis=1).astype(BF16)
    qh = wuq.reshape(D_Q_RANK, D_HEADS, D_NOPE + D_ROPE)
    wuq_p = jnp.pad(qh, ((0, 0), (0, 0), (0, LANES - D_NOPE - D_ROPE))).reshape(D_Q_RANK, D_HEADS * LANES)
    kvh = wukv.reshape(D_KV_RANK, D_HEADS, D_NOPE + D_V)
    wk_p = jnp.pad(kvh[:, :, :D_NOPE], ((0, 0), (0, 0), (0, LANES - D_NOPE))).reshape(D_KV_RANK, D_HEADS * LANES)
    e_blk = jnp.zeros((LANES, LANES), F32).at[jnp.arange(D_ROPE), D_NOPE + jnp.arange(D_ROPE)].set(1.0)
    we = jnp.tile(e_blk, (1, D_HEADS))
    vh = kvh[:, :, D_NOPE:]
    even = (jnp.arange(D_HEADS) % 2 == 0)[None, :, None]
    wv_p = jnp.where(even, jnp.pad(vh, ((0, 0), (0, 0), (0, D_V))),
                     jnp.pad(vh, ((0, 0), (0, 0), (D_V, 0)))).reshape(D_KV_RANK, D_HEADS * LANES)
    wo_win = w_out[:nq].reshape(C_Q_HEADS, C_HEAD_DIM, -1)[jnp.array(order)].reshape(nq, -1)
    wo = jnp.concatenate([wo_win, w_out[nq:]], axis=0).astype(BF16)
    return w1, wuq_p.astype(BF16), wk_p.astype(BF16), we.astype(BF16), wv_p.astype(BF16), wo


def kernel(x, c, ctx, c_ctx, ada_w, ada_b, ln_g, ln_b, ev_w_in, ev_a_conv, ev_b_conv, ev_b_alog, ev_b_dtbias, ev_b_norm, ev_w_out, od_w_in, od_c_sink, od_d_qnorm, od_d_kvnorm, od_d_wuq, od_d_wukv, od_w_out, router_w, router_bias, moe_w_gate, moe_w_up, moe_w_down):
    bsz, n_lat, d = x.shape
    n_ctx = ctx.shape[1]
    assert n_ctx == TM and n_lat % TM == 0 and n_lat % GRID_W == 0
    assert ada_w.shape[0] == DEPTH and bsz + 1 <= 40
    t = n_ctx + n_lat

    cs = jnp.zeros((40, d), F32).at[:bsz].set(c).at[bsz].set(c_ctx)
    mods = _ada_mod(cs, ada_w, ada_b)

    def modarr(layer):
        m = mods[layer].reshape(40, 6, d)
        return jnp.stack([jnp.broadcast_to(m[bsz], (bsz, 6, d)), m[:bsz]], axis=1)

    rwt = router_w.T
    rbias = router_bias.reshape(N_EXPERTS, 1)

    mod0 = modarr(0)
    n_main = 3 * A_WIDTH + 4 * B_WIDTH
    w_main = ev_w_in[0][:, :n_main].astype(BF16)
    w_small = jnp.pad(ev_w_in[0][:, n_main:], ((0, 0), (0, LANES - 4 * B_HEADS))).astype(BF16)
    p, small = _inproj_even(ctx, x, mod0, w_main, w_small)
    alog_pad = jnp.zeros((1, LANES), F32).at[0, 8:16].set(ev_b_alog[0].reshape(-1))
    dtb_pad = jnp.zeros((1, LANES), F32).at[0, 8:16].set(ev_b_dtbias[0].reshape(-1))
    ya, u, w, qe, ket, att, dec = _even_prep(p, small, ev_a_conv[0], ev_b_conv[0], alog_pad, dtb_pad,
                                             GDN_CHUNK)
    o_fwd, o_rev = _gdn_rec(u, w, qe, ket, att, dec, GDN_CHUNK)
    x1, h1, info0, cnt0 = _outproj(ya, None, (ctx, x), mod0, ev_w_out[0].astype(BF16),
                             ln_g[0, 0].reshape(1, d), ln_b[0, 0].reshape(1, d), rwt, rbias, 0,
                             gdn=(o_fwd, o_rev, p, ev_b_norm[0].reshape(1, B_HEAD_DIM)))
    f, dest = _moe(h1.reshape(bsz * t, d + LANES), info0, cnt0,
                   moe_w_gate[0].astype(BF16), moe_w_up[0].astype(BF16), moe_w_down[0].astype(BF16))
    mod1 = modarr(1)
    w1, wuq_p, wk_p, we, wv_p, wo = _odd_weights(od_w_in[0], od_d_wuq[0], od_d_wukv[0], od_w_out[0])
    tab_w = _rope_tables(n_lat, C_HEAD_DIM, C_HEAD_DIM, 0)
    tab_q = _rope_tables(n_lat, D_ROPE, LANES, D_NOPE)
    tab_k = _rope_tables(n_lat, D_ROPE, LANES, 0)
    x2, qw, kw, vw, qm, km, vm = _ln2_inproj_odd(
        x1, f, dest, mod0, ln_g[0, 1].reshape(1, d), ln_b[0, 1].reshape(1, d),
        mod1, w1, od_d_qnorm[0].reshape(1, -1), od_d_kvnorm[0].reshape(1, -1),
        wuq_p, wk_p, we, wv_p, tab_w, tab_q, tab_k)
    y_win = _win_attention(od_c_sink[0], qw, kw, vw, n_ctx)
    y_mla = _mla_attention(qm, km, vm, n_ctx)
    x3, h3, info1, cnt1 = _outproj(y_win, y_mla, x2, mod1, wo, ln_g[1, 0].reshape(1, d),
                                   ln_b[1, 0].reshape(1, d), rwt, rbias, n_ctx // TM)
    f1, dest1 = _moe(h3.reshape(bsz * n_lat, d + LANES), info1, cnt1,
                     moe_w_gate[1].astype(BF16), moe_w_up[1].astype(BF16), moe_w_down[1].astype(BF16))
    return _ln2(x3, f1, dest1, mod1, ln_g[1, 1].reshape(1, d), ln_b[1, 1].reshape(1, d), 1)
```

```python
import functools
import math

import numpy as np
import jax
import jax.numpy as jnp
from jax import lax
from jax.experimental import pallas as pl
from jax.experimental.pallas import tpu as pltpu

F32 = jnp.float32
BF16 = jnp.bfloat16
HIGHEST = lax.Precision.HIGHEST

DEPTH = 2
GRID_W = 64
DEEPNORM_ALPHA = (2.0 * DEPTH) ** 0.25
LN_EPS = 1e-5
RMS_EPS = 1e-6
ROPE_BASE = 10000.0
B_HEADS = 4
B_HEAD_DIM = 128
B_WIDTH = 512
A_WIDTH = 512
C_Q_HEADS = 8
C_KV_HEADS = 2
C_HEAD_DIM = 64
C_WINDOW = 128
D_HEADS = 8
D_NOPE = 64
D_ROPE = 32
D_V = 64
D_Q_RANK = 384
D_KV_RANK = 256
N_EXPERTS = 16
N_GROUPS = 4
PER_GROUP = N_EXPERTS // N_GROUPS
D_EXPERT = 512
N_PAIRS = PER_GROUP * (PER_GROUP - 1) // 2
N_CLASSES = N_GROUPS * N_PAIRS
CLS_PAD = 32

LANES = 128
TM = 256
GDN_CHUNK = 64
MOE_TM = 256
VMEM_LIMIT = 56 * 1024 * 1024


def _cparams(n_axes, vmem=VMEM_LIMIT):
    return pltpu.CompilerParams(dimension_semantics=("arbitrary",) * n_axes, vmem_limit_bytes=vmem)


def _sigmoid(x):
    return 1.0 / (1.0 + jnp.exp(-x))


def _silu(x):
    return x * _sigmoid(x)


def _softplus(x):
    return jnp.maximum(x, 0.0) + jnp.log(1.0 + jnp.exp(-jnp.abs(x)))


def _layer_norm(v, g, b):
    mu = jnp.mean(v, axis=-1, keepdims=True)
    d = v - mu
    var = jnp.mean(d * d, axis=-1, keepdims=True)
    return d * lax.rsqrt(var + LN_EPS) * g + b


def _ada_kernel(c_ref, w_ref, b_ref, o_ref):
    s = _silu(c_ref[...])
    o_ref[...] = jnp.dot(s, w_ref[...], precision=HIGHEST, preferred_element_type=F32) + b_ref[...]


def _ada_mod(cs, ada_w, ada_b):
    depth, d, n6 = ada_w.shape
    rows = cs.shape[0]
    tn = 1536
    return pl.pallas_call(
        _ada_kernel,
        grid=(depth, n6 // tn),
        in_specs=[
            pl.BlockSpec((rows, d), lambda l, n: (0, 0)),
            pl.BlockSpec((None, d, tn), lambda l, n: (l, 0, n)),
            pl.BlockSpec((None, 1, tn), lambda l, n: (l, 0, n)),
        ],
        out_specs=pl.BlockSpec((None, rows, tn), lambda l, n: (l, 0, n)),
        out_shape=jax.ShapeDtypeStruct((depth, rows, n6), F32),
        compiler_params=_cparams(2),
        name="ada_mod",
    )(cs, ada_w, ada_b.reshape(depth, 1, n6))


def _inproj_even_kernel(c_ref, x_ref, mod_ref, wm_ref, ws_ref, p_ref, s_ref):
    mod = mod_ref[...]
    xv = jnp.where(pl.program_id(1) == 0, c_ref[...], x_ref[...])
    h = (xv * (1.0 + mod[1:2]) + mod[0:1]).astype(BF16)
    p_ref[...] = jnp.dot(h, wm_ref[...], preferred_element_type=F32).astype(BF16)
    s_ref[...] = jnp.dot(h, ws_ref[...], preferred_element_type=F32)


def _inproj_even(ctx, x, modarr, w_main, w_small):
    bsz, n_lat, d = x.shape
    t = ctx.shape[1] + n_lat
    nm = w_main.shape[1]
    return pl.pallas_call(
        _inproj_even_kernel,
        grid=(bsz, t // TM),
        in_specs=[
            pl.BlockSpec((None, TM, d), lambda b, j: (b, 0, 0)),
            pl.BlockSpec((None, TM, d), lambda b, j: (b, jnp.maximum(j - 1, 0), 0)),
            pl.BlockSpec((None, None, 6, d), lambda b, j: (b, jnp.minimum(j, 1), 0, 0)),
            pl.BlockSpec((d, nm), lambda b, j: (0, 0)),
            pl.BlockSpec((d, LANES), lambda b, j: (0, 0)),
        ],
        out_specs=[
            pl.BlockSpec((None, TM, nm), lambda b, j: (b, j, 0)),
            pl.BlockSpec((None, TM, LANES), lambda b, j: (b, j, 0)),
        ],
        out_shape=[
            jax.ShapeDtypeStruct((bsz, t, nm), BF16),
            jax.ShapeDtypeStruct((bsz, t, LANES), F32),
        ],
        compiler_params=_cparams(2),
        name="inproj_even",
    )(ctx, x, modarr, w_main, w_small)


HALO = 16


def _conv3(z, zp, zn, w):
    n = z.shape[0]
    rows = lax.broadcasted_iota(jnp.int32, z.shape, 0)
    zprev = jnp.where(rows == 0, zp, pltpu.roll(z, 1, 0))
    znext = jnp.where(rows == n - 1, zn, pltpu.roll(z, n - 1, 0))
    return w[0:1] * zprev + w[1:2] * z + w[2:3] * znext


def _even_prep_kernel(p_ref, pp_ref, pn_ref, s_ref, aw_ref, bw_ref, alog_ref, dtb_ref,
                      ya_ref, u_ref, w_ref, qe_ref, ket_ref, att_ref, dec_ref,
                      q_scr, k_scr, v_scr, *, chunk):
    j = pl.program_id(1)
    nj = pl.num_programs(1)
    prev_on = jnp.where(jnp.logical_and(j != 0, j != 1), 1.0, 0.0)
    next_on = jnp.where(jnp.logical_and(j != 0, j != nj - 1), 1.0, 0.0)
    prow = pp_ref[...].astype(F32)[HALO - 1:HALO] * prev_on
    nrow = pn_ref[...].astype(F32)[0:1] * next_on

    def seg(lo, hi):
        return p_ref[:, lo:hi].astype(F32), prow[:, lo:hi], nrow[:, lo:hi]

    a0, _, _ = seg(0, A_WIDTH)
    a1, a1p, a1n = seg(A_WIDTH, 2 * A_WIDTH)
    a2, a2p, a2n = seg(2 * A_WIDTH, 3 * A_WIDTH)
    ya_ref[...] = (a0 * _conv3(a1 * a2, a1p * a2p, a1n * a2n, aw_ref[...])).astype(BF16)

    base = 3 * A_WIDTH
    for which in range(3):
        lo = base + which * B_WIDTH
        z, zp, zn = seg(lo, lo + B_WIDTH)
        c = _silu(_conv3(z, zp, zn, bw_ref[:, which * B_WIDTH:(which + 1) * B_WIDTH]))
        for h in range(B_HEADS):
            ch = c[:, h * B_HEAD_DIM:(h + 1) * B_HEAD_DIM]
            if which < 2:
                ss = jnp.sum(ch * ch, axis=-1, keepdims=True)
                ch = ch * lax.rsqrt(ss + 1e-6)
                if which == 0:
                    ch = ch * (B_HEAD_DIM ** -0.5)
            sl = slice(h * B_HEAD_DIM, (h + 1) * B_HEAD_DIM)
            if which == 0:
                q_scr[:, sl] = ch.astype(BF16)
            elif which == 1:
                k_scr[:, sl] = ch.astype(BF16)
            else:
                v_scr[:, sl] = ch

    s = s_ref[...]
    beta = _sigmoid(s)
    g = -jnp.exp(alog_ref[...]) * _softplus(s + dtb_ref[...])
    n = s.shape[0]
    nck = n // chunk
    ri = lax.broadcasted_iota(jnp.int32, (n, n), 0)
    ci = lax.broadcasted_iota(jnp.int32, (n, n), 1)
    same = (ri // chunk) == (ci // chunk)
    m_fwd = jnp.where(jnp.logical_and(same, ci <= ri), 1.0, 0.0).astype(BF16)
    m_rev = jnp.where(jnp.logical_and(same, ci >= ri), 1.0, 0.0).astype(BF16)
    g1 = g.astype(BF16)
    r1 = g - g1.astype(F32)
    g2 = r1.astype(BF16)
    g3 = (r1 - g2.astype(F32)).astype(BF16)
    g_split = jnp.concatenate([g1, g2, g3], axis=1)

    def cumulate(m):
        parts = jnp.dot(m, g_split, preferred_element_type=F32)
        return parts[:, 0:LANES] + parts[:, LANES:2 * LANES] + parts[:, 2 * LANES:]

    gc_f = cumulate(m_fwd)
    gc_r = cumulate(m_rev)
    tot = gc_f + gc_r - g
    lane = lax.broadcasted_iota(jnp.int32, s.shape, 1)
    gc = jnp.where(lane >= 8 + B_HEADS, gc_r, gc_f)
    e_gc = jnp.exp(gc)
    e_rest = jnp.exp(tot - gc)
    gct = gc.T
    e_tot = jnp.exp(tot)

    ri = lax.broadcasted_iota(jnp.int32, (1, chunk, chunk), 1)
    ci = lax.broadcasted_iota(jnp.int32, (1, chunk, chunk), 2)
    eye = jnp.where(ri == ci, 1.0, 0.0)
    n_sq = int(np.log2(chunk))
    nt_batched = (((2,), (2,)), ((0,), (0,)))
    nn_batched = (((2,), (1,)), ((0,), (0,)))
    kk, qk, kf, qf = [], [], [], []
    for h in range(B_HEADS):
        hs = slice(h * B_HEAD_DIM, (h + 1) * B_HEAD_DIM)
        k3 = k_scr[:, hs].reshape(nck, chunk, B_HEAD_DIM)
        q3 = q_scr[:, hs].reshape(nck, chunk, B_HEAD_DIM)
        kk.append(lax.dot_general(k3, k3, nt_batched, preferred_element_type=F32))
        qk.append(lax.dot_general(q3, k3, nt_batched, preferred_element_type=F32))
        kf.append(k_scr[:, hs].astype(F32))
        qf.append(q_scr[:, hs].astype(F32))
    a_all, rhs_all = [], []
    for d in range(2):
        incl = (ci <= ri) if d == 0 else (ci >= ri)
        strict = (ci < ri) if d == 0 else (ci > ri)
        for h in range(B_HEADS):
            chain = d * B_HEADS + h
            hs = slice(h * B_HEAD_DIM, (h + 1) * B_HEAD_DIM)
            bcol = beta[:, chain:chain + 1]
            e1 = e_gc[:, 8 + chain:9 + chain]
            e2 = e_rest[:, 8 + chain:9 + chain]
            gcol = gc[:, 8 + chain:9 + chain].reshape(nck, chunk, 1)
            grow = jnp.stack([gct[8 + chain:9 + chain, cc * chunk:(cc + 1) * chunk]
                              for cc in range(nck)], axis=0)
            decay = jnp.exp(jnp.where(incl, gcol - grow, -jnp.inf))
            a_all.append(jnp.where(strict, bcol.reshape(nck, chunk, 1) * kk[h] * decay, 0.0))
            att_ref[d, h] = (qk[h] * decay).reshape(n, chunk).astype(BF16)
            qe_ref[d, h] = (qf[h] * e1).astype(BF16)
            ket = (kf[h] * e2).T
            for cc in range(nck):
                ket_ref[d, h, cc] = ket[:, cc * chunk:(cc + 1) * chunk].astype(BF16)
                dec_ref[cc, chain] = jnp.broadcast_to(
                    e_tot[cc * chunk:cc * chunk + 1, 8 + chain:9 + chain], (1, LANES))
            rhs = jnp.concatenate([(v_scr[:, hs] * bcol).astype(BF16),
                                   (kf[h] * (bcol * e1)).astype(BF16)], axis=1)
            rhs_all.append(rhs.reshape(nck, chunk, 2 * B_HEAD_DIM))
    npow = -jnp.concatenate(a_all, axis=0)
    tinv = eye + npow
    for _ in range(n_sq - 1):
        nb = npow.astype(BF16)
        npow = lax.dot_general(nb, nb, nn_batched, preferred_element_type=F32)
        tinv = tinv + lax.dot_general(tinv.astype(BF16), npow.astype(BF16), nn_batched,
                                      preferred_element_type=F32)
    uw = lax.dot_general(tinv.astype(BF16), jnp.concatenate(rhs_all, axis=0), nn_batched,
                         preferred_element_type=F32)
    for d in range(2):
        for h in range(B_HEADS):
            blk = uw[(d * B_HEADS + h) * nck:(d * B_HEADS + h + 1) * nck]
            u_ref[d, h] = blk[:, :, :B_HEAD_DIM].reshape(n, B_HEAD_DIM).astype(BF16)
            w_ref[d, h] = blk[:, :, B_HEAD_DIM:].reshape(n, B_HEAD_DIM).astype(BF16)


def _even_prep(p, small, a_conv, b_conv, alog_pad, dtb_pad, chunk):
    bsz, t, nm = p.shape
    nhb = TM // HALO
    last_hb = t // HALO - 1
    nck = TM // chunk
    nc = t // chunk

    def per_dir():
        return (pl.BlockSpec((None, 2, B_HEADS, TM, B_HEAD_DIM), lambda b, j: (b, 0, 0, j, 0)),
                jax.ShapeDtypeStruct((bsz, 2, B_HEADS, t, B_HEAD_DIM), BF16))

    outs = [
        (pl.BlockSpec((None, TM, A_WIDTH), lambda b, j: (b, j, 0)),
         jax.ShapeDtypeStruct((bsz, t, A_WIDTH), BF16)),
        per_dir(), per_dir(), per_dir(),
        (pl.BlockSpec((None, 2, B_HEADS, nck, B_HEAD_DIM, chunk), lambda b, j: (b, 0, 0, j, 0, 0)),
         jax.ShapeDtypeStruct((bsz, 2, B_HEADS, nc, B_HEAD_DIM, chunk), BF16)),
        (pl.BlockSpec((None, 2, B_HEADS, TM, chunk), lambda b, j: (b, 0, 0, j, 0)),
         jax.ShapeDtypeStruct((bsz, 2, B_HEADS, t, chunk), BF16)),
        (pl.BlockSpec((None, nck, 2 * B_HEADS, 1, LANES), lambda b, j: (b, j, 0, 0, 0)),
         jax.ShapeDtypeStruct((bsz, nc, 2 * B_HEADS, 1, LANES), F32)),
    ]
    return pl.pallas_call(
        functools.partial(_even_prep_kernel, chunk=chunk),
        grid=(bsz, t // TM),
        in_specs=[
            pl.BlockSpec((None, TM, nm), lambda b, j: (b, j, 0)),
            pl.BlockSpec((None, HALO, nm), lambda b, j: (b, jnp.maximum(j * nhb - 1, 0), 0)),
            pl.BlockSpec((None, HALO, nm), lambda b, j: (b, jnp.minimum((j + 1) * nhb, last_hb), 0)),
            pl.BlockSpec((None, TM, LANES), lambda b, j: (b, j, 0)),
            pl.BlockSpec((3, A_WIDTH), lambda b, j: (0, 0)),
            pl.BlockSpec((3, 3 * B_WIDTH), lambda b, j: (0, 0)),
            pl.BlockSpec((1, LANES), lambda b, j: (0, 0)),
            pl.BlockSpec((1, LANES), lambda b, j: (0, 0)),
        ],
        out_specs=[o[0] for o in outs],
        out_shape=[o[1] for o in outs],
        scratch_shapes=[pltpu.VMEM((TM, B_WIDTH), BF16), pltpu.VMEM((TM, B_WIDTH), BF16),
                        pltpu.VMEM((TM, B_WIDTH), F32)],
        compiler_params=_cparams(2),
        name="even_prep",
    )(p, p, p, small, a_conv, b_conv, alog_pad, dtb_pad)


def _gdn_rec_kernel(uf, wf, qf, kf, af, df, ur, wr, qr, kr, ar, dr, of_ref, or_ref, s_scr, *, chunk):
    @pl.when(pl.program_id(1) == 0)
    def _():
        s_scr[...] = jnp.zeros_like(s_scr)

    nck = uf.shape[1] // chunk
    nn_batched = (((2,), (1,)), ((0,), (0,)))

    def both(fwd, rev):
        return jnp.concatenate([fwd, rev], axis=0)

    for cc in range(nck):
        cr = nck - 1 - cc
        rf = slice(cc * chunk, (cc + 1) * chunk)
        rr = slice(cr * chunk, (cr + 1) * chunk)
        s = s_scr[...]
        sb = s.astype(BF16)
        u = both(uf[:, rf, :], ur[:, rr, :]).astype(F32)
        v_new = u - lax.dot_general(both(wf[:, rf, :], wr[:, rr, :]), sb, nn_batched,
                                    preferred_element_type=F32)
        vb = v_new.astype(BF16)
        o = (lax.dot_general(both(qf[:, rf, :], qr[:, rr, :]), sb, nn_batched,
                             preferred_element_type=F32)
             + lax.dot_general(both(af[:, rf, :], ar[:, rr, :]), vb, nn_batched,
                               preferred_element_type=F32))
        dec = both(df[cc, 0:B_HEADS], dr[cr, B_HEADS:2 * B_HEADS])
        s_scr[...] = s * dec + lax.dot_general(both(kf[:, cc], kr[:, cr]), vb, nn_batched,
                                               preferred_element_type=F32)
        for h in range(B_HEADS):
            hs = slice(h * B_HEAD_DIM, (h + 1) * B_HEAD_DIM)
            of_ref[rf, hs] = o[h].astype(BF16)
            or_ref[rr, hs] = o[B_HEADS + h].astype(BF16)


def _gdn_rec(u, w, qe, ket, att, dec, chunk):
    bsz, _, _, t, _ = u.shape
    nt = t // TM
    nck = TM // chunk

    def tile(d, s):
        return s if d == 0 else jnp.where(s == 0, 0, nt - s)

    in_specs = []
    for d in range(2):
        for _ in range(3):
            in_specs.append(pl.BlockSpec((None, None, B_HEADS, TM, B_HEAD_DIM),
                                         lambda b, s, d=d: (b, d, 0, tile(d, s), 0)))
        in_specs.append(pl.BlockSpec((None, None, B_HEADS, nck, B_HEAD_DIM, chunk),
                                     lambda b, s, d=d: (b, d, 0, tile(d, s), 0, 0)))
        in_specs.append(pl.BlockSpec((None, None, B_HEADS, TM, chunk),
                                     lambda b, s, d=d: (b, d, 0, tile(d, s), 0)))
        in_specs.append(pl.BlockSpec((None, nck, 2 * B_HEADS, 1, LANES),
                                     lambda b, s, d=d: (b, tile(d, s), 0, 0, 0)))
    return pl.pallas_call(
        functools.partial(_gdn_rec_kernel, chunk=chunk),
        grid=(bsz, nt),
        in_specs=in_specs,
        out_specs=[pl.BlockSpec((None, TM, B_WIDTH), lambda b, s, d=d: (b, tile(d, s), 0))
                   for d in range(2)],
        out_shape=[jax.ShapeDtypeStruct((bsz, t, B_WIDTH), BF16)] * 2,
        scratch_shapes=[pltpu.VMEM((2 * B_HEADS, B_HEAD_DIM, B_HEAD_DIM), F32)],
        compiler_params=_cparams(2),
        name="gdn_rec",
    )(u, w, qe, ket, att, dec, u, w, qe, ket, att, dec)


def _route(logits_t, bias):
    aff = _sigmoid(logits_t)
    sel = aff + bias
    e_idx = lax.broadcasted_iota(jnp.int32, sel.shape, 0)
    pos = e_idx % PER_GROUP
    grp = e_idx // PER_GROUP

    def group_rot(v, k):
        return jnp.where(pos + k < PER_GROUP, pltpu.roll(v, N_EXPERTS - k, 0),
                         pltpu.roll(v, PER_GROUP - k, 0))

    rank = jnp.zeros(sel.shape, F32)
    for k in range(1, PER_GROUP):
        other = group_rot(sel, k)
        rank = rank + jnp.where(pos + k >= PER_GROUP, jnp.where(other >= sel, 1.0, 0.0),
                                jnp.where(other > sel, 1.0, 0.0))
    top = rank < 2.0
    gsum = jnp.where(top, sel, 0.0)
    gs = gsum
    for k in range(1, PER_GROUP):
        gs = gs + group_rot(gsum, k)
    beaten = jnp.zeros(sel.shape, F32)
    for m in range(1, N_GROUPS):
        other = pltpu.roll(gs, PER_GROUP * m, 0)
        beaten = beaten + jnp.where(grp >= m, jnp.where(gs > other, 0.0, 1.0),
                                    jnp.where(gs >= other, 0.0, 1.0))
    chosen = jnp.where(top, beaten, 1.0) < 0.5
    denom = jnp.sum(jnp.where(chosen, aff, 0.0), axis=0, keepdims=True)
    gate = jnp.where(chosen, aff / denom, 0.0)
    posf = pos.astype(F32)
    lo = jnp.min(jnp.where(chosen, posf, float(PER_GROUP)), axis=0, keepdims=True)
    hi = jnp.max(jnp.where(chosen, posf, -1.0), axis=0, keepdims=True)
    w_lo = jnp.sum(jnp.where(posf == lo, gate, 0.0), axis=0, keepdims=True)
    w_hi = jnp.sum(jnp.where(posf == hi, gate, 0.0), axis=0, keepdims=True)
    group = jnp.max(jnp.where(chosen, grp.astype(F32), 0.0), axis=0, keepdims=True)
    pair = lo * (2 * PER_GROUP - 1 - lo) * 0.5 + hi - lo - 1.0
    return group * float(N_PAIRS) + pair, w_lo, w_hi


def _outproj_even_kernel(ya_ref, of_ref, or_ref, gate_ref, bn_ref, c_ref, x_ref, *rest):
    xres = jnp.where(pl.program_id(1) == 0, c_ref[...], x_ref[...])
    parts = []
    for h in range(B_HEADS):
        hs = slice(h * B_HEAD_DIM, (h + 1) * B_HEAD_DIM)
        o = of_ref[:, hs].astype(F32) + or_ref[:, hs].astype(F32)
        y = o * lax.rsqrt(jnp.mean(o * o, axis=-1, keepdims=True) + RMS_EPS) * bn_ref[...]
        parts.append((y * _silu(gate_ref[:, hs].astype(F32))).astype(BF16))
    _outproj_body(ya_ref[...], jnp.concatenate(parts, axis=1), xres, *rest)


def _outproj_kernel(ya_ref, yb_ref, x_ref, *rest):
    _outproj_body(ya_ref[...], yb_ref[...], x_ref[...], *rest)


def _outproj_body(ya, yb, xres, mod_ref, w_ref, lng_ref, lnb_ref, rwt_ref, rb_ref,
                  xo_ref, h_ref, info_ref, cnt_ref, run_scr, wt_scr):
    mod = mod_ref[...]
    wa = w_ref[0:ya.shape[1], :]
    wb = w_ref[ya.shape[1]:, :]
    y = (jnp.dot(ya, wa, preferred_element_type=F32)
         + jnp.dot(yb, wb, preferred_element_type=F32))
    xn = _layer_norm(DEEPNORM_ALPHA * xres + mod[2:3] * y, lng_ref[...], lnb_ref[...])
    xo_ref[...] = xn
    h = xn * (1.0 + mod[4:5]) + mod[3:4]
    d = h.shape[1]
    h_ref[:, 0:d] = h
    rw = rwt_ref[...]
    rw_hi = rw.astype(BF16)
    rw_lo = (rw - rw_hi.astype(F32)).astype(BF16)
    h_hi = h.astype(BF16)
    h_lo = (h - h_hi.astype(F32)).astype(BF16)
    nt = (((1,), (1,)), ((), ()))
    logits_t = (lax.dot_general(rw_hi, h_hi, nt, preferred_element_type=F32)
                + lax.dot_general(rw_hi, h_lo, nt, preferred_element_type=F32)
                + lax.dot_general(rw_lo, h_hi, nt, preferred_element_type=F32))
    cls, w_lo, w_hi = _route(logits_t, rb_ref[...])

    @pl.when(jnp.logical_and(pl.program_id(0) == 0, pl.program_id(1) == 0))
    def _():
        run_scr[...] = jnp.zeros_like(run_scr)

    n = cls.shape[1]
    crow = lax.broadcasted_iota(jnp.int32, (CLS_PAD, n), 0).astype(F32)
    onehot = jnp.where(crow == cls, 1.0, 0.0)
    si = lax.broadcasted_iota(jnp.int32, (n, n), 0)
    ti = lax.broadcasted_iota(jnp.int32, (n, n), 1)
    before = jnp.where(si < ti, 1.0, 0.0).astype(BF16)
    cum = jnp.dot(onehot.astype(BF16), before, preferred_element_type=F32)
    run = run_scr[...]
    rank = jnp.sum(onehot * (cum + run[:, 0:1]), axis=0, keepdims=True)
    run = run + jnp.sum(onehot, axis=1, keepdims=True)
    run_scr[...] = run
    cnt_ref[...] = run
    info_ref[...] = jnp.zeros_like(info_ref)
    info_ref[0:1, :] = cls
    info_ref[1:2, :] = rank
    wt_scr[...] = jnp.zeros_like(wt_scr)
    wt_scr[0:1, :] = w_lo
    wt_scr[1:2, :] = w_hi
    h_ref[:, d:] = wt_scr[...].T


def _outproj(ya, yb, xres, modarr, w_out, ln_g, ln_b, rwt, rbias, row_blk0, gdn=None):
    bsz, n, wa = ya.shape

    def tok(width, col_blk=0):
        return pl.BlockSpec((None, TM, width), lambda b, j: (b, j, col_blk))

    if gdn is None:
        d = xres.shape[2]
        body, mix_args = _outproj_kernel, (ya, yb, xres)
        mix_specs = [tok(wa), tok(yb.shape[2]),
                     pl.BlockSpec((None, TM, d), lambda b, j: (b, j + row_blk0, 0))]
    else:
        o_fwd, o_rev, p, b_norm = gdn
        ctx, x = xres
        d = x.shape[2]
        gate_blk = (3 * A_WIDTH + 3 * B_WIDTH) // B_WIDTH
        body, mix_args = _outproj_even_kernel, (ya, o_fwd, o_rev, p, b_norm, ctx, x)
        mix_specs = [tok(wa), tok(B_WIDTH), tok(B_WIDTH), tok(B_WIDTH, gate_blk),
                     pl.BlockSpec((1, B_HEAD_DIM), lambda b, j: (0, 0)),
                     pl.BlockSpec((None, TM, d), lambda b, j: (b, 0, 0)),
                     pl.BlockSpec((None, TM, d), lambda b, j: (b, jnp.maximum(j - 1, 0), 0))]
    return pl.pallas_call(
        body,
        grid=(bsz, n // TM),
        in_specs=mix_specs + [
            pl.BlockSpec((None, None, 6, d), lambda b, j: (b, jnp.minimum(j + row_blk0, 1), 0, 0)),
            pl.BlockSpec(w_out.shape, lambda b, j: (0, 0)),
            pl.BlockSpec((1, d), lambda b, j: (0, 0)),
            pl.BlockSpec((1, d), lambda b, j: (0, 0)),
            pl.BlockSpec((N_EXPERTS, d), lambda b, j: (0, 0)),
            pl.BlockSpec((N_EXPERTS, 1), lambda b, j: (0, 0)),
        ],
        out_specs=[
            pl.BlockSpec((None, TM, d), lambda b, j: (b, j, 0)),
            pl.BlockSpec((None, TM, d + LANES), lambda b, j: (b, j, 0)),
            pl.BlockSpec((None, None, 8, TM), lambda b, j: (b, j, 0, 0)),
            pl.BlockSpec((CLS_PAD, LANES), lambda b, j: (0, 0)),
        ],
        out_shape=[
            jax.ShapeDtypeStruct((bsz, n, d), F32),
            jax.ShapeDtypeStruct((bsz, n, d + LANES), F32),
            jax.ShapeDtypeStruct((bsz, n // TM, 8, TM), F32),
            jax.ShapeDtypeStruct((CLS_PAD, LANES), F32),
        ],
        scratch_shapes=[pltpu.VMEM((CLS_PAD, LANES), F32),
                        pltpu.VMEM((LANES, TM), F32)],
        compiler_params=_cparams(2),
        name="outproj",
    )(*mix_args, modarr, w_out, ln_g, ln_b, rwt, rbias)


SCATTER_ROWS = 4096


def _row_scatter_kernel(dst_ref, src_ref, init_hbm, out_hbm, sem):
    del init_hbm
    rows = src_ref.shape[0]

    def body(g, carry):
        r0 = pl.multiple_of(g * 8, 8)
        for u in range(8):
            pltpu.make_async_copy(src_ref.at[pl.ds(r0 + u, 1)],
                                  out_hbm.at[pl.ds(dst_ref[0, r0 + u], 1)], sem).start(priority=u % 2)
        return carry

    lax.fori_loop(0, rows // 8, body, 0)
    pltpu.make_async_copy(src_ref, out_hbm.at[pl.ds(0, rows)], sem).wait()


def _row_scatter(src, dest, init):
    n, width = src.shape
    n_out = init.shape[0]
    rows = math.gcd(n, SCATTER_ROWS)
    return pl.pallas_call(
        _row_scatter_kernel,
        grid=(n // rows,),
        in_specs=[
            pl.BlockSpec((None, 1, rows), lambda j: (j, 0, 0), memory_space=pltpu.SMEM),
            pl.BlockSpec((rows, width), lambda j: (j, 0)),
            pl.BlockSpec(memory_space=pl.ANY),
        ],
        out_specs=pl.BlockSpec(memory_space=pl.ANY),
        out_shape=jax.ShapeDtypeStruct((n_out, width), src.dtype),
        scratch_shapes=[pltpu.SemaphoreType.DMA(())],
        input_output_aliases={2: 0},
        compiler_params=_cparams(1),
        name="row_scatter",
    )(dest.reshape(n // rows, 1, rows), src, init)


def _moe_kernel(elo_ref, ehi_ref, nused_ref, x_ref, wg_lo, wu_lo, wd_lo, wg_hi, wu_hi, wd_hi, o_ref):
    used = pl.program_id(0) < nused_ref[0]
    d = o_ref.shape[1]

    @pl.when(used)
    def _():
        x = x_ref[:, 0:d].astype(BF16)
        acc = None
        for col, (wg, wu, wd) in enumerate(((wg_lo, wu_lo, wd_lo), (wg_hi, wu_hi, wd_hi))):
            gate = jnp.dot(x, wg[...], preferred_element_type=F32)
            up = jnp.dot(x, wu[...], preferred_element_type=F32)
            act = (_silu(gate) * up).astype(BF16)
            y = x_ref[:, d + col:d + col + 1] * jnp.dot(act, wd[...], preferred_element_type=F32)
            acc = y if acc is None else acc + y
        o_ref[...] = acc

    @pl.when(jnp.logical_not(used))
    def _():
        o_ref[...] = jnp.zeros_like(o_ref)


def _moe_plan(info, counts, n_tiles):
    mt = MOE_TM
    cls = info[:, :, 0, :].reshape(-1).astype(jnp.int32)
    rank = info[:, :, 1, :].reshape(-1).astype(jnp.int32)
    cnt = counts[:N_CLASSES, 0].astype(jnp.int32)
    padded = ((cnt + mt - 1) // mt) * mt
    ends = jnp.cumsum(padded)
    starts = ends - padded
    classes = jnp.arange(N_CLASSES, dtype=jnp.int32)
    dest = jnp.sum(jnp.where(cls[:, None] == classes[None, :], starts[None, :], 0), axis=1) + rank
    n_used = ends[-1] // mt
    tidx = jnp.arange(n_tiles, dtype=jnp.int32)
    tidx = jnp.minimum(tidx, n_used - 1)
    tcls = jnp.sum((ends[None, :] <= (tidx * mt)[:, None]).astype(jnp.int32), axis=1)
    tcls = jnp.minimum(tcls, N_CLASSES - 1)
    pairs = [(a, b) for a in range(PER_GROUP) for b in range(a + 1, PER_GROUP)]
    pair = tcls % N_PAIRS
    lo = sum(jnp.where(pair == k, a, 0) for k, (a, _) in enumerate(pairs))
    hi = sum(jnp.where(pair == k, b, 0) for k, (_, b) in enumerate(pairs))
    group = tcls // N_PAIRS
    return (group * PER_GROUP + lo, group * PER_GROUP + hi, n_used.reshape(1).astype(jnp.int32),
            dest.astype(jnp.int32))


def _moe(h_ext, info, counts, wg, wu, wd, spare=None):
    n, width = h_ext.shape
    d = width - LANES
    ne, _, de = wg.shape
    mt = MOE_TM
    n_tiles = n // mt + N_CLASSES
    if spare is None:
        spare = jnp.zeros((n_tiles * mt, width), h_ext.dtype)
    assert spare.shape[0] % mt == 0 and spare.shape[0] >= n_tiles * mt and spare.shape[1] == width
    n_tiles = spare.shape[0] // mt
    elo, ehi, n_used, dest = _moe_plan(info, counts, n_tiles)
    h_sorted = _row_scatter(h_ext, dest, spare)

    def expert(which, shape):
        if which == 0:
            return pl.BlockSpec((None,) + shape, lambda i, lo, hi, nu: (lo[i], 0, 0))
        return pl.BlockSpec((None,) + shape, lambda i, lo, hi, nu: (hi[i], 0, 0))

    grid_spec = pltpu.PrefetchScalarGridSpec(
        num_scalar_prefetch=3,
        grid=(n_tiles,),
        in_specs=[
            pl.BlockSpec((mt, width), lambda i, lo, hi, nu: (jnp.minimum(i, nu[0] - 1), 0)),
            expert(0, (d, de)), expert(0, (d, de)), expert(0, (de, d)),
            expert(1, (d, de)), expert(1, (d, de)), expert(1, (de, d)),
        ],
        out_specs=pl.BlockSpec((mt, d), lambda i, *_: (i, 0)),
    )
    f_sorted = pl.pallas_call(
        _moe_kernel,
        grid_spec=grid_spec,
        out_shape=jax.ShapeDtypeStruct((n_tiles * mt, d), F32),
        compiler_params=_cparams(1),
        name="moe",
    )(elo, ehi, n_used, h_sorted, wg, wu, wd, wg, wu, wd)
    return f_sorted, dest, h_sorted


def _ln2_kernel(cur_ref, nxt_ref, x_ref, f_hbm, mod_ref, lng_ref, lnb_ref, o_ref, fbuf, sem):
    o_ref[...] = _gathered_ln2(cur_ref, nxt_ref, x_ref, f_hbm, mod_ref, lng_ref, lnb_ref, fbuf, sem)


def _gathered_ln2(cur_ref, nxt_ref, x_ref, f_hbm, mod_ref, lng_ref, lnb_ref, fbuf, sem):
    nj = pl.num_programs(1)
    step = pl.program_id(0) * nj + pl.program_id(1)
    n_steps = pl.num_programs(0) * nj
    slot = step % 2
    rows = fbuf.shape[1]

    def gather_start(idx_ref, s):
        def body(g, carry):
            r0 = pl.multiple_of(g * 8, 8)
            for u in range(8):
                pltpu.make_async_copy(f_hbm.at[pl.ds(idx_ref[0, r0 + u], 1)],
                                      fbuf.at[s, pl.ds(r0 + u, 1)], sem.at[s]).start(priority=u % 2)
            return carry
        lax.fori_loop(0, rows // 8, body, 0)

    @pl.when(step == 0)
    def _():
        gather_start(cur_ref, 0)

    def slot_wait(s):
        pltpu.make_async_copy(f_hbm.at[pl.ds(0, rows)], fbuf.at[s], sem.at[s]).wait()

    slot_wait(slot)
    for r in range(rows):
        pltpu.make_async_copy(f_hbm.at[pl.ds(nxt_ref[0, r], 1)], fbuf.at[1 - slot, pl.ds(r, 1)],
                              sem.at[1 - slot]).start(priority=r % 2)
    mod = mod_ref[...]
    v = DEEPNORM_ALPHA * x_ref[...] + mod[5:6] * fbuf[slot]
    out = _layer_norm(v, lng_ref[...], lnb_ref[...])

    @pl.when(step == n_steps - 1)
    def _():
        slot_wait(1 - slot)

    return out


def _ln2(x, f_sorted, dest, modarr, ln_g, ln_b, kind0):
    bsz, n, d = x.shape
    nj = n // TM
    n_steps = bsz * nj

    def idx_rows(offset):
        return pl.BlockSpec((None, 1, TM), lambda b, j: (jnp.minimum(b * nj + j + offset, n_steps - 1), 0, 0),
                            memory_space=pltpu.SMEM)

    dest3 = dest.reshape(n_steps, 1, TM)
    return pl.pallas_call(
        _ln2_kernel,
        grid=(bsz, nj),
        in_specs=[
            idx_rows(0), idx_rows(1),
            pl.BlockSpec((None, TM, d), lambda b, j: (b, j, 0)),
            pl.BlockSpec(memory_space=pl.ANY),
            pl.BlockSpec((None, None, 6, d), lambda b, j: (b, jnp.minimum(j + kind0, 1), 0, 0)),
            pl.BlockSpec((1, d), lambda b, j: (0, 0)),
            pl.BlockSpec((1, d), lambda b, j: (0, 0)),
        ],
        out_specs=pl.BlockSpec((None, TM, d), lambda b, j: (b, j, 0)),
        out_shape=jax.ShapeDtypeStruct((bsz, n, d), F32),
        scratch_shapes=[pltpu.VMEM((2, TM, d), F32), pltpu.SemaphoreType.DMA((2,))],
        compiler_params=_cparams(2),
        name="ln2",
    )(dest3, dest3, x, f_sorted, modarr, ln_g, ln_b)


def _rope(x, c, s1, s2, shift):
    w = x.shape[1]
    return x * c + pltpu.roll(x, w - shift, 1) * s1 + pltpu.roll(x, shift, 1) * s2


def _rms(x, g):
    return x * lax.rsqrt(jnp.mean(x * x, axis=-1, keepdims=True) + RMS_EPS) * g


def _ln2_inproj_odd_kernel(cur_ref, nxt_ref, x1_ref, f_hbm, mod0_ref, lng_ref, lnb_ref,
                           mod_ref, w_ref, qn_ref, kvn_ref, wuq_ref, wk_ref, we_ref, wv_ref,
                           tw_ref, tq_ref, tk_ref,
                           x2_ref, qw_ref, kw_ref, vw_ref, qm_ref, km_ref, vm_ref, fbuf, sem):
    x2 = _gathered_ln2(cur_ref, nxt_ref, x1_ref, f_hbm, mod0_ref, lng_ref, lnb_ref, fbuf, sem)
    x2_ref[...] = x2
    j = pl.program_id(1)
    is_ctx = j == 0
    mod = mod_ref[...]
    h = (x2 * (1.0 + mod[1:2]) + mod[0:1]).astype(BF16)
    p = jnp.dot(h, w_ref[...], preferred_element_type=F32)

    def tables(t_ref):
        c = jnp.where(is_ctx, 1.0, t_ref[0])
        s1 = jnp.where(is_ctx, 0.0, t_ref[1])
        s2 = jnp.where(is_ctx, 0.0, t_ref[2])
        return c, s1, s2

    cw, s1w, s2w = tables(tw_ref)
    nq = C_Q_HEADS * C_HEAD_DIM
    for r in range(nq // LANES):
        blk = _rope(p[:, r * LANES:(r + 1) * LANES], cw, s1w, s2w, C_HEAD_DIM // 2)
        qw_ref[:, r * LANES:(r + 1) * LANES] = (blk * (C_HEAD_DIM ** -0.5)).astype(BF16)
    kw_ref[...] = _rope(p[:, nq:nq + LANES], cw, s1w, s2w, C_HEAD_DIM // 2).astype(BF16)
    vw_ref[...] = p[:, nq + LANES:nq + 2 * LANES].astype(BF16)

    o = nq + 2 * LANES
    dq = _rms(p[:, o:o + D_Q_RANK], qn_ref[...]).astype(BF16)
    o += D_Q_RANK
    dkv = _rms(p[:, o:o + D_KV_RANK], kvn_ref[...]).astype(BF16)
    o += D_KV_RANK
    cq, s1q, s2q = tables(tq_ref)
    ck, s1k, s2k = tables(tk_ref)
    krope = _rope(p[:, o:o + LANES], ck, s1k, s2k, D_ROPE // 2).astype(BF16)
    scale = (D_NOPE + D_ROPE) ** -0.5
    q_all = jnp.dot(dq, wuq_ref[...], preferred_element_type=F32)
    for hh in range(D_HEADS):
        sl = slice(hh * LANES, (hh + 1) * LANES)
        qm_ref[:, sl] = (_rope(q_all[:, sl], cq, s1q, s2q, D_ROPE // 2) * scale).astype(BF16)
    km_ref[...] = (jnp.dot(dkv, wk_ref[...], preferred_element_type=F32)
                   + jnp.dot(krope, we_ref[...], preferred_element_type=F32)).astype(BF16)
    vm_ref[...] = jnp.dot(dkv, wv_ref[...], preferred_element_type=F32).astype(BF16)


def _ln2_inproj_odd(x1, f_sorted, dest, mod0, ln_g, ln_b, modarr, w1, qnorm, kvnorm, wuq, wk, we, wv,
                    tab_w, tab_q, tab_k):
    bsz, t, d = x1.shape
    hw = D_HEADS * LANES
    nj = t // TM
    n_steps = bsz * nj

    def idx_rows(offset):
        return pl.BlockSpec((None, 1, TM), lambda b, j: (jnp.minimum(b * nj + j + offset, n_steps - 1), 0, 0),
                            memory_space=pltpu.SMEM)

    dest3 = dest.reshape(n_steps, 1, TM)

    def tab_spec():
        return pl.BlockSpec((3, TM, LANES), lambda b, j: (0, jnp.maximum(j - 1, 0), 0))

    def full(a):
        return pl.BlockSpec(a.shape, lambda b, j: (0,) * a.ndim)

    def out(width):
        return (pl.BlockSpec((None, TM, width), lambda b, j: (b, j, 0)),
                jax.ShapeDtypeStruct((bsz, t, width), BF16))

    def out_latent(width):
        return (pl.BlockSpec((None, TM, width), lambda b, j: (b, jnp.maximum(j - 1, 0), 0)),
                jax.ShapeDtypeStruct((bsz, t - TM, width), BF16))

    x2_out = (pl.BlockSpec((None, TM, d), lambda b, j: (b, j, 0)), jax.ShapeDtypeStruct((bsz, t, d), F32))
    outs = (x2_out, out_latent(C_Q_HEADS * C_HEAD_DIM), out(LANES), out(LANES), out_latent(hw),
            out(hw), out(hw))

    def mod_spec():
        return pl.BlockSpec((None, None, 6, d), lambda b, j: (b, jnp.minimum(j, 1), 0, 0))

    return pl.pallas_call(
        _ln2_inproj_odd_kernel,
        grid=(bsz, nj),
        in_specs=[
            idx_rows(0), idx_rows(1),
            pl.BlockSpec((None, TM, d), lambda b, j: (b, j, 0)),
            pl.BlockSpec(memory_space=pl.ANY),
            mod_spec(), full(ln_g), full(ln_b), mod_spec(),
            full(w1), full(qnorm), full(kvnorm), full(wuq), full(wk), full(we), full(wv),
            tab_spec(), tab_spec(), tab_spec(),
        ],
        out_specs=[o[0] for o in outs],
        out_shape=[o[1] for o in outs],
        scratch_shapes=[pltpu.VMEM((2, TM, d), F32), pltpu.SemaphoreType.DMA((2,))],
        compiler_params=_cparams(2),
        name="ln2_inproj_odd",
    )(dest3, dest3, x1, f_sorted, mod0, ln_g, ln_b, modarr, w1, qnorm, kvnorm, wuq, wk, we, wv,
      tab_w, tab_q, tab_k)


WIN_TQ = 256


def _win_kernel(sink_ref, q_ref, k_ref, v_ref, o_ref, klo_scr, khi_scr, *, n_ctx):
    i = pl.program_id(1)
    wdw = C_WINDOW
    t = k_ref.shape[0]
    lane = lax.broadcasted_iota(jnp.int32, (t, LANES), 1)

    @pl.when(i == 0)
    def _():
        kk = k_ref[...]
        klo_scr[...] = jnp.where(lane < C_HEAD_DIM, kk, jnp.zeros_like(kk))
        khi_scr[...] = jnp.where(lane >= C_HEAD_DIM, kk, jnp.zeros_like(kk))

    tq = q_ref.shape[0]
    span = tq + 2 * wdw
    n_lat_blk = (t - n_ctx) // wdw
    blk0 = jnp.clip(i * (tq // wdw) - 1, 0, n_lat_blk - span // wdw)
    r0 = pl.multiple_of(n_ctx + blk0 * wdw, wdw)
    kpos = blk0 * wdw + lax.broadcasted_iota(jnp.int32, (tq, span), 1)
    qpos = i * tq + lax.broadcasted_iota(jnp.int32, (tq, span), 0)
    near = jnp.abs(kpos - qpos) <= wdw
    v_loc = v_ref[pl.ds(r0, span), :]
    v_ctx = v_ref[0:n_ctx, :]
    olane = lax.broadcasted_iota(jnp.int32, (tq, LANES), 1)
    n_rep = C_Q_HEADS // C_KV_HEADS
    for r in range(n_rep):
        q = q_ref[:, r * LANES:(r + 1) * LANES]
        outs = []
        for g, k_scr in enumerate((klo_scr, khi_scr)):
            k_loc = k_scr[pl.ds(r0, span), :]
            k_ctx = k_scr[0:n_ctx, :]
            s_loc = lax.dot_general(q, k_loc, (((1,), (1,)), ((), ())), preferred_element_type=F32)
            s_loc = jnp.where(near, s_loc, -jnp.inf)
            s_ctx = lax.dot_general(q, k_ctx, (((1,), (1,)), ((), ())), preferred_element_type=F32)
            sink = sink_ref[g * n_rep + r]
            m = jnp.maximum(jnp.maximum(jnp.max(s_loc, axis=-1, keepdims=True),
                                        jnp.max(s_ctx, axis=-1, keepdims=True)), sink)
            p_loc = jnp.exp(s_loc - m)
            p_ctx = jnp.exp(s_ctx - m)
            den = (jnp.sum(p_loc, axis=-1, keepdims=True) + jnp.sum(p_ctx, axis=-1, keepdims=True)
                   + jnp.exp(sink - m))
            pv = (jnp.dot(p_loc.astype(BF16), v_loc, preferred_element_type=F32)
                  + jnp.dot(p_ctx.astype(BF16), v_ctx, preferred_element_type=F32))
            outs.append(pv / den)
        o_ref[:, r * LANES:(r + 1) * LANES] = jnp.where(olane < C_HEAD_DIM, outs[0], outs[1]).astype(BF16)


def _win_attention(sink, qw, kw, vw, n_ctx):
    bsz, n_lat, nq = qw.shape
    t = n_ctx + n_lat
    grid_spec = pltpu.PrefetchScalarGridSpec(
        num_scalar_prefetch=1,
        grid=(bsz, n_lat // WIN_TQ),
        in_specs=[
            pl.BlockSpec((None, WIN_TQ, nq), lambda b, i, s: (b, i, 0)),
            pl.BlockSpec((None, t, LANES), lambda b, i, s: (b, 0, 0)),
            pl.BlockSpec((None, t, LANES), lambda b, i, s: (b, 0, 0)),
        ],
        out_specs=pl.BlockSpec((None, WIN_TQ, nq), lambda b, i, s: (b, i, 0)),
        scratch_shapes=[pltpu.VMEM((t, LANES), BF16), pltpu.VMEM((t, LANES), BF16)],
    )
    return pl.pallas_call(
        functools.partial(_win_kernel, n_ctx=n_ctx),
        grid_spec=grid_spec,
        out_shape=jax.ShapeDtypeStruct((bsz, n_lat, nq), BF16),
        compiler_params=_cparams(2),
        name="win_attention",
    )(sink, qw, kw, vw)


MLA_TQ = 512


def _mla_kernel(q_ref, k_ref, v_ref, o_ref):
    for hp in range(D_HEADS // 2):
        acc = None
        for hh in (2 * hp, 2 * hp + 1):
            sl = slice(hh * LANES, (hh + 1) * LANES)
            s = lax.dot_general(q_ref[:, sl], k_ref[:, sl], (((1,), (1,)), ((), ())),
                                preferred_element_type=F32)
            m = jnp.max(s, axis=-1, keepdims=True)
            p = jnp.exp(s - m)
            den = jnp.sum(p, axis=-1, keepdims=True)
            pv = jnp.dot(p.astype(BF16), v_ref[:, sl], preferred_element_type=F32) / den
            acc = pv if acc is None else acc + pv
        o_ref[:, hp * LANES:(hp + 1) * LANES] = acc.astype(BF16)


def _mla_attention(qm, km, vm, n_ctx):
    bsz, n_lat, hw = qm.shape
    t = n_ctx + n_lat
    ow = D_HEADS * D_V
    return pl.pallas_call(
        _mla_kernel,
        grid=(bsz, n_lat // MLA_TQ),
        in_specs=[
            pl.BlockSpec((None, MLA_TQ, hw), lambda b, i: (b, i, 0)),
            pl.BlockSpec((None, t, hw), lambda b, i: (b, 0, 0)),
            pl.BlockSpec((None, t, hw), lambda b, i: (b, 0, 0)),
        ],
        out_specs=pl.BlockSpec((None, MLA_TQ, ow), lambda b, i: (b, i, 0)),
        out_shape=jax.ShapeDtypeStruct((bsz, n_lat, ow), BF16),
        compiler_params=_cparams(2),
        name="mla_attention",
    )(qm, km, vm)


def _rope_tables(n_tokens, rot_dim, group, offset):
    t = jnp.arange(n_tokens)
    rows = (t // GRID_W).astype(F32)
    cols = (t % GRID_W).astype(F32)
    n_freq = rot_dim // 4
    inv_freq = ROPE_BASE ** (-jnp.arange(n_freq, dtype=F32) / n_freq)
    ang = jnp.concatenate([rows[:, None] * inv_freq, cols[:, None] * inv_freq], -1)
    cos, sin = jnp.cos(ang), jnp.sin(ang)
    half = rot_dim // 2
    c = jnp.ones((n_tokens, LANES), F32)
    s1 = jnp.zeros((n_tokens, LANES), F32)
    s2 = jnp.zeros((n_tokens, LANES), F32)
    for start in range(offset, LANES, group):
        c = c.at[:, start:start + half].set(cos).at[:, start + half:start + rot_dim].set(cos)
        s1 = s1.at[:, start:start + half].set(-sin)
        s2 = s2.at[:, start + half:start + rot_dim].set(sin)
    return jnp.stack([c, s1, s2])


def _odd_weights(w_in, wuq, wukv, w_out):
    d = w_in.shape[0]
    nq = C_Q_HEADS * C_HEAD_DIM
    nkv = C_KV_HEADS * C_HEAD_DIM
    n_rep = C_Q_HEADS // C_KV_HEADS
    order = [g * n_rep + r for r in range(n_rep) for g in range(C_KV_HEADS)]
    cq = w_in[:, :nq].reshape(d, C_Q_HEADS, C_HEAD_DIM)[:, order].reshape(d, nq)
    rest = w_in[:, nq:nq + 2 * nkv + D_Q_RANK + D_KV_RANK]
    krope = jnp.pad(w_in[:, nq + 2 * nkv + D_Q_RANK + D_KV_RANK:], ((0, 0), (0, LANES - D_ROPE)))
    w1 = jnp.concatenate([cq, rest, krope], axis=1).astype(BF16)
    qh = wuq.reshape(D_Q_RANK, D_HEADS, D_NOPE + D_ROPE)
    wuq_p = jnp.pad(qh, ((0, 0), (0, 0), (0, LANES - D_NOPE - D_ROPE))).reshape(D_Q_RANK, D_HEADS * LANES)
    kvh = wukv.reshape(D_KV_RANK, D_HEADS, D_NOPE + D_V)
    wk_p = jnp.pad(kvh[:, :, :D_NOPE], ((0, 0), (0, 0), (0, LANES - D_NOPE))).reshape(D_KV_RANK, D_HEADS * LANES)
    e_blk = jnp.zeros((LANES, LANES), F32).at[jnp.arange(D_ROPE), D_NOPE + jnp.arange(D_ROPE)].set(1.0)
    we = jnp.tile(e_blk, (1, D_HEADS))
    vh = kvh[:, :, D_NOPE:]
    even = (jnp.arange(D_HEADS) % 2 == 0)[None, :, None]
    wv_p = jnp.where(even, jnp.pad(vh, ((0, 0), (0, 0), (0, D_V))),
                     jnp.pad(vh, ((0, 0), (0, 0), (D_V, 0)))).reshape(D_KV_RANK, D_HEADS * LANES)
    wo_win = w_out[:nq].reshape(C_Q_HEADS, C_HEAD_DIM, -1)[jnp.array(order)].reshape(nq, -1)
    wo = jnp.concatenate([wo_win, w_out[nq:]], axis=0).astype(BF16)
    return w1, wuq_p.astype(BF16), wk_p.astype(BF16), we.astype(BF16), wv_p.astype(BF16), wo


def kernel(x, c, ctx, c_ctx, ada_w, ada_b, ln_g, ln_b, ev_w_in, ev_a_conv, ev_b_conv, ev_b_alog, ev_b_dtbias, ev_b_norm, ev_w_out, od_w_in, od_c_sink, od_d_qnorm, od_d_kvnorm, od_d_wuq, od_d_wukv, od_w_out, router_w, router_bias, moe_w_gate, moe_w_up, moe_w_down):
    bsz, n_lat, d = x.shape
    n_ctx = ctx.shape[1]
    assert n_ctx == TM and n_lat % TM == 0 and n_lat % GRID_W == 0
    assert ada_w.shape[0] == DEPTH and bsz + 1 <= 40
    t = n_ctx + n_lat

    cs = jnp.zeros((40, d), F32).at[:bsz].set(c).at[bsz].set(c_ctx)
    mods = _ada_mod(cs, ada_w, ada_b)

    def modarr(layer):
        m = mods[layer].reshape(40, 6, d)
        return jnp.stack([jnp.broadcast_to(m[bsz], (bsz, 6, d)), m[:bsz]], axis=1)

    rwt = router_w.T
    rbias = router_bias.reshape(N_EXPERTS, 1)

    mod0 = modarr(0)
    n_main = 3 * A_WIDTH + 4 * B_WIDTH
    w_main = ev_w_in[0][:, :n_main].astype(BF16)
    w_small = jnp.pad(ev_w_in[0][:, n_main:], ((0, 0), (0, LANES - 4 * B_HEADS))).astype(BF16)
    p, small = _inproj_even(ctx, x, mod0, w_main, w_small)
    alog_pad = jnp.zeros((1, LANES), F32).at[0, 8:16].set(ev_b_alog[0].reshape(-1))
    dtb_pad = jnp.zeros((1, LANES), F32).at[0, 8:16].set(ev_b_dtbias[0].reshape(-1))
    ya, u, w, qe, ket, att, dec = _even_prep(p, small, ev_a_conv[0], ev_b_conv[0], alog_pad, dtb_pad,
                                             GDN_CHUNK)
    o_fwd, o_rev = _gdn_rec(u, w, qe, ket, att, dec, GDN_CHUNK)
    x1, h1, info0, cnt0 = _outproj(ya, None, (ctx, x), mod0, ev_w_out[0].astype(BF16),
                             ln_g[0, 0].reshape(1, d), ln_b[0, 0].reshape(1, d), rwt, rbias, 0,
                             gdn=(o_fwd, o_rev, p, ev_b_norm[0].reshape(1, B_HEAD_DIM)))
    f, dest, h_sorted0 = _moe(h1.reshape(bsz * t, d + LANES), info0, cnt0, moe_w_gate[0].astype(BF16),
                              moe_w_up[0].astype(BF16), moe_w_down[0].astype(BF16))
    mod1 = modarr(1)
    w1, wuq_p, wk_p, we, wv_p, wo = _odd_weights(od_w_in[0], od_d_wuq[0], od_d_wukv[0], od_w_out[0])
    tab_w = _rope_tables(n_lat, C_HEAD_DIM, C_HEAD_DIM, 0)
    tab_q = _rope_tables(n_lat, D_ROPE, LANES, D_NOPE)
    tab_k = _rope_tables(n_lat, D_ROPE, LANES, 0)
    x2, qw, kw, vw, qm, km, vm = _ln2_inproj_odd(
        x1, f, dest, mod0, ln_g[0, 1].reshape(1, d), ln_b[0, 1].reshape(1, d),
        mod1, w1, od_d_qnorm[0].reshape(1, -1), od_d_kvnorm[0].reshape(1, -1),
        wuq_p, wk_p, we, wv_p, tab_w, tab_q, tab_k)
    y_win = _win_attention(od_c_sink[0], qw, kw, vw, n_ctx)
    y_mla = _mla_attention(qm, km, vm, n_ctx)
    x3, h3, info1, cnt1 = _outproj(y_win, y_mla, x2, mod1, wo, ln_g[1, 0].reshape(1, d),
                                   ln_b[1, 0].reshape(1, d), rwt, rbias, n_ctx // TM)
    f1, dest1, _ = _moe(h3.reshape(bsz * n_lat, d + LANES), info1, cnt1, moe_w_gate[1].astype(BF16),
                        moe_w_up[1].astype(BF16), moe_w_down[1].astype(BF16), spare=h_sorted0)
    return _ln2(x3, f1, dest1, mod1, ln_g[1, 1].reshape(1, d), ln_b[1, 1].reshape(1, d), 1)
```
